```python
import jax, jax.numpy as jnp
from jax import lax
import numpy as np

D_MODEL = 1024
BATCH = 8
SEQ = 4096
DEPTH = 4

GRID_W = 64
CTX_LEN = 256
N_MOD = 6
NORM_EPS = 1e-6
ATT_HEADS = 8
ATT_KV_HEADS = 2
HEAD_DIM = 64
WINDOW = 128
ATT_BLOCK = 128
ROPE_BASE = 10000.0
ROPE_PAIRS = HEAD_DIM // 4
ATT_Q = ATT_HEADS * HEAD_DIM
ATT_KV = ATT_KV_HEADS * HEAD_DIM
MLSTM_HEADS = 4
MLSTM_HEAD_DIM = 64
MLSTM_CHUNK = 64
M_W = MLSTM_HEADS * MLSTM_HEAD_DIM
M_GATES = 4 * MLSTM_HEADS
LRU_WIDTH = 256
LRU_BLOCKS = 4
LRU_BW = LRU_WIDTH // LRU_BLOCKS
LRU_C = 8.0
CONV_WIDTH = 4
CONV_LEFT = CONV_WIDTH // 2
D_IN = ATT_Q + 2 * ATT_KV + 4 * M_W + M_GATES + 2 * LRU_WIDTH
D_MIX = ATT_Q + M_W + LRU_WIDTH
D_FF = -(-8 * D_MODEL // (3 * 256)) * 256

kernel_name = "hybrid_prefix_dit_mlstm_rglru_swa"


def rms_norm(x, g):
    xf = x.astype(jnp.float32)
    y = xf * lax.rsqrt(jnp.mean(xf * xf, axis=-1, keepdims=True) + NORM_EPS)
    return (y * g.astype(jnp.float32)).astype(x.dtype)


def modulate(h, shift, scale):
    return h * (1 + scale) + shift


def flip_streams(a, n_ctx):
    return jnp.concatenate([a[:, :n_ctx][:, ::-1], a[:, n_ctx:][:, ::-1]], axis=1)


def rotate(x, cos, sin):
    x1, x2 = jnp.split(x, 2, axis=-1)
    cos, sin = cos.astype(x.dtype), sin.astype(x.dtype)
    return jnp.concatenate([x1 * cos - x2 * sin, x2 * cos + x1 * sin], axis=-1)


def rope_axial(x, rope):
    cos_r, sin_r, cos_c, sin_c = rope
    half = HEAD_DIM // 2
    return jnp.concatenate([rotate(x[..., :half], cos_r, sin_r), rotate(x[..., half:], cos_c, sin_c)], axis=-1)


def windowed_gqa(q_l, k_l, v_l, q_c, k_c, v_c, sink, rope, need_ctx):
    B, S, _ = q_l.shape
    n_ctx = k_c.shape[1]
    G = ATT_HEADS // ATT_KV_HEADS
    scale = HEAD_DIM ** -0.5
    f32 = jnp.float32
    q = rope_axial(q_l.reshape(B, S, ATT_HEADS, HEAD_DIM), rope) * scale
    k = rope_axial(k_l.reshape(B, S, ATT_KV_HEADS, HEAD_DIM), rope)
    v = v_l.reshape(B, S, ATT_KV_HEADS, HEAD_DIM)
    kc = k_c.reshape(B, n_ctx, ATT_KV_HEADS, HEAD_DIM)
    vc = v_c.reshape(B, n_ctx, ATT_KV_HEADS, HEAD_DIM)
    nb = S // ATT_BLOCK
    qb = q.reshape(B, nb, ATT_BLOCK, ATT_KV_HEADS, G, HEAD_DIM)

    def band(a):
        ap = jnp.pad(a, ((0, 0), (ATT_BLOCK, ATT_BLOCK), (0, 0), (0, 0)))
        ap = ap.reshape(B, nb + 2, ATT_BLOCK, ATT_KV_HEADS, HEAD_DIM)
        return jnp.concatenate([ap[:, :-2], ap[:, 1:-1], ap[:, 2:]], axis=2)

    kw, vw = band(k), band(v)
    blk = jnp.arange(nb)[:, None, None] * ATT_BLOCK
    qpos = blk + jnp.arange(ATT_BLOCK)[None, :, None]
    kpos = blk - ATT_BLOCK + jnp.arange(3 * ATT_BLOCK)[None, None, :]
    valid = (jnp.abs(qpos - kpos) <= WINDOW) & (kpos >= 0) & (kpos < S)
    s_loc = jnp.einsum('bnqkgd,bnskd->bnkgqs', qb, kw).astype(f32)
    s_loc = jnp.where(valid[None, :, None, None], s_loc, -jnp.inf)
    s_ctx = jnp.einsum('bnqkgd,bckd->bnkgqc', qb, kc).astype(f32)
    sink_h = sink.astype(f32).reshape(ATT_KV_HEADS, G, 1, 1)
    s_sink = jnp.broadcast_to(sink_h, s_loc.shape[:-1] + (1,))
    p = jax.nn.softmax(jnp.concatenate([s_loc, s_ctx, s_sink], axis=-1), axis=-1).astype(v.dtype)
    L = 3 * ATT_BLOCK
    o = (jnp.einsum('bnkgqs,bnskd->bnqkgd', p[..., :L], vw)
         + jnp.einsum('bnkgqc,bckd->bnqkgd', p[..., L:L + n_ctx], vc))
    out_l = o.reshape(B, S, ATT_Q)
    out_c = None
    if need_ctx:
        qc = q_c.reshape(B, n_ctx, ATT_KV_HEADS, G, HEAD_DIM) * scale
        s = jnp.einsum('bqkgd,bckd->bkgqc', qc, kc).astype(f32)
        s = jnp.concatenate([s, jnp.broadcast_to(sink_h, s.shape[:-1] + (1,))], axis=-1)
        pc = jax.nn.softmax(s, axis=-1).astype(vc.dtype)
        out_c = jnp.einsum('bkgqc,bckd->bqkgd', pc[..., :n_ctx], vc).reshape(B, n_ctx, ATT_Q)
    return out_l, out_c


def mlstm_chunkwise(q, k, v, log_i, log_f):
    B, T, H, dh = q.shape
    L = MLSTM_CHUNK
    N = T // L
    to_chunks = lambda a: jnp.moveaxis(a.reshape(B, N, L, H, -1), 3, 1)
    qc, kc, vc = to_chunks(q), to_chunks(k), to_chunks(v)
    li = jnp.moveaxis(log_i.reshape(B, N, L, H), 3, 1)
    lf = jnp.moveaxis(log_f.reshape(B, N, L, H), 3, 1)
    b = jnp.cumsum(lf, axis=-1)
    g = b[..., -1]
    a = g[..., None] - b + li
    m_loc = jnp.max(a, axis=-1)
    w = jnp.exp(a - m_loc[..., None])
    C_loc = jnp.einsum('bhnl,bhnld,bhnle->bhnde', w, kc, vc)
    n_loc = jnp.einsum('bhnl,bhnld->bhnd', w, kc)

    def step(carry, xs):
        C, n, m = carry
        g_j, m_j, C_j, n_j = xs
        m_new = jnp.maximum(g_j + m, m_j)
        f_prev = jnp.exp(g_j + m - m_new)
        f_loc = jnp.exp(m_j - m_new)
        C_new = f_prev[..., None, None] * C + f_loc[..., None, None] * C_j
        n_new = f_prev[..., None] * n + f_loc[..., None] * n_j
        return (C_new, n_new, m_new), (C, n, m)

    init = (jnp.zeros((B, H, dh, dh), q.dtype), jnp.zeros((B, H, dh), q.dtype), jnp.zeros((B, H), q.dtype))
    xs = (jnp.moveaxis(g, 2, 0), jnp.moveaxis(m_loc, 2, 0), jnp.moveaxis(C_loc, 2, 0), jnp.moveaxis(n_loc, 2, 0))
    _, (C0, n0, m0) = lax.scan(step, init, xs)
    C0, n0, m0 = jnp.moveaxis(C0, 0, 2), jnp.moveaxis(n0, 0, 2), jnp.moveaxis(m0, 0, 2)
    tril = jnp.tril(jnp.ones((L, L), dtype=bool))
    d_log = jnp.where(tril, b[..., :, None] - b[..., None, :] + li[..., None, :], -jnp.inf)
    m_inter = b + m0[..., None]
    m_t = jnp.maximum(jnp.max(d_log, axis=-1), m_inter)
    f_inter = jnp.exp(m_inter - m_t)
    s = jnp.einsum('bhntd,bhnsd->bhnts', qc, kc) * jnp.exp(d_log - m_t[..., None])
    num = (f_inter[..., None] * jnp.einsum('bhntd,bhnde->bhnte', qc, C0)
           + jnp.einsum('bhnts,bhnse->bhnte', s, vc))
    den = f_inter * jnp.einsum('bhntd,bhnd->bhnt', qc, n0) + jnp.sum(s, axis=-1)
    h = num / jnp.maximum(jnp.abs(den), jnp.exp(-m_t))[..., None]
    return jnp.moveaxis(h, 1, 3).reshape(B, T, H, dh)


def head_norm(h, g):
    mu = jnp.mean(h, axis=-1, keepdims=True)
    var = jnp.mean(jnp.square(h - mu), axis=-1, keepdims=True)
    y = (h - mu) * lax.rsqrt(var + NORM_EPS)
    return y.reshape(h.shape[0], h.shape[1], -1) * g.astype(jnp.float32)


def mlstm_mixer(q_l, k_l, v_l, o_l, g_l, q_c, k_c, v_c, o_c, g_c, gate_b, norm_g, need_ctx):
    n_ctx = q_c.shape[1]
    f32 = jnp.float32
    heads = lambda a: a.astype(f32).reshape(a.shape[0], a.shape[1], MLSTM_HEADS, MLSTM_HEAD_DIM)
    gates = lambda a: a.astype(f32).reshape(a.shape[0], a.shape[1], 4, MLSTM_HEADS) + gate_b.astype(f32)
    q = jnp.concatenate([heads(q_c), heads(q_l)], axis=1)
    k = jnp.concatenate([heads(k_c), heads(k_l)], axis=1) * (MLSTM_HEAD_DIM ** -0.5)
    v = jnp.concatenate([heads(v_c), heads(v_l)], axis=1)
    gt = jnp.concatenate([gates(g_c), gates(g_l)], axis=1)
    h_f = mlstm_chunkwise(q, k, v, gt[:, :, 0], jax.nn.log_sigmoid(gt[:, :, 1]))
    fl = lambda a: flip_streams(a, n_ctx)
    h_b = fl(mlstm_chunkwise(fl(q), fl(k), fl(v), fl(gt[:, :, 2]), fl(jax.nn.log_sigmoid(gt[:, :, 3]))))
    h = h_f + h_b
    finish = lambda hh, o: (jax.nn.sigmoid(o.astype(f32)) * head_norm(hh, norm_g)).astype(o.dtype)
    out_l = finish(h[:, n_ctx:], o_l)
    out_c = finish(h[:, :n_ctx], o_c) if need_ctx else None
    return out_l, out_c


def dwconv(x, w, b):
    T = x.shape[1]
    xp = jnp.pad(x, ((0, 0), (CONV_LEFT, CONV_WIDTH - 1 - CONV_LEFT), (0, 0)))
    return b + sum(w[j] * xp[:, j:j + T] for j in range(CONV_WIDTH))


def block_diag(x, w):
    B, T, _ = x.shape
    y = jnp.einsum('btni,nij->btnj', x.reshape(B, T, LRU_BLOCKS, LRU_BW), w)
    return y.reshape(B, T, LRU_WIDTH)


def linear_scan(a, u):
    def comb(lhs, rhs):
        al, ul = lhs
        ar, ur = rhs
        return al * ar, ar * ul + ur
    _, h = lax.associative_scan(comb, (a, u), axis=1)
    return h


def rglru_mixer(x_l, y_l, x_c, y_c, conv_w, conv_b, gate_w, gate_b, lam, need_ctx):
    n_ctx = x_c.shape[1]
    f32 = jnp.float32
    cw, cb = conv_w.astype(f32), conv_b.astype(f32)
    seq = jnp.concatenate([dwconv(x_c.astype(f32), cw, cb), dwconv(x_l.astype(f32), cw, cb)], axis=1)
    h = 0.0
    for d in range(2):
        r = jax.nn.sigmoid(block_diag(seq, gate_w[d, 0].astype(f32)) + gate_b[d, 0].astype(f32))
        i = jax.nn.sigmoid(block_diag(seq, gate_w[d, 1].astype(f32)) + gate_b[d, 1].astype(f32))
        log_a = -LRU_C * r * jax.nn.softplus(-lam[d].astype(f32))
        a = jnp.exp(log_a)
        u = jnp.sqrt(-jnp.expm1(2.0 * log_a)) * (i * seq)
        if d == 0:
            h = h + linear_scan(a, u)
        else:
            h = h + flip_streams(linear_scan(flip_streams(a, n_ctx), flip_streams(u, n_ctx)), n_ctx)
    out_l = (h[:, n_ctx:] * jax.nn.gelu(y_l.astype(f32))).astype(x_l.dtype)
    out_c = (h[:, :n_ctx] * jax.nn.gelu(y_c.astype(f32))).astype(x_c.dtype) if need_ctx else None
    return out_l, out_c


def token_mixers(p_l, p_c, rope, sink, m_gate_b, m_norm, conv_w, conv_b, lru_w, lru_b, lru_lam, need_ctx):
    sizes = [ATT_Q, ATT_KV, ATT_KV, M_W, M_W, M_W, M_W, M_GATES, LRU_WIDTH, LRU_WIDTH]
    pts = np.cumsum(sizes)[:-1].tolist()
    aq, ak, av, mq, mk, mv, mo, mg, rx, ry = jnp.split(p_l, pts, axis=-1)
    cq, ck, cv, cmq, cmk, cmv, cmo, cmg, crx, cry = jnp.split(p_c, pts, axis=-1)
    att_l, att_c = windowed_gqa(aq, ak, av, cq, ck, cv, sink, rope, need_ctx)
    mem_l, mem_c = mlstm_mixer(mq, mk, mv, mo, mg, cmq, cmk, cmv, cmo, cmg, m_gate_b, m_norm, need_ctx)
    rec_l, rec_c = rglru_mixer(rx, ry, crx, cry, conv_w, conv_b, lru_w, lru_b, lru_lam, need_ctx)
    out_l = jnp.concatenate([att_l, mem_l, rec_l], axis=-1)
    out_c = jnp.concatenate([att_c, mem_c, rec_c], axis=-1) if need_ctx else None
    return out_l, out_c


def swiglu(h, w_in, w_out):
    gate, up = jnp.split(h @ w_in, 2, axis=-1)
    return (jax.nn.silu(gate) * up) @ w_out


def setup_inputs(seed: int = 0) -> dict:
    key = jax.random.key(seed)
    ks = jax.random.split(key, 20)
    f32 = jnp.float32
    nrm = lambda k, shape, s: jax.random.normal(k, shape, f32) * s
    x = nrm(ks[0], (BATCH, SEQ, D_MODEL), 1.0)
    c = nrm(ks[1], (BATCH, D_MODEL), 1.0)
    ctx = nrm(ks[2], (BATCH, CTX_LEN, D_MODEL), 1.0)
    c_ctx = nrm(ks[3], (D_MODEL,), 1.0)
    w_ada = nrm(ks[4], (DEPTH, D_MODEL, N_MOD * D_MODEL), 0.5 * D_MODEL ** -0.5)
    b_ada = nrm(ks[5], (DEPTH, N_MOD * D_MODEL), 0.01)
    norm_gain = 1.0 + nrm(ks[6], (DEPTH, 4, D_MODEL), 0.05)
    w_in = nrm(ks[7], (DEPTH, D_MODEL, D_IN), D_MODEL ** -0.5)
    w_out = nrm(ks[8], (DEPTH, D_MIX, D_MODEL), D_MIX ** -0.5)
    attn_sink = nrm(ks[9], (DEPTH, ATT_HEADS), 0.5)
    i_bias = nrm(ks[10], (DEPTH, 2, MLSTM_HEADS), 0.1)
    f_bias = 3.0 + 3.0 * jax.random.uniform(ks[11], (DEPTH, 2, MLSTM_HEADS), f32)
    mlstm_gate_b = jnp.stack([i_bias[:, 0], f_bias[:, 0], i_bias[:, 1], f_bias[:, 1]], axis=1)
    mlstm_norm = 1.0 + nrm(ks[12], (DEPTH, M_W), 0.05)
    conv_w = nrm(ks[13], (DEPTH, CONV_WIDTH, LRU_WIDTH), CONV_WIDTH ** -0.5)
    conv_b = nrm(ks[14], (DEPTH, LRU_WIDTH), 0.01)
    lru_gate_w = nrm(ks[15], (DEPTH, 2, 2, LRU_BLOCKS, LRU_BW, LRU_BW), LRU_BW ** -0.5)
    lru_gate_b = nrm(ks[16], (DEPTH, 2, 2, LRU_WIDTH), 0.01)
    a_c = jax.random.uniform(ks[17], (DEPTH, 2, LRU_WIDTH), f32, 0.9, 0.999)
    p = a_c ** (1.0 / LRU_C)
    lru_lam = jnp.log(p) - jnp.log1p(-p)
    w_ffn_in = nrm(ks[18], (DEPTH, D_MODEL, 2 * D_FF), D_MODEL ** -0.5)
    w_ffn_out = nrm(ks[19], (DEPTH, D_FF, D_MODEL), D_FF ** -0.5)
    return {"x": x, "c": c, "ctx": ctx, "c_ctx": c_ctx, "w_ada": w_ada, "b_ada": b_ada,
            "norm_gain": norm_gain, "w_in": w_in, "w_out": w_out, "attn_sink": attn_sink,
            "mlstm_gate_b": mlstm_gate_b, "mlstm_norm": mlstm_norm, "conv_w": conv_w, "conv_b": conv_b,
            "lru_gate_w": lru_gate_w, "lru_gate_b": lru_gate_b, "lru_lam": lru_lam,
            "w_ffn_in": w_ffn_in, "w_ffn_out": w_ffn_out}


def reference(x, c, ctx, c_ctx, w_ada, b_ada, norm_gain, w_in, w_out, attn_sink, mlstm_gate_b, mlstm_norm,
              conv_w, conv_b, lru_gate_w, lru_gate_b, lru_lam, w_ffn_in, w_ffn_out):
    n_lat = x.shape[1]
    n_rows = n_lat // GRID_W
    t = jnp.arange(n_rows * GRID_W)
    row = (t // GRID_W).astype(jnp.float32)
    col = (t % GRID_W).astype(jnp.float32)
    freqs = ROPE_BASE ** (-jnp.arange(ROPE_PAIRS, dtype=jnp.float32) / ROPE_PAIRS)
    ang_r = (row[:, None] * freqs)[:, None, :]
    ang_c = (col[:, None] * freqs)[:, None, :]
    rope = (jnp.cos(ang_r), jnp.sin(ang_r), jnp.cos(ang_c), jnp.sin(ang_c))
    s_lat = jax.nn.silu(c)
    s_ctx = jax.nn.silu(c_ctx)
    for l in range(DEPTH):
        need_ctx = l < DEPTH - 1
        mod_l = jnp.split((s_lat @ w_ada[l] + b_ada[l])[:, None, :], N_MOD, axis=-1)
        mod_c = jnp.split(s_ctx @ w_ada[l] + b_ada[l], N_MOD, axis=-1)
        h_l = modulate(rms_norm(x, norm_gain[l, 0]), mod_l[0], mod_l[1])
        h_c = modulate(rms_norm(ctx, norm_gain[l, 0]), mod_c[0], mod_c[1])
        mix_l, mix_c = token_mixers(h_l @ w_in[l], h_c @ w_in[l], rope, attn_sink[l], mlstm_gate_b[l], mlstm_norm[l],
                                    conv_w[l], conv_b[l], lru_gate_w[l], lru_gate_b[l], lru_lam[l], need_ctx)
        x = x + mod_l[2] * rms_norm(mix_l @ w_out[l], norm_gain[l, 1])
        f_l = swiglu(modulate(rms_norm(x, norm_gain[l, 2]), mod_l[3], mod_l[4]), w_ffn_in[l], w_ffn_out[l])
        x = x + mod_l[5] * rms_norm(f_l, norm_gain[l, 3])
        if need_ctx:
            ctx = ctx + mod_c[2] * rms_norm(mix_c @ w_out[l], norm_gain[l, 1])
            f_c = swiglu(modulate(rms_norm(ctx, norm_gain[l, 2]), mod_c[3], mod_c[4]), w_ffn_in[l], w_ffn_out[l])
            ctx = ctx + mod_c[5] * rms_norm(f_c, norm_gain[l, 3])
    return x
```

```python
import functools

import jax
import jax.numpy as jnp
import numpy as np
from jax import lax
from jax.experimental import pallas as pl
from jax.experimental.pallas import tpu as pltpu

F32 = jnp.float32
BF16 = jnp.bfloat16

D_MODEL = 1024
GRID_W = 64
CTX_LEN = 256
N_MOD = 6
NORM_EPS = 1e-6
ATT_HEADS = 8
ATT_KV_HEADS = 2
ATT_GROUP = ATT_HEADS // ATT_KV_HEADS
HEAD_DIM = 64
WINDOW = 128
ATT_BLOCK = 128
ROPE_BASE = 10000.0
ROPE_PAIRS = HEAD_DIM // 4
ATT_Q = ATT_HEADS * HEAD_DIM
ATT_KV = ATT_KV_HEADS * HEAD_DIM
MLSTM_HEADS = 4
MLSTM_HEAD_DIM = 64
M_W = MLSTM_HEADS * MLSTM_HEAD_DIM
M_GATES = 4 * MLSTM_HEADS
LRU_WIDTH = 256
LRU_BLOCKS = 4
LRU_BW = LRU_WIDTH // LRU_BLOCKS
LRU_C = 8.0
CONV_WIDTH = 4
CONV_LEFT = CONV_WIDTH // 2
D_FF = -(-8 * D_MODEL // (3 * 256)) * 256

LANES = 128
SUBLANES = 8
TOKEN_TILE = CTX_LEN
FF_CHUNK = 256
MOD_ROWS = 16
VMEM_LIMIT = 56 * 1024 * 1024


def _params(sem):
    return pltpu.CompilerParams(dimension_semantics=sem, vmem_limit_bytes=VMEM_LIMIT)


def _dot(a, b):
    return jnp.dot(a, b, preferred_element_type=F32)


def _dot_nt(a, b):
    return lax.dot_general(a, b, (((1,), (1,)), ((), ())), preferred_element_type=F32)


def _dot_exact(a, b):
    return jnp.dot(a, b, preferred_element_type=F32, precision=lax.Precision.HIGHEST)


def _sigmoid(x):
    return 1.0 / (1.0 + jnp.exp(-x))


def _softplus(x):
    return jnp.maximum(x, 0.0) + jnp.log1p(jnp.exp(-jnp.abs(x)))


def _rms(x, g):
    return x * lax.rsqrt(jnp.mean(x * x, axis=-1, keepdims=True) + NORM_EPS) * g


def _mod_kernel(c_ref, w_ref, b_ref, o_ref):
    c = c_ref[...]
    s = (c * _sigmoid(c)).astype(BF16)
    o_ref[...] = _dot(s, w_ref[...].astype(BF16)) + b_ref[...]


def _modulation(cvec, w_ada, b_ada):
    depth, d, n = w_ada.shape
    tn = 1536
    return pl.pallas_call(
        _mod_kernel,
        grid=(depth, n // tn),
        in_specs=[
            pl.BlockSpec((MOD_ROWS, d), lambda l, j: (0, 0)),
            pl.BlockSpec((None, d, tn), lambda l, j: (l, 0, j)),
            pl.BlockSpec((None, 1, tn), lambda l, j: (l, 0, j)),
        ],
        out_specs=pl.BlockSpec((None, MOD_ROWS, tn), lambda l, j: (l, 0, j)),
        out_shape=jax.ShapeDtypeStruct((depth, MOD_ROWS, n), F32),
        compiler_params=_params(("arbitrary", "arbitrary")),
        name="modulation",
    )(cvec, w_ada, b_ada.reshape(depth, 1, n))


_IN_SPLITS = (("aq", ATT_Q), ("ak", ATT_KV), ("av", ATT_KV), ("mq", M_W), ("mk", M_W), ("mv", M_W), ("mo", M_W),
              ("rx", LRU_WIDTH), ("ry", LRU_WIDTH))
_GATE_COL0 = ATT_Q + 2 * ATT_KV + 4 * M_W
D_IN_MAIN = sum(w for _, w in _IN_SPLITS)


def _rope_slab(x, cs, sn, first):
    swapped = jnp.where(first, pltpu.roll(x, LANES - ROPE_PAIRS, 1), pltpu.roll(x, ROPE_PAIRS, 1))
    return x * cs + swapped * sn


def _inproj_kernel(x_ref, mod_ref, g_ref, w_ref, wg_ref, wgt_ref, cs_ref, sn_ref,
                   aq_ref, ak_ref, av_ref, mq_ref, mk_ref, mv_ref, mo_ref, rx_ref, ry_ref, gc_ref, gr_ref):
    x = x_ref[...]
    h = _rms(x, g_ref[0:1, :]) * (1.0 + mod_ref[1:2, :]) + mod_ref[0:1, :]
    hb = h.astype(BF16)
    cs = cs_ref[...]
    sn = sn_ref[...]
    lane = lax.broadcasted_iota(jnp.int32, cs.shape, 1)
    first = (lane & (2 * ROPE_PAIRS - 1)) < ROPE_PAIRS
    col = 0
    for i in range(ATT_Q // LANES):
        p = _dot(hb, w_ref[:, col:col + LANES])
        aq_ref[:, i * LANES:(i + 1) * LANES] = _rope_slab(p, cs, sn, first) * (HEAD_DIM ** -0.5)
        col += LANES
    p = _dot(hb, w_ref[:, col:col + LANES])
    ak_ref[...] = _rope_slab(p, cs, sn, first)
    col += LANES
    for ref in (av_ref, mq_ref, mk_ref, mv_ref, mo_ref, rx_ref, ry_ref):
        n = ref.shape[-1]
        ref[...] = _dot(hb, w_ref[:, col:col + n])
        col += n
    gc_ref[...] = _dot(hb, wg_ref[...])
    gr_ref[...] = _dot_nt(wgt_ref[...], hb)


def _in_projection(xs, mod, gain, w_main, w_gate, w_gate_t, rope_cs, rope_sn):
    nb, t, d = xs.shape
    nt = t // TOKEN_TILE
    tile = lambda n: pl.BlockSpec((None, TOKEN_TILE, n), lambda b, i: (b, i, 0))
    const = lambda a: pl.BlockSpec(a.shape, lambda b, i: (0,) * a.ndim)
    out_shapes = [jax.ShapeDtypeStruct((nb, t, n), F32) for _, n in _IN_SPLITS]
    out_shapes += [jax.ShapeDtypeStruct((nb, t, LANES), F32), jax.ShapeDtypeStruct((nb, M_GATES, t), F32)]
    out_specs = [tile(n) for _, n in _IN_SPLITS]
    out_specs += [tile(LANES), pl.BlockSpec((None, M_GATES, TOKEN_TILE), lambda b, i: (b, 0, i))]
    return pl.pallas_call(
        _inproj_kernel,
        grid=(nb, nt),
        in_specs=[
            tile(d),
            pl.BlockSpec((None, N_MOD, d), lambda b, i: (jnp.where(i == 0, nb, b), 0, 0)),
            const(gain), const(w_main), const(w_gate), const(w_gate_t),
            pl.BlockSpec((TOKEN_TILE, LANES), lambda b, i: (i, 0)),
            pl.BlockSpec((TOKEN_TILE, LANES), lambda b, i: (i, 0)),
        ],
        out_specs=out_specs,
        out_shape=out_shapes,
        compiler_params=_params(("parallel", "arbitrary")),
        name="in_projection",
    )(xs, mod, gain, w_main, w_gate, w_gate_t, rope_cs, rope_sn)


def _attn_kernel(sink_ref, q_ref, kp_ref, kc_ref, kn_ref, vp_ref, vc_ref, vn_ref, kx_ref, vx_ref, o_ref):
    j = pl.program_id(1)
    nblk = pl.num_programs(1)
    ctx_blocks = CTX_LEN // ATT_BLOCK
    rows = ATT_GROUP * ATT_BLOCK
    r = lax.broadcasted_iota(jnp.int32, (rows, ATT_BLOCK), 0) & (ATT_BLOCK - 1)
    c = lax.broadcasted_iota(jnp.int32, (rows, ATT_BLOCK), 1)
    m_prev = (c >= r) & (j >= ctx_blocks + 1)
    m_cur = jnp.broadcast_to(j >= ctx_blocks, (rows, ATT_BLOCK))
    m_next = (c <= r) & (j >= ctx_blocks) & (j <= nblk - 2)
    row = lax.broadcasted_iota(jnp.int32, (rows, 1), 0)
    neg = -jnp.inf
    q = q_ref[...]
    for kh in range(ATT_KV_HEADS):
        ks = slice(kh * HEAD_DIM, (kh + 1) * HEAD_DIM)
        q4 = jnp.concatenate(
            [q[:, (kh * ATT_GROUP + g) * HEAD_DIM:(kh * ATT_GROUP + g + 1) * HEAD_DIM] for g in range(ATT_GROUP)],
            axis=0).astype(BF16)
        sink = jnp.full((rows, 1), sink_ref[kh * ATT_GROUP], F32)
        for g in range(1, ATT_GROUP):
            sink = jnp.where(row >= g * ATT_BLOCK, sink_ref[kh * ATT_GROUP + g], sink)
        s_p = jnp.where(m_prev, _dot_nt(q4, kp_ref[:, ks].astype(BF16)), neg)
        s_c = jnp.where(m_cur, _dot_nt(q4, kc_ref[:, ks].astype(BF16)), neg)
        s_n = jnp.where(m_next, _dot_nt(q4, kn_ref[:, ks].astype(BF16)), neg)
        s_x = _dot_nt(q4, kx_ref[:, ks].astype(BF16))
        m = jnp.maximum(jnp.maximum(jnp.max(s_p, axis=1, keepdims=True), jnp.max(s_c, axis=1, keepdims=True)),
                        jnp.maximum(jnp.max(s_n, axis=1, keepdims=True), jnp.max(s_x, axis=1, keepdims=True)))
        m = jnp.maximum(m, sink)
        p_p = jnp.exp(s_p - m)
        p_c = jnp.exp(s_c - m)
        p_n = jnp.exp(s_n - m)
        p_x = jnp.exp(s_x - m)
        den = (jnp.sum(p_p, axis=1, keepdims=True) + jnp.sum(p_c, axis=1, keepdims=True)
               + jnp.sum(p_n, axis=1, keepdims=True) + jnp.sum(p_x, axis=1, keepdims=True) + jnp.exp(sink - m))
        o = (_dot(p_p.astype(BF16), vp_ref[:, ks].astype(BF16)) + _dot(p_c.astype(BF16), vc_ref[:, ks].astype(BF16))
             + _dot(p_n.astype(BF16), vn_ref[:, ks].astype(BF16)) + _dot(p_x.astype(BF16), vx_ref[:, ks].astype(BF16)))
        o = o / den
        for g in range(ATT_GROUP):
            hd = kh * ATT_GROUP + g
            o_ref[:, hd * HEAD_DIM:(hd + 1) * HEAD_DIM] = o[g * ATT_BLOCK:(g + 1) * ATT_BLOCK, :]


def _attention(aq, ak, av, sink):
    nb, t, _ = aq.shape
    nblk = t // ATT_BLOCK
    kv = lambda f: pl.BlockSpec((None, ATT_BLOCK, ATT_KV), lambda b, j: (b, f(j), 0))
    prev = lambda j: jnp.maximum(j - 1, 0)
    cur = lambda j: j
    nxt = lambda j: jnp.minimum(j + 1, nblk - 1)
    ctx = pl.BlockSpec((None, CTX_LEN, ATT_KV), lambda b, j: (b, 0, 0))
    return pl.pallas_call(
        _attn_kernel,
        grid=(nb, nblk),
        in_specs=[
            pl.BlockSpec(memory_space=pltpu.SMEM),
            pl.BlockSpec((None, ATT_BLOCK, ATT_Q), lambda b, j: (b, j, 0)),
            kv(prev), kv(cur), kv(nxt), kv(prev), kv(cur), kv(nxt), ctx, ctx,
        ],
        out_specs=pl.BlockSpec((None, ATT_BLOCK, ATT_Q), lambda b, j: (b, j, 0)),
        out_shape=jax.ShapeDtypeStruct((nb, t, ATT_Q), F32),
        compiler_params=_params(("parallel", "arbitrary")),
        name="attention",
    )(sink, aq, ak, ak, ak, av, av, av, ak, av)


def _seq_block(d, j, nblk):
    return jnp.where((d == 0) | (j == 0), j, nblk - j)


def _mlstm_kernel(q_ref, k_ref, v_ref, o_ref, gc_ref, gr_ref, gbr_ref, gbc_ref, ng_ref, out_ref,
                  hf_ref, c_ref, n_ref, m_ref):
    d = pl.program_id(1)
    j = pl.program_id(2)
    nblk = pl.num_programs(2)
    L = TOKEN_TILE
    H, dh = MLSTM_HEADS, MLSTM_HEAD_DIM
    row0 = pl.multiple_of(_seq_block(d, j, nblk) * L, L)

    @pl.when(j == 0)
    def _():
        c_ref[...] = jnp.zeros_like(c_ref)
        n_ref[...] = jnp.zeros_like(n_ref)
        m_ref[...] = jnp.zeros_like(m_ref)

    def body(rev):
        ioff = 2 * H if rev else 0
        foff = ioff + H
        ti = lax.broadcasted_iota(jnp.int32, (L, L), 0)
        si = lax.broadcasted_iota(jnp.int32, (L, L), 1)
        vis = (si >= ti) if rev else (si <= ti)
        tri_ts = vis.astype(F32)
        tri_st = ((ti >= si) if rev else (ti <= si)).astype(F32)
        gcol = gc_ref[...] + gbr_ref[...]
        grow = gr_ref[...] + gbc_ref[...]
        lf_col = -_softplus(-gcol)
        lf_row = -_softplus(-grow)
        b_col = _dot_exact(tri_ts, lf_col)
        b_row = _dot_exact(lf_row, tri_st)
        last = 0 if rev else L - 1
        q = q_ref[...]
        k = k_ref[...] * (dh ** -0.5)
        v = v_ref[...]
        kt = k.T
        qb = q.astype(BF16)
        vb = v.astype(BF16)
        for h in range(H):
            hs = slice(h * dh, (h + 1) * dh)
            bt = b_col[:, foff + h:foff + h + 1]
            bs = b_row[foff + h:foff + h + 1, :]
            li_t = gcol[:, ioff + h:ioff + h + 1]
            li_s = grow[ioff + h:ioff + h + 1, :]
            m0 = m_ref[h][:, 0:1]
            dlog = jnp.where(vis, bt - bs + li_s, -jnp.inf)
            m_t = jnp.maximum(jnp.max(dlog, axis=1, keepdims=True), bt + m0)
            f_inter = jnp.exp(bt + m0 - m_t)
            s = _dot(qb[:, hs], kt[hs, :].astype(BF16)) * jnp.exp(dlog - m_t)
            c0 = c_ref[h]
            n0 = n_ref[h]
            num = f_inter * _dot(qb[:, hs], c0.astype(BF16)) + _dot(s.astype(BF16), vb[:, hs])
            den = f_inter * jnp.sum(q[:, hs] * n0, axis=1, keepdims=True) + jnp.sum(s, axis=1, keepdims=True)
            hh = num / jnp.maximum(jnp.abs(den), jnp.exp(-m_t))
            g = bt[last:last + 1, :]
            a = g - bt + li_t
            m_loc = jnp.max(a, axis=0, keepdims=True)
            w = jnp.exp(a - m_loc)
            m_new = jnp.maximum(g + m0, m_loc)
            f_prev = jnp.exp(g + m0 - m_new)
            f_loc = jnp.exp(m_loc - m_new)
            c_loc = _dot(kt[hs, :].astype(BF16), (w * v[:, hs]).astype(BF16))
            n_loc = jnp.sum(w * k[:, hs], axis=0, keepdims=True)
            c_ref[h] = f_prev * c0 + f_loc * c_loc
            n_ref[h] = f_prev * n0 + f_loc * n_loc
            m_ref[h] = jnp.broadcast_to(m_new, m_ref.shape[1:])
            if not rev:
                hf_ref[pl.ds(row0, L), hs] = hh
            else:
                ht = hf_ref[pl.ds(row0, L), hs] + hh
                mu = jnp.mean(ht, axis=1, keepdims=True)
                var = jnp.mean(jnp.square(ht - mu), axis=1, keepdims=True)
                y = (ht - mu) * lax.rsqrt(var + NORM_EPS) * ng_ref[:, hs]
                out_ref[:, hs] = _sigmoid(o_ref[:, hs]) * y

    @pl.when(d == 0)
    def _():
        body(False)

    @pl.when(d == 1)
    def _():
        body(True)


def _mlstm(mq, mk, mv, mo, gcol, grow, gate_b, norm_g):
    nb, t, w = mq.shape
    nblk = t // TOKEN_TILE
    blk = lambda b, d, j: (b, _seq_block(d, j, nblk), 0)
    tile = lambda n: pl.BlockSpec((None, TOKEN_TILE, n), blk)
    gb = gate_b.reshape(M_GATES)
    gbr = jnp.pad(gb, (0, LANES - M_GATES)).reshape(1, LANES)
    gbc = gb.reshape(M_GATES, 1)
    const = lambda a: pl.BlockSpec(a.shape, lambda b, d, j: (0,) * a.ndim)
    ng = norm_g.reshape(1, w)
    return pl.pallas_call(
        _mlstm_kernel,
        grid=(nb, 2, nblk),
        in_specs=[
            tile(w), tile(w), tile(w), tile(w), tile(LANES),
            pl.BlockSpec((None, M_GATES, TOKEN_TILE), lambda b, d, j: (b, 0, _seq_block(d, j, nblk))),
            const(gbr), const(gbc), const(ng),
        ],
        out_specs=pl.BlockSpec((None, TOKEN_TILE, w), lambda b, d, j: (b, jnp.where(d == 0, 0, _seq_block(d, j, nblk)), 0)),
        out_shape=jax.ShapeDtypeStruct((nb, t, w), F32),
        scratch_shapes=[
            pltpu.VMEM((t, w), F32),
            pltpu.VMEM((MLSTM_HEADS, MLSTM_HEAD_DIM, MLSTM_HEAD_DIM), F32),
            pltpu.VMEM((MLSTM_HEADS, 1, MLSTM_HEAD_DIM), F32),
            pltpu.VMEM((MLSTM_HEADS, 1, LANES), F32),
        ],
        compiler_params=_params(("parallel", "arbitrary", "arbitrary")),
        name="mlstm",
    )(mq, mk, mv, mo, gcol, grow, gbr, gbc, ng)


def _lru_kernel(x_ref, xp_ref, xn_ref, y_ref, cw_ref, cb_ref, gw_ref, gb_ref, lam_ref, out_ref,
                hf_ref, xe_ref, carry_ref):
    d = pl.program_id(1)
    j = pl.program_id(2)
    nblk = pl.num_programs(2)
    L = TOKEN_TILE
    pos = _seq_block(d, j, nblk)
    row0 = pl.multiple_of(pos * L, L)

    @pl.when(j == 0)
    def _():
        carry_ref[...] = jnp.zeros_like(carry_ref)

    has_prev = pos >= 2
    has_next = (pos >= 1) & (pos <= nblk - 2)
    xe_ref[0:SUBLANES, :] = jnp.where(has_prev, xp_ref[...], 0.0)
    xe_ref[SUBLANES:SUBLANES + L, :] = x_ref[...]
    xe_ref[SUBLANES + L:, :] = jnp.where(has_next, xn_ref[...], 0.0)
    seq = cb_ref[...]
    for tap in range(CONV_WIDTH):
        off = SUBLANES - CONV_LEFT + tap
        seq = seq + cw_ref[tap:tap + 1, :] * xe_ref[off:off + L, :]
    sb = seq.astype(BF16)
    r = _sigmoid(_dot(sb, gw_ref[0]) + gb_ref[0:1, :])
    i = _sigmoid(_dot(sb, gw_ref[1]) + gb_ref[1:2, :])
    log_a = -LRU_C * r * _softplus(-lam_ref[...])
    a0 = jnp.exp(log_a)
    th = jnp.tanh(log_a)
    u0 = jnp.sqrt(-2.0 * th / (1.0 - th)) * (i * seq)
    t_idx = lax.broadcasted_iota(jnp.int32, (L, LRU_WIDTH), 0)

    def scan(rev):
        a, u = a0, u0
        step = 1
        while step < L:
            if rev:
                ok = t_idx < L - step
                a_sh = pltpu.roll(a, L - step, 0)
                u_sh = pltpu.roll(u, L - step, 0)
            else:
                ok = t_idx >= step
                a_sh = pltpu.roll(a, step, 0)
                u_sh = pltpu.roll(u, step, 0)
            u = jnp.where(ok, a * u_sh + u, u)
            a = jnp.where(ok, a * a_sh, a)
            step *= 2
        h = u + a * carry_ref[...]
        last = 0 if rev else L - 1
        carry_ref[...] = h[last:last + 1, :]
        return h

    @pl.when(d == 0)
    def _():
        hf_ref[pl.ds(row0, L), :] = scan(False)

    @pl.when(d == 1)
    def _():
        h = hf_ref[pl.ds(row0, L), :] + scan(True)
        y = y_ref[...]
        gelu = 0.5 * y * (1.0 + jnp.tanh(np.sqrt(2.0 / np.pi).astype(np.float32) * (y + 0.044715 * (y * y * y))))
        out_ref[...] = h * gelu


def _rglru(rx, ry, conv_w, conv_b, gate_w, gate_b, lam):
    nb, t, w = rx.shape
    nblk = t // TOKEN_TILE
    per_tile = TOKEN_TILE // SUBLANES
    n8 = t // SUBLANES
    blk = lambda b, d, j: (b, _seq_block(d, j, nblk), 0)
    tile = pl.BlockSpec((None, TOKEN_TILE, w), blk)
    halo = lambda f: pl.BlockSpec((None, SUBLANES, w), lambda b, d, j: (b, f(_seq_block(d, j, nblk)), 0))
    prev8 = lambda p: jnp.maximum(p * per_tile - 1, 0)
    next8 = lambda p: jnp.minimum((p + 1) * per_tile, n8 - 1)
    eye = jnp.eye(LRU_BLOCKS, dtype=gate_w.dtype)
    gw = jnp.einsum('dgnij,nm->dgnimj', gate_w, eye).reshape(2, 2, w, w).astype(BF16)
    cb = conv_b.reshape(1, w)
    return pl.pallas_call(
        _lru_kernel,
        grid=(nb, 2, nblk),
        in_specs=[
            tile, halo(prev8), halo(next8), tile,
            pl.BlockSpec(conv_w.shape, lambda b, d, j: (0, 0)),
            pl.BlockSpec(cb.shape, lambda b, d, j: (0, 0)),
            pl.BlockSpec((None, 2, w, w), lambda b, d, j: (d, 0, 0, 0)),
            pl.BlockSpec((None, 2, w), lambda b, d, j: (d, 0, 0)),
            pl.BlockSpec((None, 1, w), lambda b, d, j: (d, 0, 0)),
        ],
        out_specs=pl.BlockSpec((None, TOKEN_TILE, w), lambda b, d, j: (b, jnp.where(d == 0, 0, _seq_block(d, j, nblk)), 0)),
        out_shape=jax.ShapeDtypeStruct((nb, t, w), F32),
        scratch_shapes=[
            pltpu.VMEM((t, w), F32),
            pltpu.VMEM((TOKEN_TILE + 2 * SUBLANES, w), F32),
            pltpu.VMEM((1, w), F32),
        ],
        compiler_params=_params(("parallel", "arbitrary", "arbitrary")),
        name="rglru",
    )(rx, rx, rx, ry, conv_w, cb, gw, gate_b, lam.reshape(2, 1, w))


def _outffn_kernel(x_ref, att_ref, mem_ref, rec_ref, mod_ref, g_ref, wo_ref, wg_ref, wu_ref, wd_ref, out_ref, acc_ref):
    mix = (_dot(att_ref[...].astype(BF16), wo_ref[0:ATT_Q, :])
           + _dot(mem_ref[...].astype(BF16), wo_ref[ATT_Q:ATT_Q + M_W, :])
           + _dot(rec_ref[...].astype(BF16), wo_ref[ATT_Q + M_W:, :]))
    x1 = x_ref[...] + mod_ref[2:3, :] * _rms(mix, g_ref[1:2, :])
    h = (_rms(x1, g_ref[2:3, :]) * (1.0 + mod_ref[4:5, :]) + mod_ref[3:4, :]).astype(BF16)
    acc_ref[...] = jnp.zeros_like(acc_ref)

    def chunk(ci, carry):
        gate = _dot(h, wg_ref[ci])
        up = _dot(h, wu_ref[ci])
        act = (gate * _sigmoid(gate) * up).astype(BF16)
        acc_ref[...] += _dot(act, wd_ref[ci])
        return carry

    lax.fori_loop(0, wg_ref.shape[0], chunk, 0)
    out_ref[...] = x1 + mod_ref[5:6, :] * _rms(acc_ref[...], g_ref[3:4, :])


def _out_ffn(xs, att, mem, rec, mod, gain, w_out, w_gate, w_up, w_down):
    nb, t, d = xs.shape
    nt = t // TOKEN_TILE
    tile = lambda n: pl.BlockSpec((None, TOKEN_TILE, n), lambda b, i: (b, i, 0))
    const = lambda a: pl.BlockSpec(a.shape, lambda b, i: (0,) * a.ndim, pipeline_mode=pl.Buffered(1))
    return pl.pallas_call(
        _outffn_kernel,
        grid=(nb, nt),
        in_specs=[
            tile(d), tile(ATT_Q), tile(M_W), tile(LRU_WIDTH),
            pl.BlockSpec((None, N_MOD, d), lambda b, i: (jnp.where(i == 0, nb, b), 0, 0)),
            const(gain), const(w_out), const(w_gate), const(w_up), const(w_down),
        ],
        out_specs=tile(d),
        out_shape=jax.ShapeDtypeStruct((nb, t, d), F32),
        scratch_shapes=[pltpu.VMEM((TOKEN_TILE, d), F32)],
        compiler_params=_params(("parallel", "arbitrary")),
        name="out_ffn",
    )(xs, att, mem, rec, mod, gain, w_out, w_gate, w_up, w_down)


def _rope_tables(n_lat):
    t = jnp.arange(n_lat)
    row = (t // GRID_W).astype(F32)
    col = (t % GRID_W).astype(F32)
    freqs = ROPE_BASE ** (-jnp.arange(ROPE_PAIRS, dtype=F32) / ROPE_PAIRS)
    ang_r = row[:, None] * freqs
    ang_c = col[:, None] * freqs
    cs = jnp.concatenate([jnp.cos(ang_r), jnp.cos(ang_r), jnp.cos(ang_c), jnp.cos(ang_c)], axis=-1)
    sn = jnp.concatenate([-jnp.sin(ang_r), jnp.sin(ang_r), -jnp.sin(ang_c), jnp.sin(ang_c)], axis=-1)
    cs = jnp.concatenate([jnp.ones((CTX_LEN, HEAD_DIM), F32), cs], axis=0)
    sn = jnp.concatenate([jnp.zeros((CTX_LEN, HEAD_DIM), F32), sn], axis=0)
    return jnp.tile(cs, (1, LANES // HEAD_DIM)), jnp.tile(sn, (1, LANES // HEAD_DIM))


def kernel(x, c, ctx, c_ctx, w_ada, b_ada, norm_gain, w_in, w_out, attn_sink, mlstm_gate_b, mlstm_norm, conv_w, conv_b,
           lru_gate_w, lru_gate_b, lru_lam, w_ffn_in, w_ffn_out):
    nb, n_lat, d = x.shape
    depth = w_ada.shape[0]
    assert ctx.shape[1] == CTX_LEN and n_lat % TOKEN_TILE == 0 and nb < MOD_ROWS
    cvec = jnp.concatenate([c, c_ctx[None, :], jnp.zeros((MOD_ROWS - nb - 1, d), F32)], axis=0)
    mod = _modulation(cvec, w_ada, b_ada).reshape(depth, MOD_ROWS, N_MOD, d)
    rope_cs, rope_sn = _rope_tables(n_lat)
    xs = jnp.concatenate([ctx, x], axis=1)
    n_ff = D_FF // FF_CHUNK
    for l in range(depth):
        w_main = jnp.concatenate([w_in[l, :, :_GATE_COL0], w_in[l, :, _GATE_COL0 + M_GATES:]], axis=1).astype(BF16)
        w_g = w_in[l, :, _GATE_COL0:_GATE_COL0 + M_GATES]
        w_gate = jnp.pad(w_g, ((0, 0), (0, LANES - M_GATES))).astype(BF16)
        w_gate_t = w_g.T.astype(BF16)
        aq, ak, av, mq, mk, mv, mo, rx, ry, gcol, grow = _in_projection(
            xs, mod[l], norm_gain[l], w_main, w_gate, w_gate_t, rope_cs, rope_sn)
        att = _attention(aq, ak, av, attn_sink[l])
        mem = _mlstm(mq, mk, mv, mo, gcol, grow, mlstm_gate_b[l], mlstm_norm[l])
        rec = _rglru(rx, ry, conv_w[l], conv_b[l], lru_gate_w[l], lru_gate_b[l], lru_lam[l])
        wf = w_ffn_in[l].astype(BF16)
        w_gate_ff = wf[:, :D_FF].reshape(d, n_ff, FF_CHUNK).transpose(1, 0, 2)
        w_up_ff = wf[:, D_FF:].reshape(d, n_ff, FF_CHUNK).transpose(1, 0, 2)
        w_down_ff = w_ffn_out[l].astype(BF16).reshape(n_ff, FF_CHUNK, d)
        xs = _out_ffn(xs, att, mem, rec, mod[l], norm_gain[l], w_out[l].astype(BF16), w_gate_ff, w_up_ff, w_down_ff)
    return xs[:, CTX_LEN:]
```

```python
import functools

import jax
import jax.numpy as jnp
import numpy as np
from jax import lax
from jax.experimental import pallas as pl
from jax.experimental.pallas import tpu as pltpu

F32 = jnp.float32
BF16 = jnp.bfloat16

D_MODEL = 1024
GRID_W = 64
CTX_LEN = 256
N_MOD = 6
NORM_EPS = 1e-6
ATT_HEADS = 8
ATT_KV_HEADS = 2
ATT_GROUP = ATT_HEADS // ATT_KV_HEADS
HEAD_DIM = 64
WINDOW = 128
ATT_BLOCK = 128
ROPE_BASE = 10000.0
ROPE_PAIRS = HEAD_DIM // 4
ATT_Q = ATT_HEADS * HEAD_DIM
ATT_KV = ATT_KV_HEADS * HEAD_DIM
MLSTM_HEADS = 4
MLSTM_HEAD_DIM = 64
M_W = MLSTM_HEADS * MLSTM_HEAD_DIM
M_GATES = 4 * MLSTM_HEADS
LRU_WIDTH = 256
LRU_BLOCKS = 4
LRU_BW = LRU_WIDTH // LRU_BLOCKS
LRU_C = 8.0
CONV_WIDTH = 4
CONV_LEFT = CONV_WIDTH // 2
D_FF = -(-8 * D_MODEL // (3 * 256)) * 256

LANES = 128
SUBLANES = 8
TOKEN_TILE = CTX_LEN
FF_CHUNK = 256
MOD_ROWS = 16
VMEM_LIMIT = 56 * 1024 * 1024


def _params(sem):
    return pltpu.CompilerParams(dimension_semantics=sem, vmem_limit_bytes=VMEM_LIMIT)


def _dot(a, b):
    return jnp.dot(a, b, preferred_element_type=F32)


def _dot_nt(a, b):
    return lax.dot_general(a, b, (((1,), (1,)), ((), ())), preferred_element_type=F32)


def _dot_exact(a, b):
    return jnp.dot(a, b, preferred_element_type=F32, precision=lax.Precision.HIGHEST)


def _sigmoid(x):
    return 1.0 / (1.0 + jnp.exp(-x))


def _softplus(x):
    return jnp.maximum(x, 0.0) + jnp.log1p(jnp.exp(-jnp.abs(x)))


def _rms(x, g):
    return x * lax.rsqrt(jnp.mean(x * x, axis=-1, keepdims=True) + NORM_EPS) * g


def _mod_kernel(c_ref, w_ref, b_ref, o_ref):
    c = c_ref[...]
    s = (c * _sigmoid(c)).astype(BF16)
    o_ref[...] = _dot(s, w_ref[...].astype(BF16)) + b_ref[...]


def _modulation(cvec, w_ada, b_ada):
    depth, d, n = w_ada.shape
    tn = 1536
    return pl.pallas_call(
        _mod_kernel,
        grid=(depth, n // tn),
        in_specs=[
            pl.BlockSpec((MOD_ROWS, d), lambda l, j: (0, 0)),
            pl.BlockSpec((None, d, tn), lambda l, j: (l, 0, j)),
            pl.BlockSpec((None, 1, tn), lambda l, j: (l, 0, j)),
        ],
        out_specs=pl.BlockSpec((None, MOD_ROWS, tn), lambda l, j: (l, 0, j)),
        out_shape=jax.ShapeDtypeStruct((depth, MOD_ROWS, n), F32),
        compiler_params=_params(("arbitrary", "arbitrary")),
        name="modulation",
    )(cvec, w_ada, b_ada.reshape(depth, 1, n))


_IN_SPLITS = (("aq", ATT_Q), ("ak", ATT_KV), ("av", ATT_KV), ("mq", M_W), ("mk", M_W), ("mv", M_W), ("mo", M_W),
              ("rx", LRU_WIDTH), ("ry", LRU_WIDTH))
_GATE_COL0 = ATT_Q + 2 * ATT_KV + 4 * M_W
D_IN_MAIN = sum(w for _, w in _IN_SPLITS)


def _rope_slab(x, cs, sn, first):
    swapped = jnp.where(first, pltpu.roll(x, LANES - ROPE_PAIRS, 1), pltpu.roll(x, ROPE_PAIRS, 1))
    return x * cs + swapped * sn


def _inproj_kernel(x_ref, mod_ref, g_ref, w_ref, wgt_ref, cs_ref, sn_ref,
                   aq_ref, ak_ref, av_ref, mq_ref, mk_ref, mv_ref, mo_ref, rx_ref, ry_ref, gr_ref):
    x = x_ref[...]
    h = _rms(x, g_ref[0:1, :]) * (1.0 + mod_ref[1:2, :]) + mod_ref[0:1, :]
    hb = h.astype(BF16)
    cs = cs_ref[...]
    sn = sn_ref[...]
    lane = lax.broadcasted_iota(jnp.int32, cs.shape, 1)
    first = (lane & (2 * ROPE_PAIRS - 1)) < ROPE_PAIRS
    col = 0
    for i in range(ATT_Q // LANES):
        p = _dot(hb, w_ref[:, col:col + LANES])
        aq_ref[:, i * LANES:(i + 1) * LANES] = _rope_slab(p, cs, sn, first) * (HEAD_DIM ** -0.5)
        col += LANES
    p = _dot(hb, w_ref[:, col:col + LANES])
    ak_ref[...] = _rope_slab(p, cs, sn, first)
    col += LANES
    for ref in (av_ref, mq_ref, mk_ref, mv_ref, mo_ref, rx_ref, ry_ref):
        n = ref.shape[-1]
        ref[...] = _dot(hb, w_ref[:, col:col + n])
        col += n
    gr_ref[...] = _dot_nt(wgt_ref[...], hb)


def _in_projection(xs, mod, gain, w_main, w_gate_t, rope_cs, rope_sn):
    nb, t, d = xs.shape
    nt = t // TOKEN_TILE
    tile = lambda n: pl.BlockSpec((None, TOKEN_TILE, n), lambda b, i: (b, i, 0))
    const = lambda a: pl.BlockSpec(a.shape, lambda b, i: (0,) * a.ndim)
    out_shapes = [jax.ShapeDtypeStruct((nb, t, n), F32) for _, n in _IN_SPLITS]
    out_shapes += [jax.ShapeDtypeStruct((nb, M_GATES, t), F32)]
    out_specs = [tile(n) for _, n in _IN_SPLITS]
    out_specs += [pl.BlockSpec((None, M_GATES, TOKEN_TILE), lambda b, i: (b, 0, i))]
    return pl.pallas_call(
        _inproj_kernel,
        grid=(nb, nt),
        in_specs=[
            tile(d),
            pl.BlockSpec((None, N_MOD, d), lambda b, i: (jnp.where(i == 0, nb, b), 0, 0)),
            const(gain), const(w_main), const(w_gate_t),
            pl.BlockSpec((TOKEN_TILE, LANES), lambda b, i: (i, 0)),
            pl.BlockSpec((TOKEN_TILE, LANES), lambda b, i: (i, 0)),
        ],
        out_specs=out_specs,
        out_shape=out_shapes,
        compiler_params=_params(("parallel", "arbitrary")),
        name="in_projection",
    )(xs, mod, gain, w_main, w_gate_t, rope_cs, rope_sn)


def _attn_kernel(sink_ref, q_ref, kp_ref, kc_ref, kn_ref, vp_ref, vc_ref, vn_ref, kx_ref, vx_ref, o_ref):
    j = pl.program_id(1)
    nblk = pl.num_programs(1)
    ctx_blocks = CTX_LEN // ATT_BLOCK
    rows = ATT_GROUP * ATT_BLOCK
    r = lax.broadcasted_iota(jnp.int32, (rows, ATT_BLOCK), 0) & (ATT_BLOCK - 1)
    c = lax.broadcasted_iota(jnp.int32, (rows, ATT_BLOCK), 1)
    m_prev = (c >= r) & (j >= ctx_blocks + 1)
    m_cur = jnp.broadcast_to(j >= ctx_blocks, (rows, ATT_BLOCK))
    m_next = (c <= r) & (j >= ctx_blocks) & (j <= nblk - 2)
    row = lax.broadcasted_iota(jnp.int32, (rows, 1), 0)
    neg = -jnp.inf
    q = q_ref[...]
    for kh in range(ATT_KV_HEADS):
        ks = slice(kh * HEAD_DIM, (kh + 1) * HEAD_DIM)
        q4 = jnp.concatenate(
            [q[:, (kh * ATT_GROUP + g) * HEAD_DIM:(kh * ATT_GROUP + g + 1) * HEAD_DIM] for g in range(ATT_GROUP)],
            axis=0).astype(BF16)
        sink = jnp.full((rows, 1), sink_ref[kh * ATT_GROUP], F32)
        for g in range(1, ATT_GROUP):
            sink = jnp.where(row >= g * ATT_BLOCK, sink_ref[kh * ATT_GROUP + g], sink)
        s_p = jnp.where(m_prev, _dot_nt(q4, kp_ref[:, ks].astype(BF16)), neg)
        s_c = jnp.where(m_cur, _dot_nt(q4, kc_ref[:, ks].astype(BF16)), neg)
        s_n = jnp.where(m_next, _dot_nt(q4, kn_ref[:, ks].astype(BF16)), neg)
        s_x = _dot_nt(q4, kx_ref[:, ks].astype(BF16))
        m = jnp.maximum(jnp.maximum(jnp.max(s_p, axis=1, keepdims=True), jnp.max(s_c, axis=1, keepdims=True)),
                        jnp.maximum(jnp.max(s_n, axis=1, keepdims=True), jnp.max(s_x, axis=1, keepdims=True)))
        m = jnp.maximum(m, sink)
        p_p = jnp.exp(s_p - m)
        p_c = jnp.exp(s_c - m)
        p_n = jnp.exp(s_n - m)
        p_x = jnp.exp(s_x - m)
        den = (jnp.sum(p_p, axis=1, keepdims=True) + jnp.sum(p_c, axis=1, keepdims=True)
               + jnp.sum(p_n, axis=1, keepdims=True) + jnp.sum(p_x, axis=1, keepdims=True) + jnp.exp(sink - m))
        o = (_dot(p_p.astype(BF16), vp_ref[:, ks].astype(BF16)) + _dot(p_c.astype(BF16), vc_ref[:, ks].astype(BF16))
             + _dot(p_n.astype(BF16), vn_ref[:, ks].astype(BF16)) + _dot(p_x.astype(BF16), vx_ref[:, ks].astype(BF16)))
        o = o / den
        for g in range(ATT_GROUP):
            hd = kh * ATT_GROUP + g
            o_ref[:, hd * HEAD_DIM:(hd + 1) * HEAD_DIM] = o[g * ATT_BLOCK:(g + 1) * ATT_BLOCK, :]


def _attention(aq, ak, av, sink):
    nb, t, _ = aq.shape
    nblk = t // ATT_BLOCK
    kv = lambda f: pl.BlockSpec((None, ATT_BLOCK, ATT_KV), lambda b, j: (b, f(j), 0))
    prev = lambda j: jnp.maximum(j - 1, 0)
    cur = lambda j: j
    nxt = lambda j: jnp.minimum(j + 1, nblk - 1)
    ctx = pl.BlockSpec((None, CTX_LEN, ATT_KV), lambda b, j: (b, 0, 0))
    return pl.pallas_call(
        _attn_kernel,
        grid=(nb, nblk),
        in_specs=[
            pl.BlockSpec(memory_space=pltpu.SMEM),
            pl.BlockSpec((None, ATT_BLOCK, ATT_Q), lambda b, j: (b, j, 0)),
            kv(prev), kv(cur), kv(nxt), kv(prev), kv(cur), kv(nxt), ctx, ctx,
        ],
        out_specs=pl.BlockSpec((None, ATT_BLOCK, ATT_Q), lambda b, j: (b, j, 0)),
        out_shape=jax.ShapeDtypeStruct((nb, t, ATT_Q), F32),
        compiler_params=_params(("parallel", "arbitrary")),
        name="attention",
    )(sink, aq, ak, ak, ak, av, av, av, ak, av)


def _seq_block(d, j, nblk):
    return jnp.where((d == 0) | (j == 0), j, nblk - j)


def _lane_scan(x, op, fill, rev):
    n = x.shape[-1]
    lane = lax.broadcasted_iota(jnp.int32, x.shape, 1)
    step = 1
    while step < n:
        if rev:
            shifted = jnp.where(lane < n - step, pltpu.roll(x, n - step, 1), fill)
        else:
            shifted = jnp.where(lane >= step, pltpu.roll(x, step, 1), fill)
        x = op(x, shifted)
        step *= 2
    return x


def _split3(x):
    hi = x.astype(BF16).astype(F32)
    mid = (x - hi).astype(BF16).astype(F32)
    lo = x - hi - mid
    return hi, mid, lo


def _mlstm_tables():
    H, dh = MLSTM_HEADS, MLSTM_HEAD_DIM
    n_pairs = H // 2
    n_slabs = H + 3 * n_pairs
    sel = np.zeros((LANES, n_slabs * LANES), np.float32)
    for part in range(3):
        base = part * 4 * H
        for h in range(H):
            sel[base + h, h * LANES:(h + 1) * LANES] = 1.0
        for qty in range(3):
            for p in range(n_pairs):
                slab = H + qty * n_pairs + p
                for half in range(2):
                    row = base + H * qty + 2 * p + half
                    sel[row, slab * LANES + half * dh:slab * LANES + (half + 1) * dh] = 1.0
    ones_bd = np.zeros((2 * TOKEN_TILE, LANES), np.float32)
    ones_bd[:TOKEN_TILE, :dh] = 1.0
    ones_bd[TOKEN_TILE:, dh:] = 1.0
    avg = np.zeros((3 * LANES, LANES), np.float32)
    for part in range(3):
        for half in range(2):
            avg[part * LANES + half * dh:part * LANES + (half + 1) * dh, half * dh:(half + 1) * dh] = 1.0 / dh
    return jnp.asarray(sel, BF16), jnp.asarray(ones_bd, BF16), jnp.asarray(avg, BF16)


def _mlstm_block_terms(rev, g, sel_ref, cols_ref, r_ref, gm_ref, slot):
    L = TOKEN_TILE
    H, dh = MLSTM_HEADS, MLSTM_HEAD_DIM
    ioff = 2 * H if rev else 0
    li = g[ioff:ioff + H, :]
    lf = -_softplus(-g[ioff + H:ioff + 2 * H, :])
    b = _lane_scan(lf, jnp.add, 0.0, rev)
    r = li - b
    cm = _lane_scan(r, jnp.maximum, -jnp.inf, rev)
    last = 0 if rev else L - 1
    g_tot = b[:, last:last + 1]
    a = g_tot - b + li
    m_loc = jnp.max(a, axis=1, keepdims=True)
    w = jnp.exp(a - m_loc)
    stacked = jnp.concatenate([cm, b, w, jnp.zeros((H, L), F32)], axis=0)
    hi, mid, lo = _split3(stacked)
    pad = jnp.zeros((LANES - 3 * 4 * H, L), F32)
    cols_ref[slot] = _dot(jnp.concatenate([hi, mid, lo, pad], axis=0).T.astype(BF16), sel_ref[...])
    r_ref[slot] = jnp.concatenate([r, jnp.zeros((SUBLANES - H, L), F32)], axis=0)
    lane_row = lax.broadcasted_iota(jnp.int32, (1, LANES), 1)

    def layout(col):
        rows = [jnp.broadcast_to(col, (H, LANES))]
        rows += [jnp.where(lane_row < dh, col[2 * p:2 * p + 1, :], col[2 * p + 1:2 * p + 2, :]) for p in range(H // 2)]
        rows += [jnp.zeros((SUBLANES - H - H // 2, LANES), F32)]
        return jnp.concatenate(rows, axis=0)

    gm_ref[slot] = jnp.concatenate([layout(g_tot), layout(m_loc)], axis=0)


def _mlstm_kernel(q_ref, k_ref, v_ref, o_ref, gr_ref, gnext_ref, gb_ref, ng_ref, sel_ref, ones_ref, avg_ref, out_ref,
                  hf_ref, c_ref, n_ref, m_ref, cols_ref, r_ref, gm_ref):
    d = pl.program_id(1)
    j = pl.program_id(2)
    nblk = pl.num_programs(2)
    L = TOKEN_TILE
    H, dh = MLSTM_HEADS, MLSTM_HEAD_DIM
    n_pairs = H // 2
    row0 = pl.multiple_of(_seq_block(d, j, nblk) * L, L)
    slot = j & 1

    def body(rev):
        @pl.when(j == 0)
        def _():
            c_ref[...] = jnp.zeros_like(c_ref)
            n_ref[...] = jnp.zeros_like(n_ref)
            m_ref[...] = jnp.zeros_like(m_ref)
            _mlstm_block_terms(rev, gr_ref[...] + gb_ref[...], sel_ref, cols_ref, r_ref, gm_ref, 0)

        r = r_ref[slot]
        gm = gm_ref[slot]
        slab = lambda i: cols_ref[slot, :, i * LANES:(i + 1) * LANES]
        m0 = m_ref[...]
        g_tot, m_loc = gm[0:SUBLANES], gm[SUBLANES:]
        m_new = jnp.maximum(g_tot + m0, m_loc)
        f_prev = jnp.exp(g_tot + m0 - m_new)
        f_loc = jnp.exp(m_loc - m_new)
        m_ref[...] = m_new
        lane = lax.broadcasted_iota(jnp.int32, (L, LANES), 1)
        first_head = lane < dh
        t_idx = lax.broadcasted_iota(jnp.int32, (L, LANES), 0)
        kt = (k_ref[...] * (dh ** -0.5)).T.astype(BF16)
        zeros_half = jnp.zeros((dh, L), BF16)
        bd_mask = ((lax.broadcasted_iota(jnp.int32, (LANES, LANES), 0) < dh)
                   == (lax.broadcasted_iota(jnp.int32, (LANES, LANES), 1) < dh))
        for p in range(n_pairs):
            ps = slice(p * LANES, (p + 1) * LANES)
            qb = q_ref[:, ps].astype(BF16)
            v = v_ref[:, ps]
            kt_pair = kt[ps, :]
            sm = []
            for half in range(2):
                h = 2 * p + half
                kt_h = kt[h * dh:(h + 1) * dh, :]
                kt_masked = jnp.concatenate([kt_h, zeros_half] if half == 0 else [zeros_half, kt_h], axis=0)
                s = _dot(qb, kt_masked)
                m_b = jnp.maximum(slab(h), m0[h:h + 1, :])
                for lt in range(L // LANES):
                    s_idx = lane + lt * LANES
                    vis = (s_idx >= t_idx) if rev else (s_idx <= t_idx)
                    e = jnp.exp(jnp.where(vis, r[h:h + 1, lt * LANES:(lt + 1) * LANES] - m_b, -jnp.inf))
                    sm.append((s[:, lt * LANES:(lt + 1) * LANES] * e).astype(BF16))
            sm = jnp.concatenate(sm, axis=1)
            v_bd = jnp.concatenate([jnp.where(first_head, v, 0.0), jnp.where(first_head, 0.0, v)], axis=0).astype(BF16)
            m0_pair = m0[H + p:H + p + 1, :]
            m_pair = jnp.maximum(slab(H + p), m0_pair)
            f_pair = jnp.exp(m0_pair - m_pair)
            num = _dot(sm, v_bd) + f_pair * _dot(qb, c_ref[p].astype(BF16))
            den = _dot(sm, ones_ref[...]) + f_pair * _dot(qb, n_ref[p].astype(BF16))
            hh = num / jnp.maximum(jnp.abs(den), jnp.exp(-(slab(H + n_pairs + p) + m_pair)))
            w_pair = slab(H + 2 * n_pairs + p)
            c_loc = jnp.where(bd_mask, _dot(kt_pair, (w_pair * v).astype(BF16)), 0.0)
            n_loc = jnp.where(bd_mask, _dot(kt_pair, w_pair.astype(BF16)), 0.0)
            c_ref[p] = f_prev[H + p:H + p + 1, :] * c_ref[p] + f_loc[H + p:H + p + 1, :] * c_loc
            n_ref[p] = f_prev[H + p:H + p + 1, :] * n_ref[p] + f_loc[H + p:H + p + 1, :] * n_loc
            if not rev:
                hf_ref[pl.ds(row0, L), ps] = hh
            else:
                ht = hf_ref[pl.ds(row0, L), ps] + hh
                mu = _dot(jnp.concatenate(_split3(ht), axis=1).astype(BF16), avg_ref[...])
                cen = ht - mu
                var = _dot(jnp.concatenate(_split3(cen * cen), axis=1).astype(BF16), avg_ref[...])
                y = cen * lax.rsqrt(var + NORM_EPS) * ng_ref[:, ps]
                out_ref[:, ps] = _sigmoid(o_ref[:, ps]) * y
        _mlstm_block_terms(rev, gnext_ref[...] + gb_ref[...], sel_ref, cols_ref, r_ref, gm_ref, 1 - slot)

    @pl.when(d == 0)
    def _():
        body(False)

    @pl.when(d == 1)
    def _():
        body(True)


def _mlstm(mq, mk, mv, mo, grow, gate_b, norm_g):
    nb, t, w = mq.shape
    nblk = t // TOKEN_TILE
    blk = lambda b, d, j: (b, _seq_block(d, j, nblk), 0)
    tile = pl.BlockSpec((None, TOKEN_TILE, w), blk)
    gb = gate_b.reshape(M_GATES, 1)
    ng = norm_g.reshape(1, w)
    sel, ones_bd, avg = _mlstm_tables()
    const = lambda a: pl.BlockSpec(a.shape, lambda b, d, j: (0,) * a.ndim)
    n_pairs = MLSTM_HEADS // 2
    return pl.pallas_call(
        _mlstm_kernel,
        grid=(nb, 2, nblk),
        in_specs=[
            tile, tile, tile, tile,
            pl.BlockSpec((None, M_GATES, TOKEN_TILE), lambda b, d, j: (b, 0, _seq_block(d, j, nblk))),
            pl.BlockSpec((None, M_GATES, TOKEN_TILE),
                         lambda b, d, j: (b, 0, _seq_block(d, jnp.minimum(j + 1, nblk - 1), nblk))),
            const(gb), const(ng), const(sel), const(ones_bd), const(avg),
        ],
        out_specs=pl.BlockSpec((None, TOKEN_TILE, w), lambda b, d, j: (b, jnp.where(d == 0, 0, _seq_block(d, j, nblk)), 0)),
        out_shape=jax.ShapeDtypeStruct((nb, t, w), F32),
        scratch_shapes=[
            pltpu.VMEM((t, w), F32),
            pltpu.VMEM((n_pairs, LANES, LANES), F32),
            pltpu.VMEM((n_pairs, LANES, LANES), F32),
            pltpu.VMEM((SUBLANES, LANES), F32),
            pltpu.VMEM((2, TOKEN_TILE, sel.shape[1]), F32),
            pltpu.VMEM((2, SUBLANES, TOKEN_TILE), F32),
            pltpu.VMEM((2, 2 * SUBLANES, LANES), F32),
        ],
        compiler_params=_params(("parallel", "arbitrary", "arbitrary")),
        name="mlstm",
    )(mq, mk, mv, mo, grow, grow, gb, ng, sel, ones_bd, avg)


def _lru_kernel(x_ref, xp_ref, xn_ref, y_ref, cw_ref, cb_ref, gw_ref, gb_ref, lam_ref, out_ref,
                hf_ref, xe_ref, carry_ref):
    d = pl.program_id(1)
    j = pl.program_id(2)
    nblk = pl.num_programs(2)
    L = TOKEN_TILE
    pos = _seq_block(d, j, nblk)
    row0 = pl.multiple_of(pos * L, L)

    @pl.when(j == 0)
    def _():
        carry_ref[...] = jnp.zeros_like(carry_ref)

    has_prev = pos >= 2
    has_next = (pos >= 1) & (pos <= nblk - 2)
    xe_ref[0:SUBLANES, :] = jnp.where(has_prev, xp_ref[...], 0.0)
    xe_ref[SUBLANES:SUBLANES + L, :] = x_ref[...]
    xe_ref[SUBLANES + L:, :] = jnp.where(has_next, xn_ref[...], 0.0)
    seq = cb_ref[...]
    for tap in range(CONV_WIDTH):
        off = SUBLANES - CONV_LEFT + tap
        seq = seq + cw_ref[tap:tap + 1, :] * xe_ref[off:off + L, :]
    sb = seq.astype(BF16)
    r = _sigmoid(_dot(sb, gw_ref[0]) + gb_ref[0:1, :])
    i = _sigmoid(_dot(sb, gw_ref[1]) + gb_ref[1:2, :])
    log_a = -LRU_C * r * _softplus(-lam_ref[...])
    a0 = jnp.exp(log_a)
    th = jnp.tanh(log_a)
    u0 = jnp.sqrt(-2.0 * th / (1.0 - th)) * (i * seq)
    t_idx = lax.broadcasted_iota(jnp.int32, (L, LRU_WIDTH), 0)

    def scan(rev):
        a, u = a0, u0
        step = 1
        while step < L:
            if rev:
                ok = t_idx < L - step
                a_sh = pltpu.roll(a, L - step, 0)
                u_sh = pltpu.roll(u, L - step, 0)
            else:
                ok = t_idx >= step
                a_sh = pltpu.roll(a, step, 0)
                u_sh = pltpu.roll(u, step, 0)
            u = jnp.where(ok, a * u_sh + u, u)
            a = jnp.where(ok, a * a_sh, a)
            step *= 2
        h = u + a * carry_ref[...]
        last = 0 if rev else L - 1
        carry_ref[...] = h[last:last + 1, :]
        return h

    @pl.when(d == 0)
    def _():
        hf_ref[pl.ds(row0, L), :] = scan(False)

    @pl.when(d == 1)
    def _():
        h = hf_ref[pl.ds(row0, L), :] + scan(True)
        y = y_ref[...]
        gelu = 0.5 * y * (1.0 + jnp.tanh(np.sqrt(2.0 / np.pi).astype(np.float32) * (y + 0.044715 * (y * y * y))))
        out_ref[...] = h * gelu


def _rglru(rx, ry, conv_w, conv_b, gate_w, gate_b, lam):
    nb, t, w = rx.shape
    nblk = t // TOKEN_TILE
    per_tile = TOKEN_TILE // SUBLANES
    n8 = t // SUBLANES
    blk = lambda b, d, j: (b, _seq_block(d, j, nblk), 0)
    tile = pl.BlockSpec((None, TOKEN_TILE, w), blk)
    halo = lambda f: pl.BlockSpec((None, SUBLANES, w), lambda b, d, j: (b, f(_seq_block(d, j, nblk)), 0))
    prev8 = lambda p: jnp.maximum(p * per_tile - 1, 0)
    next8 = lambda p: jnp.minimum((p + 1) * per_tile, n8 - 1)
    eye = jnp.eye(LRU_BLOCKS, dtype=gate_w.dtype)
    gw = jnp.einsum('dgnij,nm->dgnimj', gate_w, eye).reshape(2, 2, w, w).astype(BF16)
    cb = conv_b.reshape(1, w)
    return pl.pallas_call(
        _lru_kernel,
        grid=(nb, 2, nblk),
        in_specs=[
            tile, halo(prev8), halo(next8), tile,
            pl.BlockSpec(conv_w.shape, lambda b, d, j: (0, 0)),
            pl.BlockSpec(cb.shape, lambda b, d, j: (0, 0)),
            pl.BlockSpec((None, 2, w, w), lambda b, d, j: (d, 0, 0, 0)),
            pl.BlockSpec((None, 2, w), lambda b, d, j: (d, 0, 0)),
            pl.BlockSpec((None, 1, w), lambda b, d, j: (d, 0, 0)),
        ],
        out_specs=pl.BlockSpec((None, TOKEN_TILE, w), lambda b, d, j: (b, jnp.where(d == 0, 0, _seq_block(d, j, nblk)), 0)),
        out_shape=jax.ShapeDtypeStruct((nb, t, w), F32),
        scratch_shapes=[
            pltpu.VMEM((t, w), F32),
            pltpu.VMEM((TOKEN_TILE + 2 * SUBLANES, w), F32),
            pltpu.VMEM((1, w), F32),
        ],
        compiler_params=_params(("parallel", "arbitrary", "arbitrary")),
        name="rglru",
    )(rx, rx, rx, ry, conv_w, cb, gw, gate_b, lam.reshape(2, 1, w))


def _outffn_kernel(x_ref, att_ref, mem_ref, rec_ref, mod_ref, g_ref, wo_ref, wi_ref, wd_ref, out_ref, act_ref):
    mix = (_dot(att_ref[...].astype(BF16), wo_ref[0:ATT_Q, :])
           + _dot(mem_ref[...].astype(BF16), wo_ref[ATT_Q:ATT_Q + M_W, :])
           + _dot(rec_ref[...].astype(BF16), wo_ref[ATT_Q + M_W:, :]))
    x1 = x_ref[...] + mod_ref[2:3, :] * _rms(mix, g_ref[1:2, :])
    h = (_rms(x1, g_ref[2:3, :]) * (1.0 + mod_ref[4:5, :]) + mod_ref[3:4, :]).astype(BF16)
    for c0 in range(0, D_FF, FF_CHUNK):
        gate = _dot(h, wi_ref[:, c0:c0 + FF_CHUNK])
        up = _dot(h, wi_ref[:, D_FF + c0:D_FF + c0 + FF_CHUNK])
        half = 0.5 * gate
        act_ref[:, c0:c0 + FF_CHUNK] = ((half + half * jnp.tanh(half)) * up).astype(BF16)
    f = _dot(act_ref[...], wd_ref[...])
    out_ref[...] = x1 + mod_ref[5:6, :] * _rms(f, g_ref[3:4, :])


def _out_ffn(xs, att, mem, rec, mod, gain, w_out, w_ffn_in, w_down):
    nb, t, d = xs.shape
    nt = t // TOKEN_TILE
    tile = lambda n: pl.BlockSpec((None, TOKEN_TILE, n), lambda b, i: (b, i, 0))
    const = lambda a: pl.BlockSpec(a.shape, lambda b, i: (0,) * a.ndim, pipeline_mode=pl.Buffered(1))
    return pl.pallas_call(
        _outffn_kernel,
        grid=(nb, nt),
        in_specs=[
            tile(d), tile(ATT_Q), tile(M_W), tile(LRU_WIDTH),
            pl.BlockSpec((None, N_MOD, d), lambda b, i: (jnp.where(i == 0, nb, b), 0, 0)),
            const(gain), const(w_out), const(w_ffn_in), const(w_down),
        ],
        out_specs=tile(d),
        out_shape=jax.ShapeDtypeStruct((nb, t, d), F32),
        scratch_shapes=[pltpu.VMEM((TOKEN_TILE, D_FF), BF16)],
        compiler_params=_params(("parallel", "arbitrary")),
        name="out_ffn",
    )(xs, att, mem, rec, mod, gain, w_out, w_ffn_in, w_down)


def _rope_tables(n_lat):
    t = jnp.arange(n_lat)
    row = (t // GRID_W).astype(F32)
    col = (t % GRID_W).astype(F32)
    freqs = ROPE_BASE ** (-jnp.arange(ROPE_PAIRS, dtype=F32) / ROPE_PAIRS)
    ang_r = row[:, None] * freqs
    ang_c = col[:, None] * freqs
    cs = jnp.concatenate([jnp.cos(ang_r), jnp.cos(ang_r), jnp.cos(ang_c), jnp.cos(ang_c)], axis=-1)
    sn = jnp.concatenate([-jnp.sin(ang_r), jnp.sin(ang_r), -jnp.sin(ang_c), jnp.sin(ang_c)], axis=-1)
    cs = jnp.concatenate([jnp.ones((CTX_LEN, HEAD_DIM), F32), cs], axis=0)
    sn = jnp.concatenate([jnp.zeros((CTX_LEN, HEAD_DIM), F32), sn], axis=0)
    return jnp.tile(cs, (1, LANES // HEAD_DIM)), jnp.tile(sn, (1, LANES // HEAD_DIM))


def kernel(x, c, ctx, c_ctx, w_ada, b_ada, norm_gain, w_in, w_out, attn_sink, mlstm_gate_b, mlstm_norm, conv_w, conv_b,
           lru_gate_w, lru_gate_b, lru_lam, w_ffn_in, w_ffn_out):
    nb, n_lat, d = x.shape
    depth = w_ada.shape[0]
    assert ctx.shape[1] == CTX_LEN and n_lat % TOKEN_TILE == 0 and nb < MOD_ROWS
    cvec = jnp.concatenate([c, c_ctx[None, :], jnp.zeros((MOD_ROWS - nb - 1, d), F32)], axis=0)
    mod = _modulation(cvec, w_ada, b_ada).reshape(depth, MOD_ROWS, N_MOD, d)
    rope_cs, rope_sn = _rope_tables(n_lat)
    xs = jnp.concatenate([ctx, x], axis=1)
    for l in range(depth):
        w_main = jnp.concatenate([w_in[l, :, :_GATE_COL0], w_in[l, :, _GATE_COL0 + M_GATES:]], axis=1).astype(BF16)
        w_gate_t = w_in[l, :, _GATE_COL0:_GATE_COL0 + M_GATES].T.astype(BF16)
        aq, ak, av, mq, mk, mv, mo, rx, ry, grow = _in_projection(
            xs, mod[l], norm_gain[l], w_main, w_gate_t, rope_cs, rope_sn)
        att = _attention(aq, ak, av, attn_sink[l])
        mem = _mlstm(mq, mk, mv, mo, grow, mlstm_gate_b[l], mlstm_norm[l])
        rec = _rglru(rx, ry, conv_w[l], conv_b[l], lru_gate_w[l], lru_gate_b[l], lru_lam[l])
        xs = _out_ffn(xs, att, mem, rec, mod[l], norm_gain[l], w_out[l].astype(BF16), w_ffn_in[l].astype(BF16),
                      w_ffn_out[l].astype(BF16))
    return xs[:, CTX_LEN:]
```

```python
import functools

import jax
import jax.numpy as jnp
import numpy as np
from jax import lax
from jax.experimental import pallas as pl
from jax.experimental.pallas import tpu as pltpu

F32 = jnp.float32
BF16 = jnp.bfloat16

D_MODEL = 1024
GRID_W = 64
CTX_LEN = 256
N_MOD = 6
NORM_EPS = 1e-6
ATT_HEADS = 8
ATT_KV_HEADS = 2
ATT_GROUP = ATT_HEADS // ATT_KV_HEADS
HEAD_DIM = 64
WINDOW = 128
ATT_BLOCK = 128
ROPE_BASE = 10000.0
ROPE_PAIRS = HEAD_DIM // 4
ATT_Q = ATT_HEADS * HEAD_DIM
ATT_KV = ATT_KV_HEADS * HEAD_DIM
MLSTM_HEADS = 4
MLSTM_HEAD_DIM = 64
M_W = MLSTM_HEADS * MLSTM_HEAD_DIM
M_GATES = 4 * MLSTM_HEADS
LRU_WIDTH = 256
LRU_BLOCKS = 4
LRU_BW = LRU_WIDTH // LRU_BLOCKS
LRU_C = 8.0
CONV_WIDTH = 4
CONV_LEFT = CONV_WIDTH // 2
D_FF = -(-8 * D_MODEL // (3 * 256)) * 256

LANES = 128
SUBLANES = 8
TOKEN_TILE = CTX_LEN
FF_CHUNK = 256
LRU_CHAINS = 4
MOD_ROWS = 16
VMEM_LIMIT = 56 * 1024 * 1024


def _params(sem):
    return pltpu.CompilerParams(dimension_semantics=sem, vmem_limit_bytes=VMEM_LIMIT)


def _dot(a, b):
    return jnp.dot(a, b, preferred_element_type=F32)


def _dot_nt(a, b):
    return lax.dot_general(a, b, (((1,), (1,)), ((), ())), preferred_element_type=F32)


def _dot_exact(a, b):
    return jnp.dot(a, b, preferred_element_type=F32, precision=lax.Precision.HIGHEST)


def _sigmoid(x):
    return 1.0 / (1.0 + jnp.exp(-x))


def _softplus(x):
    return jnp.maximum(x, 0.0) + jnp.log1p(jnp.exp(-jnp.abs(x)))


def _rms(x, g):
    return x * lax.rsqrt(jnp.mean(x * x, axis=-1, keepdims=True) + NORM_EPS) * g


def _mod_kernel(c_ref, w_ref, b_ref, o_ref):
    c = c_ref[...]
    s = (c * _sigmoid(c)).astype(BF16)
    o_ref[...] = _dot(s, w_ref[...].astype(BF16)) + b_ref[...]


def _modulation(cvec, w_ada, b_ada):
    depth, d, n = w_ada.shape
    tn = 1536
    return pl.pallas_call(
        _mod_kernel,
        grid=(depth, n // tn),
        in_specs=[
            pl.BlockSpec((MOD_ROWS, d), lambda l, j: (0, 0)),
            pl.BlockSpec((None, d, tn), lambda l, j: (l, 0, j)),
            pl.BlockSpec((None, 1, tn), lambda l, j: (l, 0, j)),
        ],
        out_specs=pl.BlockSpec((None, MOD_ROWS, tn), lambda l, j: (l, 0, j)),
        out_shape=jax.ShapeDtypeStruct((depth, MOD_ROWS, n), F32),
        compiler_params=_params(("arbitrary", "arbitrary")),
        name="modulation",
    )(cvec, w_ada, b_ada.reshape(depth, 1, n))


_IN_COLS = {}
_col = 0
for _name, _width in (("aq", ATT_Q), ("ak", ATT_KV), ("av", ATT_KV), ("mq", M_W), ("mk", M_W), ("mv", M_W), ("mo", M_W),
                      ("mg", M_GATES), ("rx", LRU_WIDTH), ("ry", LRU_WIDTH)):
    _IN_COLS[_name] = (_col, _col + _width)
    _col += _width
_TOKEN_MAJOR = ("ak", "mq", "mk", "mv", "mo", "rx", "ry")
_FEATURE_MAJOR = ("aq", "av", "mg")


def _split_in_weights(w):
    tok = jnp.concatenate([w[:, _IN_COLS[n][0]:_IN_COLS[n][1]] for n in _TOKEN_MAJOR], axis=1)
    feat = jnp.concatenate([w[:, _IN_COLS[n][0]:_IN_COLS[n][1]] for n in _FEATURE_MAJOR], axis=1).T
    return tok.astype(BF16), feat.astype(BF16)


def _rope_slab(x, cs, sn, first):
    swapped = jnp.where(first, pltpu.roll(x, LANES - ROPE_PAIRS, 1), pltpu.roll(x, ROPE_PAIRS, 1))
    return x * cs + swapped * sn


def _inproj_kernel(x_ref, mod_ref, g_ref, w_ref, wt_ref, cs_ref, sn_ref, cst_ref, snt_ref,
                   ak_ref, mq_ref, mk_ref, mv_ref, mo_ref, rx_ref, ry_ref, aqt_ref, avt_ref, gr_ref):
    x = x_ref[...]
    h = _rms(x, g_ref[0:1, :]) * (1.0 + mod_ref[1:2, :]) + mod_ref[0:1, :]
    hb = h.astype(BF16)
    pt = _dot_nt(wt_ref[...], hb)
    cst = cst_ref[...]
    snt = snt_ref[...]
    rp = ROPE_PAIRS
    for hd in range(ATT_HEADS):
        xs = pt[hd * HEAD_DIM:(hd + 1) * HEAD_DIM, :]
        swapped = jnp.concatenate([xs[rp:2 * rp], xs[0:rp], xs[3 * rp:4 * rp], xs[2 * rp:3 * rp]], axis=0)
        aqt_ref[hd * HEAD_DIM:(hd + 1) * HEAD_DIM, :] = (xs * cst + swapped * snt) * (HEAD_DIM ** -0.5)
    avt_ref[...] = pt[ATT_Q:ATT_Q + ATT_KV, :]
    gr_ref[...] = pt[ATT_Q + ATT_KV:, :]
    lane = lax.broadcasted_iota(jnp.int32, (x.shape[0], LANES), 1)
    first = (lane & (2 * ROPE_PAIRS - 1)) < ROPE_PAIRS
    ak_ref[...] = _rope_slab(_dot(hb, w_ref[:, 0:ATT_KV]), cs_ref[...], sn_ref[...], first)
    col = ATT_KV
    for ref in (mq_ref, mk_ref, mv_ref, mo_ref, rx_ref, ry_ref):
        n = ref.shape[-1]
        ref[...] = _dot(hb, w_ref[:, col:col + n])
        col += n


def _in_projection(xs, mod, gain, w_tok, w_feat, rope):
    nb, t, d = xs.shape
    nt = t // TOKEN_TILE
    tile = lambda n: pl.BlockSpec((None, TOKEN_TILE, n), lambda b, i: (b, i, 0))
    tile_t = lambda n: pl.BlockSpec((None, n, TOKEN_TILE), lambda b, i: (b, 0, i))
    const = lambda a: pl.BlockSpec(a.shape, lambda b, i: (0,) * a.ndim)
    width = lambda n: _IN_COLS[n][1] - _IN_COLS[n][0]
    out_shapes = [jax.ShapeDtypeStruct((nb, t, width(n)), F32) for n in _TOKEN_MAJOR]
    out_shapes += [jax.ShapeDtypeStruct((nb, width(n), t), F32) for n in _FEATURE_MAJOR]
    out_specs = [tile(width(n)) for n in _TOKEN_MAJOR] + [tile_t(width(n)) for n in _FEATURE_MAJOR]
    rope_cs, rope_sn, rope_cst, rope_snt = rope
    return pl.pallas_call(
        _inproj_kernel,
        grid=(nb, nt),
        in_specs=[
            tile(d),
            pl.BlockSpec((None, N_MOD, d), lambda b, i: (jnp.where(i == 0, nb, b), 0, 0)),
            const(gain), const(w_tok), const(w_feat),
            pl.BlockSpec((TOKEN_TILE, LANES), lambda b, i: (i, 0)),
            pl.BlockSpec((TOKEN_TILE, LANES), lambda b, i: (i, 0)),
            pl.BlockSpec((HEAD_DIM, TOKEN_TILE), lambda b, i: (0, i)),
            pl.BlockSpec((HEAD_DIM, TOKEN_TILE), lambda b, i: (0, i)),
        ],
        out_specs=out_specs,
        out_shape=out_shapes,
        compiler_params=_params(("parallel", "arbitrary")),
        name="in_projection",
    )(xs, mod, gain, w_tok, w_feat, rope_cs, rope_sn, rope_cst, rope_snt)


def _attn_kernel(sink_ref, qt_ref, kp_ref, kc_ref, kn_ref, kx_ref, vp_ref, vc_ref, vn_ref, vx_ref, o_ref):
    j = pl.program_id(1)
    nblk = pl.num_programs(1)
    ctx_blocks = CTX_LEN // ATT_BLOCK
    nq = ATT_BLOCK
    cols = ATT_GROUP * nq
    c = lax.broadcasted_iota(jnp.int32, (ATT_BLOCK, cols), 0)
    r = lax.broadcasted_iota(jnp.int32, (ATT_BLOCK, cols), 1) & (nq - 1)
    m_prev = (c >= r) & (j >= ctx_blocks + 1)
    m_next = (c <= r) & (j >= ctx_blocks) & (j <= nblk - 2)
    is_lat = j >= ctx_blocks
    col = lax.broadcasted_iota(jnp.int32, (1, cols), 1)
    neg = -jnp.inf
    kp = kp_ref[...].astype(BF16)
    kc = kc_ref[...].astype(BF16)
    kn = kn_ref[...].astype(BF16)
    kx = kx_ref[...].astype(BF16)
    zeros = jnp.zeros((HEAD_DIM, cols), BF16)
    for kh in range(ATT_KV_HEADS):
        hd0 = kh * ATT_GROUP
        qg = jnp.concatenate([qt_ref[(hd0 + g) * HEAD_DIM:(hd0 + g + 1) * HEAD_DIM, :] for g in range(ATT_GROUP)],
                             axis=1).astype(BF16)
        rhs = jnp.concatenate([qg, zeros] if kh == 0 else [zeros, qg], axis=0)
        sink = jnp.full((1, cols), sink_ref[hd0], F32)
        for g in range(1, ATT_GROUP):
            sink = jnp.where(col >= g * nq, sink_ref[hd0 + g], sink)
        s_p = jnp.where(m_prev, _dot(kp, rhs), neg)
        s_c = jnp.where(is_lat, _dot(kc, rhs), neg)
        s_n = jnp.where(m_next, _dot(kn, rhs), neg)
        s_x = _dot(kx, rhs)
        s_x0, s_x1 = s_x[:ATT_BLOCK], s_x[ATT_BLOCK:]
        m_el = jnp.maximum(jnp.maximum(jnp.maximum(s_p, s_c), jnp.maximum(s_n, s_x0)), s_x1)
        m = jnp.maximum(jnp.max(m_el, axis=0, keepdims=True), sink)
        p_p = jnp.exp(s_p - m)
        p_c = jnp.exp(s_c - m)
        p_n = jnp.exp(s_n - m)
        p_x0 = jnp.exp(s_x0 - m)
        p_x1 = jnp.exp(s_x1 - m)
        den = jnp.sum((p_p + p_c) + (p_n + p_x0) + p_x1, axis=0, keepdims=True) + jnp.exp(sink - m)
        vs = slice(kh * HEAD_DIM, (kh + 1) * HEAD_DIM)
        p_x = jnp.concatenate([p_x0, p_x1], axis=0).astype(BF16)
        o = (_dot(vp_ref[vs, :].astype(BF16), p_p.astype(BF16)) + _dot(vc_ref[vs, :].astype(BF16), p_c.astype(BF16))
             + _dot(vn_ref[vs, :].astype(BF16), p_n.astype(BF16)) + _dot(vx_ref[vs, :].astype(BF16), p_x))
        o = o / den
        for pair in range(ATT_GROUP // 2):
            two = jnp.concatenate([o[:, (2 * pair) * nq:(2 * pair + 1) * nq], o[:, (2 * pair + 1) * nq:(2 * pair + 2) * nq]],
                                  axis=0)
            lo = (hd0 + 2 * pair) * HEAD_DIM
            o_ref[:, lo:lo + 2 * HEAD_DIM] = two.T


def _attention(aqt, ak, avt, sink):
    nb, t, _ = ak.shape
    nblk = t // ATT_BLOCK
    prev = lambda j: jnp.maximum(j - 1, 0)
    cur = lambda j: j
    nxt = lambda j: jnp.minimum(j + 1, nblk - 1)
    kb = lambda f: pl.BlockSpec((None, ATT_BLOCK, ATT_KV), lambda b, j: (b, f(j), 0))
    vb = lambda f: pl.BlockSpec((None, ATT_KV, ATT_BLOCK), lambda b, j: (b, 0, f(j)))
    return pl.pallas_call(
        _attn_kernel,
        grid=(nb, nblk),
        in_specs=[
            pl.BlockSpec(memory_space=pltpu.SMEM),
            pl.BlockSpec((None, ATT_Q, ATT_BLOCK), lambda b, j: (b, 0, j)),
            kb(prev), kb(cur), kb(nxt), pl.BlockSpec((None, CTX_LEN, ATT_KV), lambda b, j: (b, 0, 0)),
            vb(prev), vb(cur), vb(nxt), pl.BlockSpec((None, ATT_KV, CTX_LEN), lambda b, j: (b, 0, 0)),
        ],
        out_specs=pl.BlockSpec((None, ATT_BLOCK, ATT_Q), lambda b, j: (b, j, 0)),
        out_shape=jax.ShapeDtypeStruct((nb, t, ATT_Q), F32),
        compiler_params=_params(("parallel", "arbitrary")),
        name="attention",
    )(sink, aqt, ak, ak, ak, ak, avt, avt, avt, avt)


def _seq_block(d, j, nblk):
    return jnp.where((d == 0) | (j == 0), j, nblk - j)


def _lane_scan(x, op, fill, rev):
    n = x.shape[-1]
    lane = lax.broadcasted_iota(jnp.int32, x.shape, 1)
    step = 1
    while step < n:
        if rev:
            shifted = jnp.where(lane < n - step, pltpu.roll(x, n - step, 1), fill)
        else:
            shifted = jnp.where(lane >= step, pltpu.roll(x, step, 1), fill)
        x = op(x, shifted)
        step *= 2
    return x


def _split3(x):
    hi = x.astype(BF16).astype(F32)
    mid = (x - hi).astype(BF16).astype(F32)
    lo = x - hi - mid
    return hi, mid, lo


def _mlstm_tables():
    H, dh = MLSTM_HEADS, MLSTM_HEAD_DIM
    n_pairs = H // 2
    n_slabs = H + 3 * n_pairs
    sel = np.zeros((LANES, n_slabs * LANES), np.float32)
    for part in range(3):
        base = part * 4 * H
        for h in range(H):
            sel[base + h, h * LANES:(h + 1) * LANES] = 1.0
        for qty in range(3):
            for p in range(n_pairs):
                slab = H + qty * n_pairs + p
                for half in range(2):
                    row = base + H * qty + 2 * p + half
                    sel[row, slab * LANES + half * dh:slab * LANES + (half + 1) * dh] = 1.0
    ones_bd = np.zeros((2 * TOKEN_TILE, LANES), np.float32)
    ones_bd[:TOKEN_TILE, :dh] = 1.0
    ones_bd[TOKEN_TILE:, dh:] = 1.0
    avg = np.zeros((3 * LANES, LANES), np.float32)
    for part in range(3):
        for half in range(2):
            avg[part * LANES + half * dh:part * LANES + (half + 1) * dh, half * dh:(half + 1) * dh] = 1.0 / dh
    return jnp.asarray(sel, BF16), jnp.asarray(ones_bd, BF16), jnp.asarray(avg, BF16)


def _mlstm_block_terms(rev, g, sel_ref, cols_ref, r_ref, gm_ref, slot):
    L = TOKEN_TILE
    H, dh = MLSTM_HEADS, MLSTM_HEAD_DIM
    ioff = 2 * H if rev else 0
    li = g[ioff:ioff + H, :]
    lf = -_softplus(-g[ioff + H:ioff + 2 * H, :])
    b = _lane_scan(lf, jnp.add, 0.0, rev)
    r = li - b
    cm = _lane_scan(r, jnp.maximum, -jnp.inf, rev)
    last = 0 if rev else L - 1
    g_tot = b[:, last:last + 1]
    a = g_tot - b + li
    m_loc = jnp.max(a, axis=1, keepdims=True)
    w = jnp.exp(a - m_loc)
    stacked = jnp.concatenate([cm, b, w, jnp.zeros((H, L), F32)], axis=0)
    hi, mid, lo = _split3(stacked)
    pad = jnp.zeros((LANES - 3 * 4 * H, L), F32)
    cols_ref[slot] = _dot(jnp.concatenate([hi, mid, lo, pad], axis=0).T.astype(BF16), sel_ref[...])
    r_ref[slot] = jnp.concatenate([r, jnp.zeros((SUBLANES - H, L), F32)], axis=0)
    lane_row = lax.broadcasted_iota(jnp.int32, (1, LANES), 1)

    def layout(col):
        rows = [jnp.broadcast_to(col, (H, LANES))]
        rows += [jnp.where(lane_row < dh, col[2 * p:2 * p + 1, :], col[2 * p + 1:2 * p + 2, :]) for p in range(H // 2)]
        rows += [jnp.zeros((SUBLANES - H - H // 2, LANES), F32)]
        return jnp.concatenate(rows, axis=0)

    gm_ref[slot] = jnp.concatenate([layout(g_tot), layout(m_loc)], axis=0)


def _mlstm_kernel(q_ref, k_ref, v_ref, o_ref, gr_ref, gnext_ref, gb_ref, ng_ref, sel_ref, ones_ref, avg_ref, out_ref,
                  hf_ref, c_ref, n_ref, m_ref, cols_ref, r_ref, gm_ref):
    d = pl.program_id(1)
    j = pl.program_id(2)
    nblk = pl.num_programs(2)
    L = TOKEN_TILE
    H, dh = MLSTM_HEADS, MLSTM_HEAD_DIM
    n_pairs = H // 2
    row0 = pl.multiple_of(_seq_block(d, j, nblk) * L, L)
    slot = j & 1

    def body(rev):
        @pl.when(j == 0)
        def _():
            c_ref[...] = jnp.zeros_like(c_ref)
            n_ref[...] = jnp.zeros_like(n_ref)
            m_ref[...] = jnp.zeros_like(m_ref)
            _mlstm_block_terms(rev, gr_ref[...] + gb_ref[...], sel_ref, cols_ref, r_ref, gm_ref, 0)

        r = r_ref[slot]
        gm = gm_ref[slot]
        slab = lambda i: cols_ref[slot, :, i * LANES:(i + 1) * LANES]
        m0 = m_ref[...]
        g_tot, m_loc = gm[0:SUBLANES], gm[SUBLANES:]
        m_new = jnp.maximum(g_tot + m0, m_loc)
        f_prev = jnp.exp(g_tot + m0 - m_new)
        f_loc = jnp.exp(m_loc - m_new)
        m_ref[...] = m_new
        lane = lax.broadcasted_iota(jnp.int32, (L, LANES), 1)
        first_head = lane < dh
        t_idx = lax.broadcasted_iota(jnp.int32, (L, LANES), 0)
        kt = (k_ref[...] * (dh ** -0.5)).T.astype(BF16)
        zeros_half = jnp.zeros((dh, L), BF16)
        bd_mask = ((lax.broadcasted_iota(jnp.int32, (LANES, LANES), 0) < dh)
                   == (lax.broadcasted_iota(jnp.int32, (LANES, LANES), 1) < dh))
        for p in range(n_pairs):
            ps = slice(p * LANES, (p + 1) * LANES)
            qb = q_ref[:, ps].astype(BF16)
            v = v_ref[:, ps]
            kt_pair = kt[ps, :]
            sm = []
            for half in range(2):
                h = 2 * p + half
                kt_h = kt[h * dh:(h + 1) * dh, :]
                kt_masked = jnp.concatenate([kt_h, zeros_half] if half == 0 else [zeros_half, kt_h], axis=0)
                s = _dot(qb, kt_masked)
                m_b = jnp.maximum(slab(h), m0[h:h + 1, :])
                for lt in range(L // LANES):
                    s_idx = lane + lt * LANES
                    vis = (s_idx >= t_idx) if rev else (s_idx <= t_idx)
                    e = jnp.exp(jnp.where(vis, r[h:h + 1, lt * LANES:(lt + 1) * LANES] - m_b, -jnp.inf))
                    sm.append((s[:, lt * LANES:(lt + 1) * LANES] * e).astype(BF16))
            sm = jnp.concatenate(sm, axis=1)
            v_bd = jnp.concatenate([jnp.where(first_head, v, 0.0), jnp.where(first_head, 0.0, v)], axis=0).astype(BF16)
            m0_pair = m0[H + p:H + p + 1, :]
            m_pair = jnp.maximum(slab(H + p), m0_pair)
            f_pair = jnp.exp(m0_pair - m_pair)
            num = _dot(sm, v_bd) + f_pair * _dot(qb, c_ref[p].astype(BF16))
            den = _dot(sm, ones_ref[...]) + f_pair * _dot(qb, n_ref[p].astype(BF16))
            hh = num / jnp.maximum(jnp.abs(den), jnp.exp(-(slab(H + n_pairs + p) + m_pair)))
            w_pair = slab(H + 2 * n_pairs + p)
            c_loc = jnp.where(bd_mask, _dot(kt_pair, (w_pair * v).astype(BF16)), 0.0)
            n_loc = jnp.where(bd_mask, _dot(kt_pair, w_pair.astype(BF16)), 0.0)
            c_ref[p] = f_prev[H + p:H + p + 1, :] * c_ref[p] + f_loc[H + p:H + p + 1, :] * c_loc
            n_ref[p] = f_prev[H + p:H + p + 1, :] * n_ref[p] + f_loc[H + p:H + p + 1, :] * n_loc
            if not rev:
                hf_ref[pl.ds(row0, L), ps] = hh
            else:
                ht = hf_ref[pl.ds(row0, L), ps] + hh
                mu = _dot(jnp.concatenate(_split3(ht), axis=1).astype(BF16), avg_ref[...])
                cen = ht - mu
                var = _dot(jnp.concatenate(_split3(cen * cen), axis=1).astype(BF16), avg_ref[...])
                y = cen * lax.rsqrt(var + NORM_EPS) * ng_ref[:, ps]
                out_ref[:, ps] = _sigmoid(o_ref[:, ps]) * y
        _mlstm_block_terms(rev, gnext_ref[...] + gb_ref[...], sel_ref, cols_ref, r_ref, gm_ref, 1 - slot)

    @pl.when(d == 0)
    def _():
        body(False)

    @pl.when(d == 1)
    def _():
        body(True)


def _mlstm(mq, mk, mv, mo, grow, gate_b, norm_g):
    nb, t, w = mq.shape
    nblk = t // TOKEN_TILE
    blk = lambda b, d, j: (b, _seq_block(d, j, nblk), 0)
    tile = pl.BlockSpec((None, TOKEN_TILE, w), blk)
    gb = gate_b.reshape(M_GATES, 1)
    ng = norm_g.reshape(1, w)
    sel, ones_bd, avg = _mlstm_tables()
    const = lambda a: pl.BlockSpec(a.shape, lambda b, d, j: (0,) * a.ndim)
    n_pairs = MLSTM_HEADS // 2
    return pl.pallas_call(
        _mlstm_kernel,
        grid=(nb, 2, nblk),
        in_specs=[
            tile, tile, tile, tile,
            pl.BlockSpec((None, M_GATES, TOKEN_TILE), lambda b, d, j: (b, 0, _seq_block(d, j, nblk))),
            pl.BlockSpec((None, M_GATES, TOKEN_TILE),
                         lambda b, d, j: (b, 0, _seq_block(d, jnp.minimum(j + 1, nblk - 1), nblk))),
            const(gb), const(ng), const(sel), const(ones_bd), const(avg),
        ],
        out_specs=pl.BlockSpec((None, TOKEN_TILE, w), lambda b, d, j: (b, jnp.where(d == 0, 0, _seq_block(d, j, nblk)), 0)),
        out_shape=jax.ShapeDtypeStruct((nb, t, w), F32),
        scratch_shapes=[
            pltpu.VMEM((t, w), F32),
            pltpu.VMEM((n_pairs, LANES, LANES), F32),
            pltpu.VMEM((n_pairs, LANES, LANES), F32),
            pltpu.VMEM((SUBLANES, LANES), F32),
            pltpu.VMEM((2, TOKEN_TILE, sel.shape[1]), F32),
            pltpu.VMEM((2, SUBLANES, TOKEN_TILE), F32),
            pltpu.VMEM((2, 2 * SUBLANES, LANES), F32),
        ],
        compiler_params=_params(("parallel", "arbitrary", "arbitrary")),
        name="mlstm",
    )(mq, mk, mv, mo, grow, grow, gb, ng, sel, ones_bd, avg)


def _lru_kernel(x_ref, xp_ref, xn_ref, y_ref, cw_ref, cb_ref, gw_ref, gb_ref, lam_ref, out_ref,
                hf_ref, xe_ref, carry_ref, a_ref, u_ref, hb_ref, hl_ref, al_ref):
    d = pl.program_id(1)
    j = pl.program_id(2)
    nblk = pl.num_programs(2)
    L = TOKEN_TILE
    pos = _seq_block(d, j, nblk)
    row0 = pl.multiple_of(pos * L, L)

    @pl.when(j == 0)
    def _():
        carry_ref[...] = jnp.zeros_like(carry_ref)

    has_prev = pos >= 2
    has_next = (pos >= 1) & (pos <= nblk - 2)
    xe_ref[0:SUBLANES, :] = jnp.where(has_prev, xp_ref[...], 0.0)
    xe_ref[SUBLANES:SUBLANES + L, :] = x_ref[...]
    xe_ref[SUBLANES + L:, :] = jnp.where(has_next, xn_ref[...], 0.0)
    seq = cb_ref[...]
    for tap in range(CONV_WIDTH):
        off = SUBLANES - CONV_LEFT + tap
        seq = seq + cw_ref[tap:tap + 1, :] * xe_ref[off:off + L, :]
    sb = seq.astype(BF16)
    r = 0.5 + 0.5 * jnp.tanh(0.5 * (_dot(sb, gw_ref[0]) + gb_ref[0:1, :]))
    i = 0.5 + 0.5 * jnp.tanh(0.5 * (_dot(sb, gw_ref[1]) + gb_ref[1:2, :]))
    log_a = (-LRU_C * _softplus(-lam_ref[...])) * r
    a0 = jnp.exp(log_a)
    th = jnp.tanh(log_a)
    u0 = jnp.sqrt(-2.0 * th) * lax.rsqrt(1.0 - th) * (i * seq)
    n_lt = LRU_WIDTH // LANES
    for lt in range(n_lt):
        a_ref[lt] = a0[:, lt * LANES:(lt + 1) * LANES]
        u_ref[lt] = u0[:, lt * LANES:(lt + 1) * LANES]
    seg = L // (LRU_CHAINS * SUBLANES)
    sub = lax.broadcasted_iota(jnp.int32, (SUBLANES, LRU_WIDTH), 0)
    rows = lambda c, i: pl.ds(c * seg * SUBLANES + i, SUBLANES, stride=seg)
    strided = lambda ref, c, i: jnp.concatenate([ref[lt, rows(c, i), :] for lt in range(n_lt)], axis=1)

    def scan(rev):
        steps = range(seg - 1, -1, -1) if rev else range(seg)
        chains = range(LRU_CHAINS - 1, -1, -1) if rev else range(LRU_CHAINS)
        edge = SUBLANES - 1 if rev else 0
        shift = lambda x, n: pltpu.roll(x, SUBLANES - n if rev else n, 0)
        ends, prods = {}, {}
        for c in chains:
            h = jnp.zeros((SUBLANES, LRU_WIDTH), F32)
            ac = jnp.ones((SUBLANES, LRU_WIDTH), F32)
            for i in steps:
                ai = strided(a_ref, c, i)
                h = ai * h + strided(u_ref, c, i)
                ac = ai * ac
                hl_ref[c * seg + i] = h
                al_ref[c * seg + i] = ac
            step = 1
            while step < SUBLANES:
                ok = (sub < SUBLANES - step) if rev else (sub >= step)
                h = jnp.where(ok, ac * shift(h, step) + h, h)
                ac = jnp.where(ok, ac * shift(ac, step), ac)
                step *= 2
            ends[c], prods[c] = h, ac
        carry = carry_ref[...]
        for c in chains:
            true_ends = ends[c] + prods[c] * carry
            carry_in = jnp.where(sub == edge, carry, shift(true_ends, 1))
            carry = true_ends[SUBLANES - 1 - edge:SUBLANES - edge, :]
            for i in range(seg):
                hi = hl_ref[c * seg + i] + al_ref[c * seg + i] * carry_in
                for lt in range(n_lt):
                    hb_ref[lt, rows(c, i), :] = hi[:, lt * LANES:(lt + 1) * LANES]
        carry_ref[...] = carry
        return jnp.concatenate([hb_ref[lt] for lt in range(n_lt)], axis=1)

    @pl.when(d == 0)
    def _():
        hf_ref[pl.ds(row0, L), :] = scan(False)

    @pl.when(d == 1)
    def _():
        h = hf_ref[pl.ds(row0, L), :] + scan(True)
        y = y_ref[...]
        gelu = 0.5 * y * (1.0 + jnp.tanh(np.sqrt(2.0 / np.pi).astype(np.float32) * (y + 0.044715 * (y * y * y))))
        out_ref[...] = h * gelu


def _rglru(rx, ry, conv_w, conv_b, gate_w, gate_b, lam):
    nb, t, w = rx.shape
    nblk = t // TOKEN_TILE
    per_tile = TOKEN_TILE // SUBLANES
    n8 = t // SUBLANES
    blk = lambda b, d, j: (b, _seq_block(d, j, nblk), 0)
    tile = pl.BlockSpec((None, TOKEN_TILE, w), blk)
    halo = lambda f: pl.BlockSpec((None, SUBLANES, w), lambda b, d, j: (b, f(_seq_block(d, j, nblk)), 0))
    prev8 = lambda p: jnp.maximum(p * per_tile - 1, 0)
    next8 = lambda p: jnp.minimum((p + 1) * per_tile, n8 - 1)
    eye = jnp.eye(LRU_BLOCKS, dtype=gate_w.dtype)
    gw = jnp.einsum('dgnij,nm->dgnimj', gate_w, eye).reshape(2, 2, w, w).astype(BF16)
    cb = conv_b.reshape(1, w)
    return pl.pallas_call(
        _lru_kernel,
        grid=(nb, 2, nblk),
        in_specs=[
            tile, halo(prev8), halo(next8), tile,
            pl.BlockSpec(conv_w.shape, lambda b, d, j: (0, 0)),
            pl.BlockSpec(cb.shape, lambda b, d, j: (0, 0)),
            pl.BlockSpec((None, 2, w, w), lambda b, d, j: (d, 0, 0, 0)),
            pl.BlockSpec((None, 2, w), lambda b, d, j: (d, 0, 0)),
            pl.BlockSpec((None, 1, w), lambda b, d, j: (d, 0, 0)),
        ],
        out_specs=pl.BlockSpec((None, TOKEN_TILE, w), lambda b, d, j: (b, jnp.where(d == 0, 0, _seq_block(d, j, nblk)), 0)),
        out_shape=jax.ShapeDtypeStruct((nb, t, w), F32),
        scratch_shapes=[
            pltpu.VMEM((t, w), F32),
            pltpu.VMEM((TOKEN_TILE + 2 * SUBLANES, w), F32),
            pltpu.VMEM((1, w), F32),
            pltpu.VMEM((w // LANES, TOKEN_TILE, LANES), F32),
            pltpu.VMEM((w // LANES, TOKEN_TILE, LANES), F32),
            pltpu.VMEM((w // LANES, TOKEN_TILE, LANES), F32),
            pltpu.VMEM((TOKEN_TILE // SUBLANES, SUBLANES, w), F32),
            pltpu.VMEM((TOKEN_TILE // SUBLANES, SUBLANES, w), F32),
        ],
        compiler_params=_params(("parallel", "arbitrary", "arbitrary")),
        name="rglru",
    )(rx, rx, rx, ry, conv_w, cb, gw, gate_b, lam.reshape(2, 1, w))


def _outffn_kernel(x_ref, att_ref, mem_ref, rec_ref, mod_ref, g_ref, wo_ref, wi_ref, wd_ref, out_ref, act_ref):
    mix = (_dot(att_ref[...].astype(BF16), wo_ref[0:ATT_Q, :])
           + _dot(mem_ref[...].astype(BF16), wo_ref[ATT_Q:ATT_Q + M_W, :])
           + _dot(rec_ref[...].astype(BF16), wo_ref[ATT_Q + M_W:, :]))
    x1 = x_ref[...] + mod_ref[2:3, :] * _rms(mix, g_ref[1:2, :])
    h = (_rms(x1, g_ref[2:3, :]) * (1.0 + mod_ref[4:5, :]) + mod_ref[3:4, :]).astype(BF16)
    for c0 in range(0, D_FF, FF_CHUNK):
        gate = _dot(h, wi_ref[:, c0:c0 + FF_CHUNK])
        up = _dot(h, wi_ref[:, D_FF + c0:D_FF + c0 + FF_CHUNK])
        half = 0.5 * gate
        act_ref[:, c0:c0 + FF_CHUNK] = ((half + half * jnp.tanh(half)) * up).astype(BF16)
    f = _dot(act_ref[...], wd_ref[...])
    out_ref[...] = x1 + mod_ref[5:6, :] * _rms(f, g_ref[3:4, :])


def _out_ffn(xs, att, mem, rec, mod, gain, w_out, w_ffn_in, w_down):
    nb, t, d = xs.shape
    nt = t // TOKEN_TILE
    tile = lambda n: pl.BlockSpec((None, TOKEN_TILE, n), lambda b, i: (b, i, 0))
    const = lambda a: pl.BlockSpec(a.shape, lambda b, i: (0,) * a.ndim, pipeline_mode=pl.Buffered(1))
    return pl.pallas_call(
        _outffn_kernel,
        grid=(nb, nt),
        in_specs=[
            tile(d), tile(ATT_Q), tile(M_W), tile(LRU_WIDTH),
            pl.BlockSpec((None, N_MOD, d), lambda b, i: (jnp.where(i == 0, nb, b), 0, 0)),
            const(gain), const(w_out), const(w_ffn_in), const(w_down),
        ],
        out_specs=tile(d),
        out_shape=jax.ShapeDtypeStruct((nb, t, d), F32),
        scratch_shapes=[pltpu.VMEM((TOKEN_TILE, D_FF), BF16)],
        compiler_params=_params(("parallel", "arbitrary")),
        name="out_ffn",
    )(xs, att, mem, rec, mod, gain, w_out, w_ffn_in, w_down)


def _rope_tables(n_lat):
    t = jnp.arange(n_lat)
    row = (t // GRID_W).astype(F32)
    col = (t % GRID_W).astype(F32)
    freqs = ROPE_BASE ** (-jnp.arange(ROPE_PAIRS, dtype=F32) / ROPE_PAIRS)
    ang_r = row[:, None] * freqs
    ang_c = col[:, None] * freqs
    cs = jnp.concatenate([jnp.cos(ang_r), jnp.cos(ang_r), jnp.cos(ang_c), jnp.cos(ang_c)], axis=-1)
    sn = jnp.concatenate([-jnp.sin(ang_r), jnp.sin(ang_r), -jnp.sin(ang_c), jnp.sin(ang_c)], axis=-1)
    cs = jnp.concatenate([jnp.ones((CTX_LEN, HEAD_DIM), F32), cs], axis=0)
    sn = jnp.concatenate([jnp.zeros((CTX_LEN, HEAD_DIM), F32), sn], axis=0)
    return jnp.tile(cs, (1, LANES // HEAD_DIM)), jnp.tile(sn, (1, LANES // HEAD_DIM)), cs.T, sn.T


def kernel(x, c, ctx, c_ctx, w_ada, b_ada, norm_gain, w_in, w_out, attn_sink, mlstm_gate_b, mlstm_norm, conv_w, conv_b,
           lru_gate_w, lru_gate_b, lru_lam, w_ffn_in, w_ffn_out):
    nb, n_lat, d = x.shape
    depth = w_ada.shape[0]
    assert ctx.shape[1] == CTX_LEN and n_lat % TOKEN_TILE == 0 and nb < MOD_ROWS
    cvec = jnp.concatenate([c, c_ctx[None, :], jnp.zeros((MOD_ROWS - nb - 1, d), F32)], axis=0)
    mod = _modulation(cvec, w_ada, b_ada).reshape(depth, MOD_ROWS, N_MOD, d)
    rope = _rope_tables(n_lat)
    xs = jnp.concatenate([ctx, x], axis=1)
    for l in range(depth):
        w_tok, w_feat = _split_in_weights(w_in[l])
        ak, mq, mk, mv, mo, rx, ry, aqt, avt, grow = _in_projection(xs, mod[l], norm_gain[l], w_tok, w_feat, rope)
        att = _attention(aqt, ak, avt, attn_sink[l])
        mem = _mlstm(mq, mk, mv, mo, grow, mlstm_gate_b[l], mlstm_norm[l])
        rec = _rglru(rx, ry, conv_w[l], conv_b[l], lru_gate_w[l], lru_gate_b[l], lru_lam[l])
        xs = _out_ffn(xs, att, mem, rec, mod[l], norm_gain[l], w_out[l].astype(BF16), w_ffn_in[l].astype(BF16),
                      w_ffn_out[l].astype(BF16))
    return xs[:, CTX_LEN:]
```

```python
import functools

import jax
import jax.numpy as jnp
import numpy as np
from jax import lax
from jax.experimental import pallas as pl
from jax.experimental.pallas import tpu as pltpu

F32 = jnp.float32
BF16 = jnp.bfloat16

D_MODEL = 1024
GRID_W = 64
CTX_LEN = 256
N_MOD = 6
NORM_EPS = 1e-6
ATT_HEADS = 8
ATT_KV_HEADS = 2
ATT_GROUP = ATT_HEADS // ATT_KV_HEADS
HEAD_DIM = 64
WINDOW = 128
ATT_BLOCK = 128
ROPE_BASE = 10000.0
ROPE_PAIRS = HEAD_DIM // 4
ATT_Q = ATT_HEADS * HEAD_DIM
ATT_KV = ATT_KV_HEADS * HEAD_DIM
MLSTM_HEADS = 4
MLSTM_HEAD_DIM = 64
M_W = MLSTM_HEADS * MLSTM_HEAD_DIM
M_GATES = 4 * MLSTM_HEADS
LRU_WIDTH = 256
LRU_BLOCKS = 4
LRU_BW = LRU_WIDTH // LRU_BLOCKS
LRU_C = 8.0
CONV_WIDTH = 4
CONV_LEFT = CONV_WIDTH // 2
D_FF = -(-8 * D_MODEL // (3 * 256)) * 256
LOG2_E = float(np.log2(np.e))
Q_SCALE = HEAD_DIM ** -0.5 * LOG2_E

LANES = 128
SUBLANES = 8
TOKEN_TILE = CTX_LEN
FF_CHUNK = 256
LRU_CHAINS = 4
MOD_ROWS = 16
VMEM_LIMIT = 56 * 1024 * 1024


def _params(sem):
    return pltpu.CompilerParams(dimension_semantics=sem, vmem_limit_bytes=VMEM_LIMIT)


def _dot(a, b):
    return jnp.dot(a, b, preferred_element_type=F32)


def _dot_nt(a, b):
    return lax.dot_general(a, b, (((1,), (1,)), ((), ())), preferred_element_type=F32)


def _dot_exact(a, b):
    return jnp.dot(a, b, preferred_element_type=F32, precision=lax.Precision.HIGHEST)


def _sigmoid(x):
    return 1.0 / (1.0 + jnp.exp(-x))


def _softplus(x):
    return jnp.maximum(x, 0.0) + jnp.log1p(jnp.exp(-jnp.abs(x)))


def _rms(x, g):
    return x * lax.rsqrt(jnp.mean(x * x, axis=-1, keepdims=True) + NORM_EPS) * g


def _mod_kernel(c_ref, w_ref, b_ref, o_ref):
    c = c_ref[...]
    s = (c * _sigmoid(c)).astype(BF16)
    o_ref[...] = _dot(s, w_ref[...].astype(BF16)) + b_ref[...]


def _modulation(cvec, w_ada, b_ada):
    depth, d, n = w_ada.shape
    tn = 1536
    return pl.pallas_call(
        _mod_kernel,
        grid=(depth, n // tn),
        in_specs=[
            pl.BlockSpec((MOD_ROWS, d), lambda l, j: (0, 0)),
            pl.BlockSpec((None, d, tn), lambda l, j: (l, 0, j)),
            pl.BlockSpec((None, 1, tn), lambda l, j: (l, 0, j)),
        ],
        out_specs=pl.BlockSpec((None, MOD_ROWS, tn), lambda l, j: (l, 0, j)),
        out_shape=jax.ShapeDtypeStruct((depth, MOD_ROWS, n), F32),
        compiler_params=_params(("arbitrary", "arbitrary")),
        name="modulation",
    )(cvec, w_ada, b_ada.reshape(depth, 1, n))


_IN_COLS = {}
_col = 0
for _name, _width in (("aq", ATT_Q), ("ak", ATT_KV), ("av", ATT_KV), ("mq", M_W), ("mk", M_W), ("mv", M_W), ("mo", M_W),
                      ("mg", M_GATES), ("rx", LRU_WIDTH), ("ry", LRU_WIDTH)):
    _IN_COLS[_name] = (_col, _col + _width)
    _col += _width
_TOKEN_MAJOR = ("ak", "mq", "mk", "mv", "mo", "rx", "ry")
_FEATURE_MAJOR = ("aq", "av", "mg")


def _split_in_weights(w):
    tok = jnp.concatenate([w[:, _IN_COLS[n][0]:_IN_COLS[n][1]] for n in _TOKEN_MAJOR], axis=1)
    feat = jnp.concatenate([w[:, _IN_COLS[n][0]:_IN_COLS[n][1]] for n in _FEATURE_MAJOR], axis=1).T
    return tok.astype(BF16), feat.astype(BF16)


def _rope_slab(x, cs, sn, first):
    swapped = jnp.where(first, pltpu.roll(x, LANES - ROPE_PAIRS, 1), pltpu.roll(x, ROPE_PAIRS, 1))
    return x * cs + swapped * sn


def _stream_specs(xs, tile):
    if not isinstance(xs, tuple):
        return [tile(xs.shape[-1])]
    d = xs[0].shape[-1]
    return [pl.BlockSpec((None, TOKEN_TILE, d), lambda b, i: (b, 0, 0)),
            pl.BlockSpec((None, TOKEN_TILE, d), lambda b, i: (b, jnp.maximum(i - 1, 0), 0))]


def _stream_tile(refs):
    if len(refs) == 1:
        return refs[0][...]
    return jnp.where(pl.program_id(1) == 0, refs[0][...], refs[1][...])


def _inproj_kernel(n_streams, *refs):
    x = _stream_tile(refs[:n_streams])
    (mod_ref, g_ref, w_ref, wt_ref, cs_ref, sn_ref, cst_ref, snt_ref,
     ak_ref, mq_ref, mk_ref, mv_ref, mo_ref, rx_ref, ry_ref, aqt_ref, avt_ref, gr_ref) = refs[n_streams:]
    h = _rms(x, g_ref[0:1, :]) * (1.0 + mod_ref[1:2, :]) + mod_ref[0:1, :]
    hb = h.astype(BF16)
    pt = _dot_nt(wt_ref[...], hb)
    cst = cst_ref[...]
    snt = snt_ref[...]
    rp = ROPE_PAIRS
    for hd in range(ATT_HEADS):
        xs = pt[hd * HEAD_DIM:(hd + 1) * HEAD_DIM, :]
        swapped = jnp.concatenate([xs[rp:2 * rp], xs[0:rp], xs[3 * rp:4 * rp], xs[2 * rp:3 * rp]], axis=0)
        aqt_ref[hd * HEAD_DIM:(hd + 1) * HEAD_DIM, :] = (xs * cst + swapped * snt) * Q_SCALE
    avt_ref[...] = pt[ATT_Q:ATT_Q + ATT_KV, :]
    gr_ref[...] = pt[ATT_Q + ATT_KV:, :]
    lane = lax.broadcasted_iota(jnp.int32, (x.shape[0], LANES), 1)
    first = (lane & (2 * ROPE_PAIRS - 1)) < ROPE_PAIRS
    ak_ref[...] = _rope_slab(_dot(hb, w_ref[:, 0:ATT_KV]), cs_ref[...], sn_ref[...], first)
    col = ATT_KV
    for ref in (mq_ref, mk_ref, mv_ref, mo_ref, rx_ref, ry_ref):
        n = ref.shape[-1]
        ref[...] = _dot(hb, w_ref[:, col:col + n])
        col += n


def _in_projection(xs, mod, gain, w_tok, w_feat, rope):
    streams = xs if isinstance(xs, tuple) else (xs,)
    nb = streams[0].shape[0]
    t = sum(s.shape[1] for s in streams)
    nt = t // TOKEN_TILE
    tile = lambda n: pl.BlockSpec((None, TOKEN_TILE, n), lambda b, i: (b, i, 0))
    tile_t = lambda n: pl.BlockSpec((None, n, TOKEN_TILE), lambda b, i: (b, 0, i))
    const = lambda a: pl.BlockSpec(a.shape, lambda b, i: (0,) * a.ndim)
    width = lambda n: _IN_COLS[n][1] - _IN_COLS[n][0]
    out_shapes = [jax.ShapeDtypeStruct((nb, t, width(n)), F32) for n in _TOKEN_MAJOR]
    out_shapes += [jax.ShapeDtypeStruct((nb, width(n), t), F32) for n in _FEATURE_MAJOR]
    out_specs = [tile(width(n)) for n in _TOKEN_MAJOR] + [tile_t(width(n)) for n in _FEATURE_MAJOR]
    rope_cs, rope_sn, rope_cst, rope_snt = rope
    return pl.pallas_call(
        functools.partial(_inproj_kernel, len(streams)),
        grid=(nb, nt),
        in_specs=_stream_specs(xs, tile) + [
            pl.BlockSpec((None, N_MOD, mod.shape[-1]), lambda b, i: (jnp.where(i == 0, nb, b), 0, 0)),
            const(gain), const(w_tok), const(w_feat),
            pl.BlockSpec((TOKEN_TILE, LANES), lambda b, i: (i, 0)),
            pl.BlockSpec((TOKEN_TILE, LANES), lambda b, i: (i, 0)),
            pl.BlockSpec((HEAD_DIM, TOKEN_TILE), lambda b, i: (0, i)),
            pl.BlockSpec((HEAD_DIM, TOKEN_TILE), lambda b, i: (0, i)),
        ],
        out_specs=out_specs,
        out_shape=out_shapes,
        compiler_params=_params(("parallel", "arbitrary")),
        name="in_projection",
    )(*streams, mod, gain, w_tok, w_feat, rope_cs, rope_sn, rope_cst, rope_snt)


def _attn_kernel(sink_ref, qt_ref, kp_ref, kc_ref, kn_ref, kx_ref, vp_ref, vc_ref, vn_ref, vx_ref, o_ref):
    j = pl.program_id(1)
    nblk = pl.num_programs(1)
    ctx_blocks = CTX_LEN // ATT_BLOCK
    nq = ATT_BLOCK
    cols = ATT_GROUP * nq
    c = lax.broadcasted_iota(jnp.int32, (ATT_BLOCK, cols), 0)
    r = lax.broadcasted_iota(jnp.int32, (ATT_BLOCK, cols), 1) & (nq - 1)
    m_prev = (c >= r) & (j >= ctx_blocks + 1)
    m_next = (c <= r) & (j >= ctx_blocks) & (j <= nblk - 2)
    is_lat = j >= ctx_blocks
    col = lax.broadcasted_iota(jnp.int32, (1, cols), 1)
    neg = -jnp.inf
    kp = kp_ref[...].astype(BF16)
    kc = kc_ref[...].astype(BF16)
    kn = kn_ref[...].astype(BF16)
    kx = kx_ref[...].astype(BF16)
    zeros = jnp.zeros((HEAD_DIM, cols), BF16)
    scores = []
    for kh in range(ATT_KV_HEADS):
        hd0 = kh * ATT_GROUP
        qg = jnp.concatenate([qt_ref[(hd0 + g) * HEAD_DIM:(hd0 + g + 1) * HEAD_DIM, :] for g in range(ATT_GROUP)],
                             axis=1).astype(BF16)
        rhs = jnp.concatenate([qg, zeros] if kh == 0 else [zeros, qg], axis=0)
        s_x = _dot(kx, rhs)
        scores.append((jnp.where(m_prev, _dot(kp, rhs), neg), jnp.where(is_lat, _dot(kc, rhs), neg),
                       jnp.where(m_next, _dot(kn, rhs), neg), s_x[:ATT_BLOCK], s_x[ATT_BLOCK:]))
    for kh in range(ATT_KV_HEADS):
        hd0 = kh * ATT_GROUP
        sink = jnp.full((1, cols), sink_ref[hd0], F32)
        for g in range(1, ATT_GROUP):
            sink = jnp.where(col >= g * nq, sink_ref[hd0 + g], sink)
        sink = sink * LOG2_E
        s_p, s_c, s_n, s_x0, s_x1 = scores[kh]
        m_el = jnp.maximum(jnp.maximum(jnp.maximum(s_p, s_c), jnp.maximum(s_n, s_x0)), s_x1)
        m = jnp.maximum(jnp.max(m_el, axis=0, keepdims=True), sink)
        p_p = jnp.exp2(s_p - m)
        p_c = jnp.exp2(s_c - m)
        p_n = jnp.exp2(s_n - m)
        p_x0 = jnp.exp2(s_x0 - m)
        p_x1 = jnp.exp2(s_x1 - m)
        den = jnp.sum((p_p + p_c) + (p_n + p_x0) + p_x1, axis=0, keepdims=True) + jnp.exp2(sink - m)
        vs = slice(kh * HEAD_DIM, (kh + 1) * HEAD_DIM)
        p_x = jnp.concatenate([p_x0, p_x1], axis=0).astype(BF16)
        o = (_dot(vp_ref[vs, :].astype(BF16), p_p.astype(BF16)) + _dot(vc_ref[vs, :].astype(BF16), p_c.astype(BF16))
             + _dot(vn_ref[vs, :].astype(BF16), p_n.astype(BF16)) + _dot(vx_ref[vs, :].astype(BF16), p_x))
        o = o / den
        for pair in range(ATT_GROUP // 2):
            two = jnp.concatenate([o[:, (2 * pair) * nq:(2 * pair + 1) * nq], o[:, (2 * pair + 1) * nq:(2 * pair + 2) * nq]],
                                  axis=0)
            lo = (hd0 + 2 * pair) * HEAD_DIM
            o_ref[:, lo:lo + 2 * HEAD_DIM] = two.T


def _attention(aqt, ak, avt, sink):
    nb, t, _ = ak.shape
    nblk = t // ATT_BLOCK
    prev = lambda j: jnp.maximum(j - 1, 0)
    cur = lambda j: j
    nxt = lambda j: jnp.minimum(j + 1, nblk - 1)
    kb = lambda f: pl.BlockSpec((None, ATT_BLOCK, ATT_KV), lambda b, j: (b, f(j), 0))
    vb = lambda f: pl.BlockSpec((None, ATT_KV, ATT_BLOCK), lambda b, j: (b, 0, f(j)))
    return pl.pallas_call(
        _attn_kernel,
        grid=(nb, nblk),
        in_specs=[
            pl.BlockSpec(memory_space=pltpu.SMEM),
            pl.BlockSpec((None, ATT_Q, ATT_BLOCK), lambda b, j: (b, 0, j)),
            kb(prev), kb(cur), kb(nxt), pl.BlockSpec((None, CTX_LEN, ATT_KV), lambda b, j: (b, 0, 0)),
            vb(prev), vb(cur), vb(nxt), pl.BlockSpec((None, ATT_KV, CTX_LEN), lambda b, j: (b, 0, 0)),
        ],
        out_specs=pl.BlockSpec((None, ATT_BLOCK, ATT_Q), lambda b, j: (b, j, 0)),
        out_shape=jax.ShapeDtypeStruct((nb, t, ATT_Q), F32),
        compiler_params=_params(("parallel", "arbitrary")),
        name="attention",
    )(sink, aqt, ak, ak, ak, ak, avt, avt, avt, avt)


def _seq_block(d, j, nblk):
    return jnp.where((d == 0) | (j == 0), j, nblk - j)


def _lane_scan(x, op, fill, rev):
    n = x.shape[-1]
    lane = lax.broadcasted_iota(jnp.int32, x.shape, 1)
    step = 1
    while step < n:
        if rev:
            shifted = jnp.where(lane < n - step, pltpu.roll(x, n - step, 1), fill)
        else:
            shifted = jnp.where(lane >= step, pltpu.roll(x, step, 1), fill)
        x = op(x, shifted)
        step *= 2
    return x


def _split3(x):
    hi = x.astype(BF16).astype(F32)
    mid = (x - hi).astype(BF16).astype(F32)
    lo = x - hi - mid
    return hi, mid, lo


def _mlstm_tables():
    H, dh = MLSTM_HEADS, MLSTM_HEAD_DIM
    n_pairs = H // 2
    n_slabs = H + 3 * n_pairs
    sel = np.zeros((LANES, n_slabs * LANES), np.float32)
    for part in range(3):
        base = part * 4 * H
        for h in range(H):
            sel[base + h, h * LANES:(h + 1) * LANES] = 1.0
        for qty in range(3):
            for p in range(n_pairs):
                slab = H + qty * n_pairs + p
                for half in range(2):
                    row = base + H * qty + 2 * p + half
                    sel[row, slab * LANES + half * dh:slab * LANES + (half + 1) * dh] = 1.0
    ones_bd = np.zeros((2 * TOKEN_TILE, LANES), np.float32)
    ones_bd[:TOKEN_TILE, :dh] = 1.0
    ones_bd[TOKEN_TILE:, dh:] = 1.0
    avg = np.zeros((3 * LANES, LANES), np.float32)
    for part in range(3):
        for half in range(2):
            avg[part * LANES + half * dh:part * LANES + (half + 1) * dh, half * dh:(half + 1) * dh] = 1.0 / dh
    return jnp.asarray(sel, BF16), jnp.asarray(ones_bd, BF16), jnp.asarray(avg, BF16)


def _mlstm_block_terms(rev, g, tri_ref):
    L = TOKEN_TILE
    H = MLSTM_HEADS
    ioff = 2 * H if rev else 0
    li = g[ioff:ioff + H, :]
    lf = -_softplus(-g[ioff + H:ioff + 2 * H, :])
    parts = _dot(jnp.concatenate(_split3(lf), axis=0).astype(BF16), tri_ref[...])
    b = parts[0:H] + parts[H:2 * H] + parts[2 * H:3 * H]
    r = li - b
    cm = _lane_scan(r, jnp.maximum, -jnp.inf, rev)
    last = 0 if rev else L - 1
    g_tot = b[:, last:last + 1]
    a = g_tot - b + li
    m_loc = jnp.max(a, axis=1, keepdims=True)
    w = jnp.exp(a - m_loc)
    return r, cm, b, w, g_tot, m_loc


def _mlstm_stash_terms(terms, sel_ref, cols_ref, r_ref, gm_ref, slot):
    L = TOKEN_TILE
    H, dh = MLSTM_HEADS, MLSTM_HEAD_DIM
    r, cm, b, w, g_tot, m_loc = terms
    stacked = jnp.concatenate([cm, b, w, jnp.zeros((H, L), F32)], axis=0)
    hi, mid, lo = _split3(stacked)
    pad = jnp.zeros((LANES - 3 * 4 * H, L), F32)
    cols_ref[slot] = _dot(jnp.concatenate([hi, mid, lo, pad], axis=0).T.astype(BF16), sel_ref[...])
    r_ref[slot] = jnp.concatenate([r, jnp.zeros((SUBLANES - H, L), F32)], axis=0)
    lane_row = lax.broadcasted_iota(jnp.int32, (1, LANES), 1)

    def layout(col):
        rows = [jnp.broadcast_to(col, (H, LANES))]
        rows += [jnp.where(lane_row < dh, col[2 * p:2 * p + 1, :], col[2 * p + 1:2 * p + 2, :]) for p in range(H // 2)]
        rows += [jnp.zeros((SUBLANES - H - H // 2, LANES), F32)]
        return jnp.concatenate(rows, axis=0)

    gm_ref[slot] = jnp.concatenate([layout(g_tot), layout(m_loc)], axis=0)


def _mlstm_kernel(q_ref, k_ref, v_ref, o_ref, gr_ref, gnext_ref, gb_ref, ng_ref, tri_ref, sel_ref, ones_ref, avg_ref, out_ref,
                  hf_ref, c_ref, n_ref, m_ref, cols_ref, r_ref, gm_ref):
    d = pl.program_id(1)
    j = pl.program_id(2)
    nblk = pl.num_programs(2)
    L = TOKEN_TILE
    H, dh = MLSTM_HEADS, MLSTM_HEAD_DIM
    n_pairs = H // 2
    row0 = pl.multiple_of(_seq_block(d, j, nblk) * L, L)
    slot = j & 1

    def body(rev):
        @pl.when(j == 0)
        def _():
            c_ref[...] = jnp.zeros_like(c_ref)
            n_ref[...] = jnp.zeros_like(n_ref)
            m_ref[...] = jnp.zeros_like(m_ref)
            _mlstm_stash_terms(_mlstm_block_terms(rev, gr_ref[...] + gb_ref[...], tri_ref), sel_ref, cols_ref, r_ref, gm_ref, 0)

        next_terms = _mlstm_block_terms(rev, gnext_ref[...] + gb_ref[...], tri_ref)

        r = r_ref[slot]
        gm = gm_ref[slot]
        slab = lambda i: cols_ref[slot, :, i * LANES:(i + 1) * LANES]
        m0 = m_ref[...]
        g_tot, m_loc = gm[0:SUBLANES], gm[SUBLANES:]
        m_new = jnp.maximum(g_tot + m0, m_loc)
        f_prev = jnp.exp(g_tot + m0 - m_new)
        f_loc = jnp.exp(m_loc - m_new)
        m_ref[...] = m_new
        lane = lax.broadcasted_iota(jnp.int32, (L, LANES), 1)
        first_head = lane < dh
        t_idx = lax.broadcasted_iota(jnp.int32, (L, LANES), 0)
        kt = (k_ref[...] * (dh ** -0.5)).T.astype(BF16)
        zeros_half = jnp.zeros((dh, L), BF16)
        bd_mask = ((lax.broadcasted_iota(jnp.int32, (LANES, LANES), 0) < dh)
                   == (lax.broadcasted_iota(jnp.int32, (LANES, LANES), 1) < dh))
        for p in range(n_pairs):
            ps = slice(p * LANES, (p + 1) * LANES)
            qb = q_ref[:, ps].astype(BF16)
            v = v_ref[:, ps]
            kt_pair = kt[ps, :]
            sm = []
            for half in range(2):
                h = 2 * p + half
                kt_h = kt[h * dh:(h + 1) * dh, :]
                kt_masked = jnp.concatenate([kt_h, zeros_half] if half == 0 else [zeros_half, kt_h], axis=0)
                s = _dot(qb, kt_masked)
                m_b = jnp.maximum(slab(h), m0[h:h + 1, :])
                for lt in range(L // LANES):
                    s_idx = lane + lt * LANES
                    vis = (s_idx >= t_idx) if rev else (s_idx <= t_idx)
                    e = jnp.exp(jnp.where(vis, r[h:h + 1, lt * LANES:(lt + 1) * LANES] - m_b, -jnp.inf))
                    sm.append((s[:, lt * LANES:(lt + 1) * LANES] * e).astype(BF16))
            sm = jnp.concatenate(sm, axis=1)
            v_bd = jnp.concatenate([jnp.where(first_head, v, 0.0), jnp.where(first_head, 0.0, v)], axis=0).astype(BF16)
            m0_pair = m0[H + p:H + p + 1, :]
            m_pair = jnp.maximum(slab(H + p), m0_pair)
            f_pair = jnp.exp(m0_pair - m_pair)
            num = _dot(sm, v_bd) + f_pair * _dot(qb, c_ref[p].astype(BF16))
            den = _dot(sm, ones_ref[...]) + f_pair * _dot(qb, n_ref[p].astype(BF16))
            hh = num / jnp.maximum(jnp.abs(den), jnp.exp(-(slab(H + n_pairs + p) + m_pair)))
            w_pair = slab(H + 2 * n_pairs + p)
            c_loc = jnp.where(bd_mask, _dot(kt_pair, (w_pair * v).astype(BF16)), 0.0)
            n_loc = jnp.where(bd_mask, _dot(kt_pair, w_pair.astype(BF16)), 0.0)
            c_ref[p] = f_prev[H + p:H + p + 1, :] * c_ref[p] + f_loc[H + p:H + p + 1, :] * c_loc
            n_ref[p] = f_prev[H + p:H + p + 1, :] * n_ref[p] + f_loc[H + p:H + p + 1, :] * n_loc
            if not rev:
                hf_ref[pl.ds(row0, L), ps] = hh
            else:
                ht = hf_ref[pl.ds(row0, L), ps] + hh
                mu = _dot(jnp.concatenate(_split3(ht), axis=1).astype(BF16), avg_ref[...])
                cen = ht - mu
                var = _dot(jnp.concatenate(_split3(cen * cen), axis=1).astype(BF16), avg_ref[...])
                y = cen * lax.rsqrt(var + NORM_EPS) * ng_ref[:, ps]
                out_ref[:, ps] = _sigmoid(o_ref[:, ps]) * y
        _mlstm_stash_terms(next_terms, sel_ref, cols_ref, r_ref, gm_ref, 1 - slot)

    @pl.when(d == 0)
    def _():
        body(False)

    @pl.when(d == 1)
    def _():
        body(True)


def _mlstm(mq, mk, mv, mo, grow, gate_b, norm_g):
    nb, t, w = mq.shape
    nblk = t // TOKEN_TILE
    blk = lambda b, d, j: (b, _seq_block(d, j, nblk), 0)
    tile = pl.BlockSpec((None, TOKEN_TILE, w), blk)
    gb = gate_b.reshape(M_GATES, 1)
    ng = norm_g.reshape(1, w)
    sel, ones_bd, avg = _mlstm_tables()
    upper = np.triu(np.ones((TOKEN_TILE, TOKEN_TILE), np.float32))
    tri = jnp.asarray(np.stack([upper, upper.T]), BF16)
    const = lambda a: pl.BlockSpec(a.shape, lambda b, d, j: (0,) * a.ndim)
    n_pairs = MLSTM_HEADS // 2
    return pl.pallas_call(
        _mlstm_kernel,
        grid=(nb, 2, nblk),
        in_specs=[
            tile, tile, tile, tile,
            pl.BlockSpec((None, M_GATES, TOKEN_TILE), lambda b, d, j: (b, 0, _seq_block(d, j, nblk))),
            pl.BlockSpec((None, M_GATES, TOKEN_TILE),
                         lambda b, d, j: (b, 0, _seq_block(d, jnp.minimum(j + 1, nblk - 1), nblk))),
            const(gb), const(ng),
            pl.BlockSpec((None, TOKEN_TILE, TOKEN_TILE), lambda b, d, j: (d, 0, 0)),
            const(sel), const(ones_bd), const(avg),
        ],
        out_specs=pl.BlockSpec((None, TOKEN_TILE, w), lambda b, d, j: (b, jnp.where(d == 0, 0, _seq_block(d, j, nblk)), 0)),
        out_shape=jax.ShapeDtypeStruct((nb, t, w), F32),
        scratch_shapes=[
            pltpu.VMEM((t, w), F32),
            pltpu.VMEM((n_pairs, LANES, LANES), F32),
            pltpu.VMEM((n_pairs, LANES, LANES), F32),
            pltpu.VMEM((SUBLANES, LANES), F32),
            pltpu.VMEM((2, TOKEN_TILE, sel.shape[1]), F32),
            pltpu.VMEM((2, SUBLANES, TOKEN_TILE), F32),
            pltpu.VMEM((2, 2 * SUBLANES, LANES), F32),
        ],
        compiler_params=_params(("parallel", "arbitrary", "arbitrary")),
        name="mlstm",
    )(mq, mk, mv, mo, grow, grow, gb, ng, tri, sel, ones_bd, avg)


def _lru_kernel(x_ref, xp_ref, xn_ref, y_ref, cw_ref, cb_ref, gw_ref, gb_ref, lam_ref, out_ref,
                hf_ref, xe_ref, carry_ref, a_ref, u_ref, hb_ref, hl_ref, al_ref):
    d = pl.program_id(1)
    j = pl.program_id(2)
    nblk = pl.num_programs(2)
    L = TOKEN_TILE
    pos = _seq_block(d, j, nblk)
    row0 = pl.multiple_of(pos * L, L)

    @pl.when(j == 0)
    def _():
        carry_ref[...] = jnp.zeros_like(carry_ref)

    has_prev = pos >= 2
    has_next = (pos >= 1) & (pos <= nblk - 2)
    xe_ref[0:SUBLANES, :] = jnp.where(has_prev, xp_ref[...], 0.0)
    xe_ref[SUBLANES:SUBLANES + L, :] = x_ref[...]
    xe_ref[SUBLANES + L:, :] = jnp.where(has_next, xn_ref[...], 0.0)
    seq = cb_ref[...]
    for tap in range(CONV_WIDTH):
        off = SUBLANES - CONV_LEFT + tap
        seq = seq + cw_ref[tap:tap + 1, :] * xe_ref[off:off + L, :]
    sb = seq.astype(BF16)
    r = 0.5 + 0.5 * jnp.tanh(0.5 * (_dot(sb, gw_ref[0]) + gb_ref[0:1, :]))
    i = 0.5 + 0.5 * jnp.tanh(0.5 * (_dot(sb, gw_ref[1]) + gb_ref[1:2, :]))
    log_a = (-LRU_C * _softplus(-lam_ref[...])) * r
    a0 = jnp.exp(log_a)
    th = jnp.tanh(log_a)
    u0 = jnp.sqrt(-2.0 * th) * lax.rsqrt(1.0 - th) * (i * seq)
    n_lt = LRU_WIDTH // LANES
    for lt in range(n_lt):
        a_ref[lt] = a0[:, lt * LANES:(lt + 1) * LANES]
        u_ref[lt] = u0[:, lt * LANES:(lt + 1) * LANES]
    seg = L // (LRU_CHAINS * SUBLANES)
    sub = lax.broadcasted_iota(jnp.int32, (SUBLANES, LRU_WIDTH), 0)
    rows = lambda c, i: pl.ds(c * seg * SUBLANES + i, SUBLANES, stride=seg)
    strided = lambda ref, c, i: jnp.concatenate([ref[lt, rows(c, i), :] for lt in range(n_lt)], axis=1)

    def scan(rev):
        steps = range(seg - 1, -1, -1) if rev else range(seg)
        chains = range(LRU_CHAINS - 1, -1, -1) if rev else range(LRU_CHAINS)
        edge = SUBLANES - 1 if rev else 0
        shift = lambda x, n: pltpu.roll(x, SUBLANES - n if rev else n, 0)
        ends, prods = {}, {}
        for c in chains:
            h = jnp.zeros((SUBLANES, LRU_WIDTH), F32)
            ac = jnp.ones((SUBLANES, LRU_WIDTH), F32)
            for i in steps:
                ai = strided(a_ref, c, i)
                h = ai * h + strided(u_ref, c, i)
                ac = ai * ac
                hl_ref[c * seg + i] = h
                al_ref[c * seg + i] = ac
            step = 1
            while step < SUBLANES:
                ok = (sub < SUBLANES - step) if rev else (sub >= step)
                h = jnp.where(ok, ac * shift(h, step) + h, h)
                ac = jnp.where(ok, ac * shift(ac, step), ac)
                step *= 2
            ends[c], prods[c] = h, ac
        carry = carry_ref[...]
        for c in chains:
            true_ends = ends[c] + prods[c] * carry
            carry_in = jnp.where(sub == edge, carry, shift(true_ends, 1))
            carry = true_ends[SUBLANES - 1 - edge:SUBLANES - edge, :]
            for i in range(seg):
                hi = hl_ref[c * seg + i] + al_ref[c * seg + i] * carry_in
                for lt in range(n_lt):
                    hb_ref[lt, rows(c, i), :] = hi[:, lt * LANES:(lt + 1) * LANES]
        carry_ref[...] = carry
        return jnp.concatenate([hb_ref[lt] for lt in range(n_lt)], axis=1)

    @pl.when(d == 0)
    def _():
        hf_ref[pl.ds(row0, L), :] = scan(False)

    @pl.when(d == 1)
    def _():
        h = hf_ref[pl.ds(row0, L), :] + scan(True)
        y = y_ref[...]
        gelu = 0.5 * y * (1.0 + jnp.tanh(np.sqrt(2.0 / np.pi).astype(np.float32) * (y + 0.044715 * (y * y * y))))
        out_ref[...] = h * gelu


def _rglru(rx, ry, conv_w, conv_b, gate_w, gate_b, lam):
    nb, t, w = rx.shape
    nblk = t // TOKEN_TILE
    per_tile = TOKEN_TILE // SUBLANES
    n8 = t // SUBLANES
    blk = lambda b, d, j: (b, _seq_block(d, j, nblk), 0)
    tile = pl.BlockSpec((None, TOKEN_TILE, w), blk)
    halo = lambda f: pl.BlockSpec((None, SUBLANES, w), lambda b, d, j: (b, f(_seq_block(d, j, nblk)), 0))
    prev8 = lambda p: jnp.maximum(p * per_tile - 1, 0)
    next8 = lambda p: jnp.minimum((p + 1) * per_tile, n8 - 1)
    eye = jnp.eye(LRU_BLOCKS, dtype=gate_w.dtype)
    gw = jnp.einsum('dgnij,nm->dgnimj', gate_w, eye).reshape(2, 2, w, w).astype(BF16)
    cb = conv_b.reshape(1, w)
    return pl.pallas_call(
        _lru_kernel,
        grid=(nb, 2, nblk),
        in_specs=[
            tile, halo(prev8), halo(next8), tile,
            pl.BlockSpec(conv_w.shape, lambda b, d, j: (0, 0)),
            pl.BlockSpec(cb.shape, lambda b, d, j: (0, 0)),
            pl.BlockSpec((None, 2, w, w), lambda b, d, j: (d, 0, 0, 0)),
            pl.BlockSpec((None, 2, w), lambda b, d, j: (d, 0, 0)),
            pl.BlockSpec((None, 1, w), lambda b, d, j: (d, 0, 0)),
        ],
        out_specs=pl.BlockSpec((None, TOKEN_TILE, w), lambda b, d, j: (b, jnp.where(d == 0, 0, _seq_block(d, j, nblk)), 0)),
        out_shape=jax.ShapeDtypeStruct((nb, t, w), F32),
        scratch_shapes=[
            pltpu.VMEM((t, w), F32),
            pltpu.VMEM((TOKEN_TILE + 2 * SUBLANES, w), F32),
            pltpu.VMEM((1, w), F32),
            pltpu.VMEM((w // LANES, TOKEN_TILE, LANES), F32),
            pltpu.VMEM((w // LANES, TOKEN_TILE, LANES), F32),
            pltpu.VMEM((w // LANES, TOKEN_TILE, LANES), F32),
            pltpu.VMEM((TOKEN_TILE // SUBLANES, SUBLANES, w), F32),
            pltpu.VMEM((TOKEN_TILE // SUBLANES, SUBLANES, w), F32),
        ],
        compiler_params=_params(("parallel", "arbitrary", "arbitrary")),
        name="rglru",
    )(rx, rx, rx, ry, conv_w, cb, gw, gate_b, lam.reshape(2, 1, w))


def _outffn_kernel(n_streams, *refs):
    x = _stream_tile(refs[:n_streams])
    att_ref, mem_ref, rec_ref, mod_ref, g_ref, wo_ref, wi_ref, wd_ref, out_ref, act_ref = refs[n_streams:]
    mix = (_dot(att_ref[...].astype(BF16), wo_ref[0:ATT_Q, :])
           + _dot(mem_ref[...].astype(BF16), wo_ref[ATT_Q:ATT_Q + M_W, :])
           + _dot(rec_ref[...].astype(BF16), wo_ref[ATT_Q + M_W:, :]))
    x1 = x + mod_ref[2:3, :] * _rms(mix, g_ref[1:2, :])
    h = (_rms(x1, g_ref[2:3, :]) * (1.0 + mod_ref[4:5, :]) + mod_ref[3:4, :]).astype(BF16)
    for c0 in range(0, D_FF, FF_CHUNK):
        gate = _dot(h, wi_ref[:, c0:c0 + FF_CHUNK])
        up = _dot(h, wi_ref[:, D_FF + c0:D_FF + c0 + FF_CHUNK])
        half = 0.5 * gate
        act_ref[:, c0:c0 + FF_CHUNK] = ((half + half * jnp.tanh(half)) * up).astype(BF16)
    f = _dot(act_ref[...], wd_ref[...])
    out_ref[...] = x1 + mod_ref[5:6, :] * _rms(f, g_ref[3:4, :])


def _out_ffn(xs, att, mem, rec, mod, gain, w_out, w_ffn_in, w_down, first_tile=0):
    streams = xs if isinstance(xs, tuple) else (xs,)
    assert len(streams) == 1 or first_tile == 0
    nb, _, d = streams[0].shape
    t = sum(s.shape[1] for s in streams)
    nt = t // TOKEN_TILE
    tile = lambda n: pl.BlockSpec((None, TOKEN_TILE, n), lambda b, i: (b, i + first_tile, 0))
    const = lambda a: pl.BlockSpec(a.shape, lambda b, i: (0,) * a.ndim, pipeline_mode=pl.Buffered(1))
    return pl.pallas_call(
        functools.partial(_outffn_kernel, len(streams)),
        grid=(nb, nt - first_tile),
        in_specs=_stream_specs(xs, tile) + [
            tile(ATT_Q), tile(M_W), tile(LRU_WIDTH),
            pl.BlockSpec((None, N_MOD, d), lambda b, i: (jnp.where(i + first_tile == 0, nb, b), 0, 0)),
            const(gain), const(w_out), const(w_ffn_in), const(w_down),
        ],
        out_specs=pl.BlockSpec((None, TOKEN_TILE, d), lambda b, i: (b, i, 0)),
        out_shape=jax.ShapeDtypeStruct((nb, t - first_tile * TOKEN_TILE, d), F32),
        scratch_shapes=[pltpu.VMEM((TOKEN_TILE, D_FF), BF16)],
        compiler_params=_params(("parallel", "arbitrary")),
        name="out_ffn",
    )(*streams, att, mem, rec, mod, gain, w_out, w_ffn_in, w_down)


def _rope_tables(n_lat):
    t = jnp.arange(n_lat)
    row = (t // GRID_W).astype(F32)
    col = (t % GRID_W).astype(F32)
    freqs = ROPE_BASE ** (-jnp.arange(ROPE_PAIRS, dtype=F32) / ROPE_PAIRS)
    ang_r = row[:, None] * freqs
    ang_c = col[:, None] * freqs
    cs = jnp.concatenate([jnp.cos(ang_r), jnp.cos(ang_r), jnp.cos(ang_c), jnp.cos(ang_c)], axis=-1)
    sn = jnp.concatenate([-jnp.sin(ang_r), jnp.sin(ang_r), -jnp.sin(ang_c), jnp.sin(ang_c)], axis=-1)
    cs = jnp.concatenate([jnp.ones((CTX_LEN, HEAD_DIM), F32), cs], axis=0)
    sn = jnp.concatenate([jnp.zeros((CTX_LEN, HEAD_DIM), F32), sn], axis=0)
    return jnp.tile(cs, (1, LANES // HEAD_DIM)), jnp.tile(sn, (1, LANES // HEAD_DIM)), cs.T, sn.T


def kernel(x, c, ctx, c_ctx, w_ada, b_ada, norm_gain, w_in, w_out, attn_sink, mlstm_gate_b, mlstm_norm, conv_w, conv_b,
           lru_gate_w, lru_gate_b, lru_lam, w_ffn_in, w_ffn_out):
    nb, n_lat, d = x.shape
    depth = w_ada.shape[0]
    assert ctx.shape[1] == CTX_LEN and n_lat % TOKEN_TILE == 0 and nb < MOD_ROWS
    cvec = jnp.concatenate([c, c_ctx[None, :], jnp.zeros((MOD_ROWS - nb - 1, d), F32)], axis=0)
    mod = _modulation(cvec, w_ada, b_ada).reshape(depth, MOD_ROWS, N_MOD, d)
    rope = _rope_tables(n_lat)
    xs = (ctx, x) if depth > 1 else jnp.concatenate([ctx, x], axis=1)
    for l in range(depth):
        w_tok, w_feat = _split_in_weights(w_in[l])
        ak, mq, mk, mv, mo, rx, ry, aqt, avt, grow = _in_projection(xs, mod[l], norm_gain[l], w_tok, w_feat, rope)
        att = _attention(aqt, ak, avt, attn_sink[l])
        mem = _mlstm(mq, mk, mv, mo, grow, mlstm_gate_b[l], mlstm_norm[l])
        rec = _rglru(rx, ry, conv_w[l], conv_b[l], lru_gate_w[l], lru_gate_b[l], lru_lam[l])
        xs = _out_ffn(xs, att, mem, rec, mod[l], norm_gain[l], w_out[l].astype(BF16), w_ffn_in[l].astype(BF16),
                      w_ffn_out[l].astype(BF16), first_tile=int(l == depth - 1))
    return xs
```

```python
import functools

import jax
import jax.numpy as jnp
import numpy as np
from jax import lax
from jax.experimental import pallas as pl
from jax.experimental.pallas import tpu as pltpu

F32 = jnp.float32
BF16 = jnp.bfloat16

D_MODEL = 1024
GRID_W = 64
CTX_LEN = 256
N_MOD = 6
NORM_EPS = 1e-6
ATT_HEADS = 8
ATT_KV_HEADS = 2
ATT_GROUP = ATT_HEADS // ATT_KV_HEADS
HEAD_DIM = 64
WINDOW = 128
ATT_BLOCK = 128
ROPE_BASE = 10000.0
ROPE_PAIRS = HEAD_DIM // 4
ATT_Q = ATT_HEADS * HEAD_DIM
ATT_KV = ATT_KV_HEADS * HEAD_DIM
MLSTM_HEADS = 4
MLSTM_HEAD_DIM = 64
M_W = MLSTM_HEADS * MLSTM_HEAD_DIM
M_GATES = 4 * MLSTM_HEADS
LRU_WIDTH = 256
LRU_BLOCKS = 4
LRU_BW = LRU_WIDTH // LRU_BLOCKS
LRU_C = 8.0
CONV_WIDTH = 4
CONV_LEFT = CONV_WIDTH // 2
D_FF = -(-8 * D_MODEL // (3 * 256)) * 256
LOG2_E = float(np.log2(np.e))
Q_SCALE = HEAD_DIM ** -0.5 * LOG2_E

LANES = 128
SUBLANES = 8
TOKEN_TILE = CTX_LEN
FF_CHUNK = 256
LRU_CHAINS = 4
MOD_ROWS = 16
VMEM_LIMIT = 56 * 1024 * 1024


def _params(sem):
    return pltpu.CompilerParams(dimension_semantics=sem, vmem_limit_bytes=VMEM_LIMIT)


def _dot(a, b):
    return jnp.dot(a, b, preferred_element_type=F32)


def _dot_nt(a, b):
    return lax.dot_general(a, b, (((1,), (1,)), ((), ())), preferred_element_type=F32)


def _dot_exact(a, b):
    return jnp.dot(a, b, preferred_element_type=F32, precision=lax.Precision.HIGHEST)


def _sigmoid(x):
    return 1.0 / (1.0 + jnp.exp(-x))


def _softplus(x):
    return jnp.maximum(x, 0.0) + jnp.log1p(jnp.exp(-jnp.abs(x)))


def _rms(x, g):
    return x * lax.rsqrt(jnp.mean(x * x, axis=-1, keepdims=True) + NORM_EPS) * g


def _mod_kernel(c_ref, w_ref, b_ref, o_ref):
    c = c_ref[...]
    s = (c * _sigmoid(c)).astype(BF16)
    o_ref[...] = _dot(s, w_ref[...].astype(BF16)) + b_ref[...]


def _modulation(cvec, w_ada, b_ada):
    depth, d, n = w_ada.shape
    tn = 1536
    return pl.pallas_call(
        _mod_kernel,
        grid=(depth, n // tn),
        in_specs=[
            pl.BlockSpec((MOD_ROWS, d), lambda l, j: (0, 0)),
            pl.BlockSpec((None, d, tn), lambda l, j: (l, 0, j)),
            pl.BlockSpec((None, 1, tn), lambda l, j: (l, 0, j)),
        ],
        out_specs=pl.BlockSpec((None, MOD_ROWS, tn), lambda l, j: (l, 0, j)),
        out_shape=jax.ShapeDtypeStruct((depth, MOD_ROWS, n), F32),
        compiler_params=_params(("arbitrary", "arbitrary")),
        name="modulation",
    )(cvec, w_ada, b_ada.reshape(depth, 1, n))


_IN_COLS = {}
_col = 0
for _name, _width in (("aq", ATT_Q), ("ak", ATT_KV), ("av", ATT_KV), ("mq", M_W), ("mk", M_W), ("mv", M_W), ("mo", M_W),
                      ("mg", M_GATES), ("rx", LRU_WIDTH), ("ry", LRU_WIDTH)):
    _IN_COLS[_name] = (_col, _col + _width)
    _col += _width
_TOKEN_MAJOR = ("ak", "mq", "mk", "mv", "mo", "rx", "ry")
_FEATURE_MAJOR = ("aq", "av", "mg")


def _split_in_weights(w):
    tok = jnp.concatenate([w[:, _IN_COLS[n][0]:_IN_COLS[n][1]] for n in _TOKEN_MAJOR], axis=1)
    feat = jnp.concatenate([w[:, _IN_COLS[n][0]:_IN_COLS[n][1]] for n in _FEATURE_MAJOR], axis=1).T
    return tok.astype(BF16), feat.astype(BF16)


def _rope_slab(x, cs, sn, first):
    swapped = jnp.where(first, pltpu.roll(x, LANES - ROPE_PAIRS, 1), pltpu.roll(x, ROPE_PAIRS, 1))
    return x * cs + swapped * sn


def _stream_specs(xs, tile):
    if not isinstance(xs, tuple):
        return [tile(xs.shape[-1])]
    d = xs[0].shape[-1]
    return [pl.BlockSpec((None, TOKEN_TILE, d), lambda b, i: (b, 0, 0)),
            pl.BlockSpec((None, TOKEN_TILE, d), lambda b, i: (b, jnp.maximum(i - 1, 0), 0))]


def _stream_tile(refs):
    if len(refs) == 1:
        return refs[0][...]
    return jnp.where(pl.program_id(1) == 0, refs[0][...], refs[1][...])


def _inproj_kernel(n_streams, *refs):
    x = _stream_tile(refs[:n_streams])
    (mod_ref, g_ref, w_ref, wt_ref, cs_ref, sn_ref, cst_ref, snt_ref,
     ak_ref, mq_ref, mk_ref, mv_ref, mo_ref, rx_ref, ry_ref, aqt_ref, avt_ref, gr_ref) = refs[n_streams:]
    h = _rms(x, g_ref[0:1, :]) * (1.0 + mod_ref[1:2, :]) + mod_ref[0:1, :]
    hb = h.astype(BF16)
    pt = _dot_nt(wt_ref[...], hb)
    cst = cst_ref[...]
    snt = snt_ref[...]
    rp = ROPE_PAIRS
    for hd in range(ATT_HEADS):
        xs = pt[hd * HEAD_DIM:(hd + 1) * HEAD_DIM, :]
        swapped = jnp.concatenate([xs[rp:2 * rp], xs[0:rp], xs[3 * rp:4 * rp], xs[2 * rp:3 * rp]], axis=0)
        aqt_ref[hd * HEAD_DIM:(hd + 1) * HEAD_DIM, :] = (xs * cst + swapped * snt) * Q_SCALE
    avt_ref[...] = pt[ATT_Q:ATT_Q + ATT_KV, :]
    gr_ref[...] = pt[ATT_Q + ATT_KV:, :]
    lane = lax.broadcasted_iota(jnp.int32, (x.shape[0], LANES), 1)
    first = (lane & (2 * ROPE_PAIRS - 1)) < ROPE_PAIRS
    ak_ref[...] = _rope_slab(_dot(hb, w_ref[:, 0:ATT_KV]), cs_ref[...], sn_ref[...], first)
    col = ATT_KV
    for ref in (mq_ref, mk_ref, mv_ref, mo_ref, rx_ref, ry_ref):
        n = ref.shape[-1]
        ref[...] = _dot(hb, w_ref[:, col:col + n])
        col += n


def _in_projection(xs, mod, gain, w_tok, w_feat, rope):
    streams = xs if isinstance(xs, tuple) else (xs,)
    nb = streams[0].shape[0]
    t = sum(s.shape[1] for s in streams)
    nt = t // TOKEN_TILE
    tile = lambda n: pl.BlockSpec((None, TOKEN_TILE, n), lambda b, i: (b, i, 0))
    tile_t = lambda n: pl.BlockSpec((None, n, TOKEN_TILE), lambda b, i: (b, 0, i))
    const = lambda a: pl.BlockSpec(a.shape, lambda b, i: (0,) * a.ndim)
    width = lambda n: _IN_COLS[n][1] - _IN_COLS[n][0]
    out_shapes = [jax.ShapeDtypeStruct((nb, t, width(n)), F32) for n in _TOKEN_MAJOR]
    out_shapes += [jax.ShapeDtypeStruct((nb, width(n), t), F32) for n in _FEATURE_MAJOR]
    out_specs = [tile(width(n)) for n in _TOKEN_MAJOR] + [tile_t(width(n)) for n in _FEATURE_MAJOR]
    rope_cs, rope_sn, rope_cst, rope_snt = rope
    return pl.pallas_call(
        functools.partial(_inproj_kernel, len(streams)),
        grid=(nb, nt),
        in_specs=_stream_specs(xs, tile) + [
            pl.BlockSpec((None, N_MOD, mod.shape[-1]), lambda b, i: (jnp.where(i == 0, nb, b), 0, 0)),
            const(gain), const(w_tok), const(w_feat),
            pl.BlockSpec((TOKEN_TILE, LANES), lambda b, i: (i, 0)),
            pl.BlockSpec((TOKEN_TILE, LANES), lambda b, i: (i, 0)),
            pl.BlockSpec((HEAD_DIM, TOKEN_TILE), lambda b, i: (0, i)),
            pl.BlockSpec((HEAD_DIM, TOKEN_TILE), lambda b, i: (0, i)),
        ],
        out_specs=out_specs,
        out_shape=out_shapes,
        compiler_params=_params(("parallel", "arbitrary")),
        name="in_projection",
    )(*streams, mod, gain, w_tok, w_feat, rope_cs, rope_sn, rope_cst, rope_snt)


def _attn_block(j, nblk, sink_ref, qt_ref, kp_ref, kc_ref, kn_ref, kx_ref, vp_ref, vc_ref, vn_ref, vx_ref, o_ref):
    ctx_blocks = CTX_LEN // ATT_BLOCK
    nq = ATT_BLOCK
    cols = ATT_GROUP * nq
    c = lax.broadcasted_iota(jnp.int32, (ATT_BLOCK, cols), 0)
    r = lax.broadcasted_iota(jnp.int32, (ATT_BLOCK, cols), 1) & (nq - 1)
    m_prev = (c >= r) & (j >= ctx_blocks + 1)
    m_next = (c <= r) & (j >= ctx_blocks) & (j <= nblk - 2)
    is_lat = j >= ctx_blocks
    col = lax.broadcasted_iota(jnp.int32, (1, cols), 1)
    neg = -jnp.inf
    kp = kp_ref[...].astype(BF16)
    kc = kc_ref[...].astype(BF16)
    kn = kn_ref[...].astype(BF16)
    kx = kx_ref[...].astype(BF16)
    zeros = jnp.zeros((HEAD_DIM, cols), BF16)
    scores = []
    for kh in range(ATT_KV_HEADS):
        hd0 = kh * ATT_GROUP
        qg = jnp.concatenate([qt_ref[(hd0 + g) * HEAD_DIM:(hd0 + g + 1) * HEAD_DIM, :] for g in range(ATT_GROUP)],
                             axis=1).astype(BF16)
        rhs = jnp.concatenate([qg, zeros] if kh == 0 else [zeros, qg], axis=0)
        s_x = _dot(kx, rhs)
        scores.append((jnp.where(m_prev, _dot(kp, rhs), neg), jnp.where(is_lat, _dot(kc, rhs), neg),
                       jnp.where(m_next, _dot(kn, rhs), neg), s_x[:ATT_BLOCK], s_x[ATT_BLOCK:]))
    for kh in range(ATT_KV_HEADS):
        hd0 = kh * ATT_GROUP
        sink = jnp.full((1, cols), sink_ref[hd0], F32)
        for g in range(1, ATT_GROUP):
            sink = jnp.where(col >= g * nq, sink_ref[hd0 + g], sink)
        sink = sink * LOG2_E
        s_p, s_c, s_n, s_x0, s_x1 = scores[kh]
        m_el = jnp.maximum(jnp.maximum(jnp.maximum(s_p, s_c), jnp.maximum(s_n, s_x0)), s_x1)
        m = jnp.maximum(jnp.max(m_el, axis=0, keepdims=True), sink)
        p_p = jnp.exp2(s_p - m)
        p_c = jnp.exp2(s_c - m)
        p_n = jnp.exp2(s_n - m)
        p_x0 = jnp.exp2(s_x0 - m)
        p_x1 = jnp.exp2(s_x1 - m)
        den = jnp.sum((p_p + p_c) + (p_n + p_x0) + p_x1, axis=0, keepdims=True) + jnp.exp2(sink - m)
        vs = slice(kh * HEAD_DIM, (kh + 1) * HEAD_DIM)
        p_x = jnp.concatenate([p_x0, p_x1], axis=0).astype(BF16)
        o = (_dot(vp_ref[vs, :].astype(BF16), p_p.astype(BF16)) + _dot(vc_ref[vs, :].astype(BF16), p_c.astype(BF16))
             + _dot(vn_ref[vs, :].astype(BF16), p_n.astype(BF16)) + _dot(vx_ref[vs, :].astype(BF16), p_x))
        o = o / den
        for pair in range(ATT_GROUP // 2):
            two = jnp.concatenate([o[:, (2 * pair) * nq:(2 * pair + 1) * nq], o[:, (2 * pair + 1) * nq:(2 * pair + 2) * nq]],
                                  axis=0)
            lo = (hd0 + 2 * pair) * HEAD_DIM
            o_ref[:, lo:lo + 2 * HEAD_DIM] = two.T


def _attention_specs(aqt, ak, avt, sink, qblock):
    nb, t, _ = ak.shape
    nblk = t // ATT_BLOCK
    prev = lambda q: jnp.maximum(q - 1, 0)
    cur = lambda q: q
    nxt = lambda q: jnp.minimum(q + 1, nblk - 1)
    kb = lambda f: pl.BlockSpec((None, ATT_BLOCK, ATT_KV), lambda b, *g: (b, f(qblock(*g)), 0))
    vb = lambda f: pl.BlockSpec((None, ATT_KV, ATT_BLOCK), lambda b, *g: (b, 0, f(qblock(*g))))
    in_specs = [
        pl.BlockSpec(memory_space=pltpu.SMEM),
        pl.BlockSpec((None, ATT_Q, ATT_BLOCK), lambda b, *g: (b, 0, qblock(*g))),
        kb(prev), kb(cur), kb(nxt), pl.BlockSpec((None, CTX_LEN, ATT_KV), lambda b, *g: (b, 0, 0)),
        vb(prev), vb(cur), vb(nxt), pl.BlockSpec((None, ATT_KV, CTX_LEN), lambda b, *g: (b, 0, 0)),
    ]
    args = [sink, aqt, ak, ak, ak, ak, avt, avt, avt, avt]
    out_spec = pl.BlockSpec((None, ATT_BLOCK, ATT_Q), lambda b, *g: (b, qblock(*g), 0))
    return in_specs, args, out_spec, jax.ShapeDtypeStruct((nb, t, ATT_Q), F32)


def _seq_block(d, j, nblk):
    return jnp.where((d == 0) | (j == 0), j, nblk - j)


def _lane_scan(x, op, fill, rev):
    n = x.shape[-1]
    lane = lax.broadcasted_iota(jnp.int32, x.shape, 1)
    step = 1
    while step < n:
        if rev:
            shifted = jnp.where(lane < n - step, pltpu.roll(x, n - step, 1), fill)
        else:
            shifted = jnp.where(lane >= step, pltpu.roll(x, step, 1), fill)
        x = op(x, shifted)
        step *= 2
    return x


def _split3(x):
    hi = x.astype(BF16).astype(F32)
    mid = (x - hi).astype(BF16).astype(F32)
    lo = x - hi - mid
    return hi, mid, lo


def _mlstm_tables():
    H, dh = MLSTM_HEADS, MLSTM_HEAD_DIM
    n_pairs = H // 2
    n_slabs = H + 3 * n_pairs
    sel = np.zeros((LANES, n_slabs * LANES), np.float32)
    for part in range(3):
        base = part * 4 * H
        for h in range(H):
            sel[base + h, h * LANES:(h + 1) * LANES] = 1.0
        for qty in range(3):
            for p in range(n_pairs):
                slab = H + qty * n_pairs + p
                for half in range(2):
                    row = base + H * qty + 2 * p + half
                    sel[row, slab * LANES + half * dh:slab * LANES + (half + 1) * dh] = 1.0
    ones_bd = np.zeros((2 * TOKEN_TILE, LANES), np.float32)
    ones_bd[:TOKEN_TILE, :dh] = 1.0
    ones_bd[TOKEN_TILE:, dh:] = 1.0
    avg = np.zeros((3 * LANES, LANES), np.float32)
    for part in range(3):
        for half in range(2):
            avg[part * LANES + half * dh:part * LANES + (half + 1) * dh, half * dh:(half + 1) * dh] = 1.0 / dh
    return jnp.asarray(sel, BF16), jnp.asarray(ones_bd, BF16), jnp.asarray(avg, BF16)


def _mlstm_block_terms(rev, g, tri_ref):
    L = TOKEN_TILE
    H = MLSTM_HEADS
    ioff = 2 * H if rev else 0
    li = g[ioff:ioff + H, :]
    lf = -_softplus(-g[ioff + H:ioff + 2 * H, :])
    parts = _dot(jnp.concatenate(_split3(lf), axis=0).astype(BF16), tri_ref[...])
    b = parts[0:H] + parts[H:2 * H] + parts[2 * H:3 * H]
    r = li - b
    cm = _lane_scan(r, jnp.maximum, -jnp.inf, rev)
    last = 0 if rev else L - 1
    g_tot = b[:, last:last + 1]
    a = g_tot - b + li
    m_loc = jnp.max(a, axis=1, keepdims=True)
    w = jnp.exp(a - m_loc)
    return r, cm, b, w, g_tot, m_loc


def _mlstm_stash_terms(terms, sel_ref, cols_ref, r_ref, gm_ref, slot):
    L = TOKEN_TILE
    H, dh = MLSTM_HEADS, MLSTM_HEAD_DIM
    r, cm, b, w, g_tot, m_loc = terms
    stacked = jnp.concatenate([cm, b, w, jnp.zeros((H, L), F32)], axis=0)
    hi, mid, lo = _split3(stacked)
    pad = jnp.zeros((LANES - 3 * 4 * H, L), F32)
    cols_ref[slot] = _dot(jnp.concatenate([hi, mid, lo, pad], axis=0).T.astype(BF16), sel_ref[...])
    r_ref[slot] = jnp.concatenate([r, jnp.zeros((SUBLANES - H, L), F32)], axis=0)
    lane_row = lax.broadcasted_iota(jnp.int32, (1, LANES), 1)

    def layout(col):
        rows = [jnp.broadcast_to(col, (H, LANES))]
        rows += [jnp.where(lane_row < dh, col[2 * p:2 * p + 1, :], col[2 * p + 1:2 * p + 2, :]) for p in range(H // 2)]
        rows += [jnp.zeros((SUBLANES - H - H // 2, LANES), F32)]
        return jnp.concatenate(rows, axis=0)

    gm_ref[slot] = jnp.concatenate([layout(g_tot), layout(m_loc)], axis=0)


def _mlstm_kernel(q_ref, k_ref, v_ref, o_ref, gr_ref, gnext_ref, gb_ref, ng_ref, tri_ref, sel_ref, ones_ref, avg_ref, out_ref,
                  hf_ref, c_ref, n_ref, m_ref, cols_ref, r_ref, gm_ref, rev, init=False):
    d = pl.program_id(1)
    j = pl.program_id(2)
    nblk = pl.num_programs(2)
    L = TOKEN_TILE
    H, dh = MLSTM_HEADS, MLSTM_HEAD_DIM
    n_pairs = H // 2
    row0 = pl.multiple_of(_seq_block(d, j, nblk) * L, L)
    slot = j & 1

    if init:
        c_ref[...] = jnp.zeros_like(c_ref)
        n_ref[...] = jnp.zeros_like(n_ref)
        m_ref[...] = jnp.zeros_like(m_ref)
        _mlstm_stash_terms(_mlstm_block_terms(rev, gr_ref[...] + gb_ref[...], tri_ref), sel_ref, cols_ref, r_ref, gm_ref, 0)
        return

    def body(rev):
        next_terms = _mlstm_block_terms(rev, gnext_ref[...] + gb_ref[...], tri_ref)

        r = r_ref[slot]
        gm = gm_ref[slot]
        slab = lambda i: cols_ref[slot, :, i * LANES:(i + 1) * LANES]
        m0 = m_ref[...]
        g_tot, m_loc = gm[0:SUBLANES], gm[SUBLANES:]
        m_new = jnp.maximum(g_tot + m0, m_loc)
        f_prev = jnp.exp(g_tot + m0 - m_new)
        f_loc = jnp.exp(m_loc - m_new)
        m_ref[...] = m_new
        lane = lax.broadcasted_iota(jnp.int32, (L, LANES), 1)
        first_head = lane < dh
        t_idx = lax.broadcasted_iota(jnp.int32, (L, LANES), 0)
        kt = (k_ref[...] * (dh ** -0.5)).T.astype(BF16)
        zeros_half = jnp.zeros((dh, L), BF16)
        bd_mask = ((lax.broadcasted_iota(jnp.int32, (LANES, LANES), 0) < dh)
                   == (lax.broadcasted_iota(jnp.int32, (LANES, LANES), 1) < dh))
        for p in range(n_pairs):
            ps = slice(p * LANES, (p + 1) * LANES)
            qb = q_ref[:, ps].astype(BF16)
            v = v_ref[:, ps]
            kt_pair = kt[ps, :]
            sm = []
            for half in range(2):
                h = 2 * p + half
                kt_h = kt[h * dh:(h + 1) * dh, :]
                kt_masked = jnp.concatenate([kt_h, zeros_half] if half == 0 else [zeros_half, kt_h], axis=0)
                s = _dot(qb, kt_masked)
                m_b = jnp.maximum(slab(h), m0[h:h + 1, :])
                for lt in range(L // LANES):
                    s_idx = lane + lt * LANES
                    vis = (s_idx >= t_idx) if rev else (s_idx <= t_idx)
                    e = jnp.exp(jnp.where(vis, r[h:h + 1, lt * LANES:(lt + 1) * LANES] - m_b, -jnp.inf))
                    sm.append((s[:, lt * LANES:(lt + 1) * LANES] * e).astype(BF16))
            sm = jnp.concatenate(sm, axis=1)
            v_bd = jnp.concatenate([jnp.where(first_head, v, 0.0), jnp.where(first_head, 0.0, v)], axis=0).astype(BF16)
            m0_pair = m0[H + p:H + p + 1, :]
            m_pair = jnp.maximum(slab(H + p), m0_pair)
            f_pair = jnp.exp(m0_pair - m_pair)
            num = _dot(sm, v_bd) + f_pair * _dot(qb, c_ref[p].astype(BF16))
            den = _dot(sm, ones_ref[...]) + f_pair * _dot(qb, n_ref[p].astype(BF16))
            hh = num / jnp.maximum(jnp.abs(den), jnp.exp(-(slab(H + n_pairs + p) + m_pair)))
            w_pair = slab(H + 2 * n_pairs + p)
            c_loc = jnp.where(bd_mask, _dot(kt_pair, (w_pair * v).astype(BF16)), 0.0)
            n_loc = jnp.where(bd_mask, _dot(kt_pair, w_pair.astype(BF16)), 0.0)
            c_ref[p] = f_prev[H + p:H + p + 1, :] * c_ref[p] + f_loc[H + p:H + p + 1, :] * c_loc
            n_ref[p] = f_prev[H + p:H + p + 1, :] * n_ref[p] + f_loc[H + p:H + p + 1, :] * n_loc
            if not rev:
                hf_ref[pl.ds(row0, L), ps] = hh
            else:
                ht = hf_ref[pl.ds(row0, L), ps] + hh
                mu = _dot(jnp.concatenate(_split3(ht), axis=1).astype(BF16), avg_ref[...])
                cen = ht - mu
                var = _dot(jnp.concatenate(_split3(cen * cen), axis=1).astype(BF16), avg_ref[...])
                y = cen * lax.rsqrt(var + NORM_EPS) * ng_ref[:, ps]
                out_ref[:, ps] = _sigmoid(o_ref[:, ps]) * y
        _mlstm_stash_terms(next_terms, sel_ref, cols_ref, r_ref, gm_ref, 1 - slot)

    body(rev)


def _mlstm_specs(mq, mk, mv, mo, grow, gate_b, norm_g):
    nb, t, w = mq.shape
    nblk = t // TOKEN_TILE
    blk = lambda b, d, j: (b, _seq_block(d, j, nblk), 0)
    tile = pl.BlockSpec((None, TOKEN_TILE, w), blk)
    gb = gate_b.reshape(M_GATES, 1)
    ng = norm_g.reshape(1, w)
    sel, ones_bd, avg = _mlstm_tables()
    upper = np.triu(np.ones((TOKEN_TILE, TOKEN_TILE), np.float32))
    tri = jnp.asarray(np.stack([upper, upper.T]), BF16)
    const = lambda a: pl.BlockSpec(a.shape, lambda b, d, j: (0,) * a.ndim)
    n_pairs = MLSTM_HEADS // 2
    in_specs = [
        tile, tile, tile, tile,
        pl.BlockSpec((None, M_GATES, TOKEN_TILE), lambda b, d, j: (b, 0, _seq_block(d, j, nblk))),
        pl.BlockSpec((None, M_GATES, TOKEN_TILE),
                     lambda b, d, j: (b, 0, _seq_block(d, jnp.minimum(j + 1, nblk - 1), nblk))),
        const(gb), const(ng),
        pl.BlockSpec((None, TOKEN_TILE, TOKEN_TILE), lambda b, d, j: (d, 0, 0)),
        const(sel), const(ones_bd), const(avg),
    ]
    args = [mq, mk, mv, mo, grow, grow, gb, ng, tri, sel, ones_bd, avg]
    out_spec = pl.BlockSpec((None, TOKEN_TILE, w), lambda b, d, j: (b, jnp.where(d == 0, 0, _seq_block(d, j, nblk)), 0))
    scratch = [
        pltpu.VMEM((t, w), F32),
        pltpu.VMEM((n_pairs, LANES, LANES), F32),
        pltpu.VMEM((n_pairs, LANES, LANES), F32),
        pltpu.VMEM((SUBLANES, LANES), F32),
        pltpu.VMEM((2, TOKEN_TILE, sel.shape[1]), F32),
        pltpu.VMEM((2, SUBLANES, TOKEN_TILE), F32),
        pltpu.VMEM((2, 2 * SUBLANES, LANES), F32),
    ]
    return in_specs, args, out_spec, jax.ShapeDtypeStruct((nb, t, w), F32), scratch


def _lru_kernel(x_ref, xp_ref, xn_ref, y_ref, cw_ref, cb_ref, gw_ref, gb_ref, lam_ref, out_ref,
                hf_ref, xe_ref, carry_ref, a_ref, u_ref, hb_ref, hl_ref, al_ref, rev, init=False):
    d = pl.program_id(1)
    j = pl.program_id(2)
    nblk = pl.num_programs(2)
    L = TOKEN_TILE
    pos = _seq_block(d, j, nblk)
    row0 = pl.multiple_of(pos * L, L)

    if init:
        carry_ref[...] = jnp.zeros_like(carry_ref)
        return

    has_prev = pos >= 2
    has_next = (pos >= 1) & (pos <= nblk - 2)
    xe_ref[0:SUBLANES, :] = jnp.where(has_prev, xp_ref[...], 0.0)
    xe_ref[SUBLANES:SUBLANES + L, :] = x_ref[...]
    xe_ref[SUBLANES + L:, :] = jnp.where(has_next, xn_ref[...], 0.0)
    seq = cb_ref[...]
    for tap in range(CONV_WIDTH):
        off = SUBLANES - CONV_LEFT + tap
        seq = seq + cw_ref[tap:tap + 1, :] * xe_ref[off:off + L, :]
    sb = seq.astype(BF16)
    r = 0.5 + 0.5 * jnp.tanh(0.5 * (_dot(sb, gw_ref[0]) + gb_ref[0:1, :]))
    i = 0.5 + 0.5 * jnp.tanh(0.5 * (_dot(sb, gw_ref[1]) + gb_ref[1:2, :]))
    log_a = (-LRU_C * _softplus(-lam_ref[...])) * r
    a0 = jnp.exp(log_a)
    th = jnp.tanh(log_a)
    u0 = jnp.sqrt(-2.0 * th) * lax.rsqrt(1.0 - th) * (i * seq)
    n_lt = LRU_WIDTH // LANES
    for lt in range(n_lt):
        a_ref[lt] = a0[:, lt * LANES:(lt + 1) * LANES]
        u_ref[lt] = u0[:, lt * LANES:(lt + 1) * LANES]
    seg = L // (LRU_CHAINS * SUBLANES)
    sub = lax.broadcasted_iota(jnp.int32, (SUBLANES, LRU_WIDTH), 0)
    rows = lambda c, i: pl.ds(c * seg * SUBLANES + i, SUBLANES, stride=seg)
    strided = lambda ref, c, i: jnp.concatenate([ref[lt, rows(c, i), :] for lt in range(n_lt)], axis=1)

    def scan(rev):
        steps = range(seg - 1, -1, -1) if rev else range(seg)
        chains = range(LRU_CHAINS - 1, -1, -1) if rev else range(LRU_CHAINS)
        edge = SUBLANES - 1 if rev else 0
        shift = lambda x, n: pltpu.roll(x, SUBLANES - n if rev else n, 0)
        ends, prods = {}, {}
        for c in chains:
            h = jnp.zeros((SUBLANES, LRU_WIDTH), F32)
            ac = jnp.ones((SUBLANES, LRU_WIDTH), F32)
            for i in steps:
                ai = strided(a_ref, c, i)
                h = ai * h + strided(u_ref, c, i)
                ac = ai * ac
                hl_ref[c * seg + i] = h
                al_ref[c * seg + i] = ac
            step = 1
            while step < SUBLANES:
                ok = (sub < SUBLANES - step) if rev else (sub >= step)
                h = jnp.where(ok, ac * shift(h, step) + h, h)
                ac = jnp.where(ok, ac * shift(ac, step), ac)
                step *= 2
            ends[c], prods[c] = h, ac
        carry = carry_ref[...]
        for c in chains:
            true_ends = ends[c] + prods[c] * carry
            carry_in = jnp.where(sub == edge, carry, shift(true_ends, 1))
            carry = true_ends[SUBLANES - 1 - edge:SUBLANES - edge, :]
            for i in range(seg):
                hi = hl_ref[c * seg + i] + al_ref[c * seg + i] * carry_in
                for lt in range(n_lt):
                    hb_ref[lt, rows(c, i), :] = hi[:, lt * LANES:(lt + 1) * LANES]
        carry_ref[...] = carry
        return jnp.concatenate([hb_ref[lt] for lt in range(n_lt)], axis=1)

    def forward():
        hf_ref[pl.ds(row0, L), :] = scan(False)

    def backward():
        h = hf_ref[pl.ds(row0, L), :] + scan(True)
        y = y_ref[...]
        gelu = 0.5 * y * (1.0 + jnp.tanh(np.sqrt(2.0 / np.pi).astype(np.float32) * (y + 0.044715 * (y * y * y))))
        out_ref[...] = h * gelu

    backward() if rev else forward()


def _rglru_specs(rx, ry, conv_w, conv_b, gate_w, gate_b, lam):
    nb, t, w = rx.shape
    nblk = t // TOKEN_TILE
    per_tile = TOKEN_TILE // SUBLANES
    n8 = t // SUBLANES
    blk = lambda b, d, j: (b, _seq_block(d, j, nblk), 0)
    tile = pl.BlockSpec((None, TOKEN_TILE, w), blk)
    halo = lambda f: pl.BlockSpec((None, SUBLANES, w), lambda b, d, j: (b, f(_seq_block(d, j, nblk)), 0))
    prev8 = lambda p: jnp.maximum(p * per_tile - 1, 0)
    next8 = lambda p: jnp.minimum((p + 1) * per_tile, n8 - 1)
    eye = jnp.eye(LRU_BLOCKS, dtype=gate_w.dtype)
    gw = jnp.einsum('dgnij,nm->dgnimj', gate_w, eye).reshape(2, 2, w, w).astype(BF16)
    cb = conv_b.reshape(1, w)
    in_specs = [
        tile, halo(prev8), halo(next8), tile,
        pl.BlockSpec(conv_w.shape, lambda b, d, j: (0, 0)),
        pl.BlockSpec(cb.shape, lambda b, d, j: (0, 0)),
        pl.BlockSpec((None, 2, w, w), lambda b, d, j: (d, 0, 0, 0)),
        pl.BlockSpec((None, 2, w), lambda b, d, j: (d, 0, 0)),
        pl.BlockSpec((None, 1, w), lambda b, d, j: (d, 0, 0)),
    ]
    args = [rx, rx, rx, ry, conv_w, cb, gw, gate_b, lam.reshape(2, 1, w)]
    out_spec = pl.BlockSpec((None, TOKEN_TILE, w), lambda b, d, j: (b, jnp.where(d == 0, 0, _seq_block(d, j, nblk)), 0))
    scratch = [
        pltpu.VMEM((t, w), F32),
        pltpu.VMEM((TOKEN_TILE + 2 * SUBLANES, w), F32),
        pltpu.VMEM((1, w), F32),
        pltpu.VMEM((w // LANES, TOKEN_TILE, LANES), F32),
        pltpu.VMEM((w // LANES, TOKEN_TILE, LANES), F32),
        pltpu.VMEM((w // LANES, TOKEN_TILE, LANES), F32),
        pltpu.VMEM((TOKEN_TILE // SUBLANES, SUBLANES, w), F32),
        pltpu.VMEM((TOKEN_TILE // SUBLANES, SUBLANES, w), F32),
    ]
    return in_specs, args, out_spec, jax.ShapeDtypeStruct((nb, t, w), F32), scratch


def _mixers_kernel(n_att, n_mem, n_rec, n_mem_scratch, *refs):
    d = pl.program_id(1)
    j = pl.program_id(2)
    nblk = pl.num_programs(2)
    n_in = n_att + n_mem + n_rec
    att_in, mem_in, rec_in = refs[:n_att], refs[n_att:n_att + n_mem], refs[n_att + n_mem:n_in]
    att_out, mem_out, rec_out = refs[n_in:n_in + 3]
    mem_scratch = refs[n_in + 3:n_in + 3 + n_mem_scratch]
    rec_scratch = refs[n_in + 3 + n_mem_scratch:]
    q_per_tile = TOKEN_TILE // ATT_BLOCK

    def body(rev):
        @pl.when(j == 0)
        def _():
            _mlstm_kernel(*mem_in, mem_out, *mem_scratch, rev=rev, init=True)
            _lru_kernel(*rec_in, rec_out, *rec_scratch, rev=rev, init=True)

        _attn_block(q_per_tile * _seq_block(d, j, nblk) + int(rev), q_per_tile * nblk, *att_in, att_out)
        _mlstm_kernel(*mem_in, mem_out, *mem_scratch, rev=rev)
        _lru_kernel(*rec_in, rec_out, *rec_scratch, rev=rev)

    pl.when(d == 0)(functools.partial(body, False))
    pl.when(d == 1)(functools.partial(body, True))


def _mixers(aqt, ak, avt, sink, mlstm_args, rglru_args):
    nb, t, _ = ak.shape
    nblk = t // TOKEN_TILE
    assert TOKEN_TILE == 2 * ATT_BLOCK
    qblock = lambda d, j: (TOKEN_TILE // ATT_BLOCK) * _seq_block(d, j, nblk) + d
    att_specs, att_args, att_out, att_shape = _attention_specs(aqt, ak, avt, sink, qblock)
    mem_specs, mem_args, mem_out, mem_shape, mem_scratch = _mlstm_specs(*mlstm_args)
    rec_specs, rec_args, rec_out, rec_shape, rec_scratch = _rglru_specs(*rglru_args)
    return pl.pallas_call(
        functools.partial(_mixers_kernel, len(att_args), len(mem_args), len(rec_args), len(mem_scratch)),
        grid=(nb, 2, nblk),
        in_specs=att_specs + mem_specs + rec_specs,
        out_specs=[att_out, mem_out, rec_out],
        out_shape=[att_shape, mem_shape, rec_shape],
        scratch_shapes=mem_scratch + rec_scratch,
        compiler_params=_params(("parallel", "arbitrary", "arbitrary")),
        name="mixers",
    )(*att_args, *mem_args, *rec_args)


def _outffn_kernel(n_streams, *refs):
    x = _stream_tile(refs[:n_streams])
    att_ref, mem_ref, rec_ref, mod_ref, g_ref, wo_ref, wi_ref, wd_ref, out_ref, act_ref = refs[n_streams:]
    mix = (_dot(att_ref[...].astype(BF16), wo_ref[0:ATT_Q, :])
           + _dot(mem_ref[...].astype(BF16), wo_ref[ATT_Q:ATT_Q + M_W, :])
           + _dot(rec_ref[...].astype(BF16), wo_ref[ATT_Q + M_W:, :]))
    x1 = x + mod_ref[2:3, :] * _rms(mix, g_ref[1:2, :])
    h = (_rms(x1, g_ref[2:3, :]) * (1.0 + mod_ref[4:5, :]) + mod_ref[3:4, :]).astype(BF16)
    for c0 in range(0, D_FF, FF_CHUNK):
        gate = _dot(h, wi_ref[:, c0:c0 + FF_CHUNK])
        up = _dot(h, wi_ref[:, D_FF + c0:D_FF + c0 + FF_CHUNK])
        half = 0.5 * gate
        act_ref[:, c0:c0 + FF_CHUNK] = ((half + half * jnp.tanh(half)) * up).astype(BF16)
    f = _dot(act_ref[...], wd_ref[...])
    out_ref[...] = x1 + mod_ref[5:6, :] * _rms(f, g_ref[3:4, :])


def _out_ffn(xs, att, mem, rec, mod, gain, w_out, w_ffn_in, w_down, first_tile=0):
    streams = xs if isinstance(xs, tuple) else (xs,)
    assert len(streams) == 1 or first_tile == 0
    nb, _, d = streams[0].shape
    t = sum(s.shape[1] for s in streams)
    nt = t // TOKEN_TILE
    tile = lambda n: pl.BlockSpec((None, TOKEN_TILE, n), lambda b, i: (b, i + first_tile, 0))
    const = lambda a: pl.BlockSpec(a.shape, lambda b, i: (0,) * a.ndim, pipeline_mode=pl.Buffered(1))
    return pl.pallas_call(
        functools.partial(_outffn_kernel, len(streams)),
        grid=(nb, nt - first_tile),
        in_specs=_stream_specs(xs, tile) + [
            tile(ATT_Q), tile(M_W), tile(LRU_WIDTH),
            pl.BlockSpec((None, N_MOD, d), lambda b, i: (jnp.where(i + first_tile == 0, nb, b), 0, 0)),
            const(gain), const(w_out), const(w_ffn_in), const(w_down),
        ],
        out_specs=pl.BlockSpec((None, TOKEN_TILE, d), lambda b, i: (b, i, 0)),
        out_shape=jax.ShapeDtypeStruct((nb, t - first_tile * TOKEN_TILE, d), F32),
        scratch_shapes=[pltpu.VMEM((TOKEN_TILE, D_FF), BF16)],
        compiler_params=_params(("parallel", "arbitrary")),
        name="out_ffn",
    )(*streams, att, mem, rec, mod, gain, w_out, w_ffn_in, w_down)


def _rope_tables(n_lat):
    t = jnp.arange(n_lat)
    row = (t // GRID_W).astype(F32)
    col = (t % GRID_W).astype(F32)
    freqs = ROPE_BASE ** (-jnp.arange(ROPE_PAIRS, dtype=F32) / ROPE_PAIRS)
    ang_r = row[:, None] * freqs
    ang_c = col[:, None] * freqs
    cs = jnp.concatenate([jnp.cos(ang_r), jnp.cos(ang_r), jnp.cos(ang_c), jnp.cos(ang_c)], axis=-1)
    sn = jnp.concatenate([-jnp.sin(ang_r), jnp.sin(ang_r), -jnp.sin(ang_c), jnp.sin(ang_c)], axis=-1)
    cs = jnp.concatenate([jnp.ones((CTX_LEN, HEAD_DIM), F32), cs], axis=0)
    sn = jnp.concatenate([jnp.zeros((CTX_LEN, HEAD_DIM), F32), sn], axis=0)
    return jnp.tile(cs, (1, LANES // HEAD_DIM)), jnp.tile(sn, (1, LANES // HEAD_DIM)), cs.T, sn.T


def kernel(x, c, ctx, c_ctx, w_ada, b_ada, norm_gain, w_in, w_out, attn_sink, mlstm_gate_b, mlstm_norm, conv_w, conv_b,
           lru_gate_w, lru_gate_b, lru_lam, w_ffn_in, w_ffn_out):
    nb, n_lat, d = x.shape
    depth = w_ada.shape[0]
    assert ctx.shape[1] == CTX_LEN and n_lat % TOKEN_TILE == 0 and nb < MOD_ROWS
    cvec = jnp.concatenate([c, c_ctx[None, :], jnp.zeros((MOD_ROWS - nb - 1, d), F32)], axis=0)
    mod = _modulation(cvec, w_ada, b_ada).reshape(depth, MOD_ROWS, N_MOD, d)
    rope = _rope_tables(n_lat)
    xs = (ctx, x) if depth > 1 else jnp.concatenate([ctx, x], axis=1)
    for l in range(depth):
        w_tok, w_feat = _split_in_weights(w_in[l])
        ak, mq, mk, mv, mo, rx, ry, aqt, avt, grow = _in_projection(xs, mod[l], norm_gain[l], w_tok, w_feat, rope)
        att, mem, rec = _mixers(aqt, ak, avt, attn_sink[l],
                                (mq, mk, mv, mo, grow, mlstm_gate_b[l], mlstm_norm[l]),
                                (rx, ry, conv_w[l], conv_b[l], lru_gate_w[l], lru_gate_b[l], lru_lam[l]))
        xs = _out_ffn(xs, att, mem, rec, mod[l], norm_gain[l], w_out[l].astype(BF16), w_ffn_in[l].astype(BF16),
                      w_ffn_out[l].astype(BF16), first_tile=int(l == depth - 1))
    return xs
```

```python
import functools

import jax
import jax.numpy as jnp
import numpy as np
from jax import lax
from jax.experimental import pallas as pl
from jax.experimental.pallas import tpu as pltpu

F32 = jnp.float32
BF16 = jnp.bfloat16

D_MODEL = 1024
GRID_W = 64
CTX_LEN = 256
N_MOD = 6
NORM_EPS = 1e-6
ATT_HEADS = 8
ATT_KV_HEADS = 2
ATT_GROUP = ATT_HEADS // ATT_KV_HEADS
HEAD_DIM = 64
WINDOW = 128
ATT_BLOCK = 128
ROPE_BASE = 10000.0
ROPE_PAIRS = HEAD_DIM // 4
ATT_Q = ATT_HEADS * HEAD_DIM
ATT_KV = ATT_KV_HEADS * HEAD_DIM
MLSTM_HEADS = 4
MLSTM_HEAD_DIM = 64
M_W = MLSTM_HEADS * MLSTM_HEAD_DIM
M_GATES = 4 * MLSTM_HEADS
LRU_WIDTH = 256
LRU_BLOCKS = 4
LRU_BW = LRU_WIDTH // LRU_BLOCKS
LRU_C = 8.0
CONV_WIDTH = 4
CONV_LEFT = CONV_WIDTH // 2
D_FF = -(-8 * D_MODEL // (3 * 256)) * 256
LOG2_E = float(np.log2(np.e))
Q_SCALE = HEAD_DIM ** -0.5 * LOG2_E

LANES = 128
SUBLANES = 8
TOKEN_TILE = CTX_LEN
FF_CHUNK = 256
LRU_CHAINS = 4
MOD_ROWS = 16
VMEM_LIMIT = 56 * 1024 * 1024


def _params(sem):
    return pltpu.CompilerParams(dimension_semantics=sem, vmem_limit_bytes=VMEM_LIMIT)


def _dot(a, b):
    return jnp.dot(a, b, preferred_element_type=F32)


def _dot_nt(a, b):
    return lax.dot_general(a, b, (((1,), (1,)), ((), ())), preferred_element_type=F32)


def _dot_exact(a, b):
    return jnp.dot(a, b, preferred_element_type=F32, precision=lax.Precision.HIGHEST)


def _sigmoid(x):
    return 1.0 / (1.0 + jnp.exp(-x))


def _softplus(x):
    return jnp.maximum(x, 0.0) + jnp.log1p(jnp.exp(-jnp.abs(x)))


def _rms(x, g):
    return x * lax.rsqrt(jnp.mean(x * x, axis=-1, keepdims=True) + NORM_EPS) * g


def _mod_kernel(c_ref, w_ref, b_ref, o_ref):
    c = c_ref[...]
    s = (c * _sigmoid(c)).astype(BF16)
    o_ref[...] = _dot(s, w_ref[...].astype(BF16)) + b_ref[...]


def _modulation(cvec, w_ada, b_ada):
    depth, d, n = w_ada.shape
    tn = 1536
    return pl.pallas_call(
        _mod_kernel,
        grid=(depth, n // tn),
        in_specs=[
            pl.BlockSpec((MOD_ROWS, d), lambda l, j: (0, 0)),
            pl.BlockSpec((None, d, tn), lambda l, j: (l, 0, j)),
            pl.BlockSpec((None, 1, tn), lambda l, j: (l, 0, j)),
        ],
        out_specs=pl.BlockSpec((None, MOD_ROWS, tn), lambda l, j: (l, 0, j)),
        out_shape=jax.ShapeDtypeStruct((depth, MOD_ROWS, n), F32),
        compiler_params=_params(("arbitrary", "arbitrary")),
        name="modulation",
    )(cvec, w_ada, b_ada.reshape(depth, 1, n))


_IN_COLS = {}
_col = 0
for _name, _width in (("aq", ATT_Q), ("ak", ATT_KV), ("av", ATT_KV), ("mq", M_W), ("mk", M_W), ("mv", M_W), ("mo", M_W),
                      ("mg", M_GATES), ("rx", LRU_WIDTH), ("ry", LRU_WIDTH)):
    _IN_COLS[_name] = (_col, _col + _width)
    _col += _width
_TOKEN_MAJOR = ("ak", "mq", "mk", "mv", "mo", "rx", "ry")
_FEATURE_MAJOR = ("aq", "av", "mg")


def _split_in_weights(w):
    tok = jnp.concatenate([w[:, _IN_COLS[n][0]:_IN_COLS[n][1]] for n in _TOKEN_MAJOR], axis=1)
    feat = jnp.concatenate([w[:, _IN_COLS[n][0]:_IN_COLS[n][1]] for n in _FEATURE_MAJOR], axis=1).T
    return tok.astype(BF16), feat.astype(BF16)


def _rope_slab(x, cs, sn, first):
    swapped = jnp.where(first, pltpu.roll(x, LANES - ROPE_PAIRS, 1), pltpu.roll(x, ROPE_PAIRS, 1))
    return x * cs + swapped * sn


def _stream_specs(xs, tile):
    if not isinstance(xs, tuple):
        return [tile(xs.shape[-1])]
    d = xs[0].shape[-1]
    return [pl.BlockSpec((None, TOKEN_TILE, d), lambda b, i: (b, 0, 0)),
            pl.BlockSpec((None, TOKEN_TILE, d), lambda b, i: (b, jnp.maximum(i - 1, 0), 0))]


def _stream_tile(refs):
    if len(refs) == 1:
        return refs[0][...]
    return jnp.where(pl.program_id(1) == 0, refs[0][...], refs[1][...])


def _inproj_kernel(n_streams, *refs):
    x = _stream_tile(refs[:n_streams])
    (mod_ref, g_ref, w_ref, wt_ref, cs_ref, sn_ref, cst_ref, snt_ref,
     ak_ref, mq_ref, mk_ref, mv_ref, mo_ref, rx_ref, ry_ref, aqt_ref, avt_ref, gr_ref) = refs[n_streams:]
    h = _rms(x, g_ref[0:1, :]) * (1.0 + mod_ref[1:2, :]) + mod_ref[0:1, :]
    hb = h.astype(BF16)
    pt = _dot_nt(wt_ref[...], hb)
    cst = cst_ref[...]
    snt = snt_ref[...]
    rp = ROPE_PAIRS
    for hd in range(ATT_HEADS):
        xs = pt[hd * HEAD_DIM:(hd + 1) * HEAD_DIM, :]
        swapped = jnp.concatenate([xs[rp:2 * rp], xs[0:rp], xs[3 * rp:4 * rp], xs[2 * rp:3 * rp]], axis=0)
        roped = (xs * cst + swapped * snt) * Q_SCALE
        for qb in range(TOKEN_TILE // ATT_BLOCK):
            aqt_ref[qb, hd * HEAD_DIM:(hd + 1) * HEAD_DIM, :] = roped[:, qb * ATT_BLOCK:(qb + 1) * ATT_BLOCK]
    for qb in range(TOKEN_TILE // ATT_BLOCK):
        avt_ref[qb] = pt[ATT_Q:ATT_Q + ATT_KV, qb * ATT_BLOCK:(qb + 1) * ATT_BLOCK]
    gr_ref[...] = pt[ATT_Q + ATT_KV:, :]
    lane = lax.broadcasted_iota(jnp.int32, (x.shape[0], LANES), 1)
    first = (lane & (2 * ROPE_PAIRS - 1)) < ROPE_PAIRS
    ak_ref[...] = _rope_slab(_dot(hb, w_ref[:, 0:ATT_KV]), cs_ref[...], sn_ref[...], first)
    col = ATT_KV
    for ref in (mq_ref, mk_ref, mv_ref, mo_ref, rx_ref, ry_ref):
        n = ref.shape[-1]
        ref[...] = _dot(hb, w_ref[:, col:col + n])
        col += n


def _in_projection(xs, mod, gain, w_tok, w_feat, rope):
    streams = xs if isinstance(xs, tuple) else (xs,)
    nb = streams[0].shape[0]
    t = sum(s.shape[1] for s in streams)
    nt = t // TOKEN_TILE
    tile = lambda n: pl.BlockSpec((None, TOKEN_TILE, n), lambda b, i: (b, i, 0))
    tile_t = lambda n: pl.BlockSpec((None, n, TOKEN_TILE), lambda b, i: (b, 0, i))
    const = lambda a: pl.BlockSpec(a.shape, lambda b, i: (0,) * a.ndim)
    width = lambda n: _IN_COLS[n][1] - _IN_COLS[n][0]
    out_shapes = [jax.ShapeDtypeStruct((nb, t, width(n)), F32) for n in _TOKEN_MAJOR]
    per_tile = TOKEN_TILE // ATT_BLOCK
    out_shapes += [jax.ShapeDtypeStruct((nb, nt * per_tile, width(n), ATT_BLOCK), F32) for n in ("aq", "av")]
    out_shapes += [jax.ShapeDtypeStruct((nb, nt, M_GATES, TOKEN_TILE), F32)]
    out_specs = [tile(width(n)) for n in _TOKEN_MAJOR]
    out_specs += [pl.BlockSpec((None, per_tile, width(n), ATT_BLOCK), lambda b, i: (b, i, 0, 0)) for n in ("aq", "av")]
    out_specs += [pl.BlockSpec((None, None, M_GATES, TOKEN_TILE), lambda b, i: (b, i, 0, 0))]
    rope_cs, rope_sn, rope_cst, rope_snt = rope
    return pl.pallas_call(
        functools.partial(_inproj_kernel, len(streams)),
        grid=(nb, nt),
        in_specs=_stream_specs(xs, tile) + [
            pl.BlockSpec((None, N_MOD, mod.shape[-1]), lambda b, i: (jnp.where(i == 0, nb, b), 0, 0)),
            const(gain), const(w_tok), const(w_feat),
            pl.BlockSpec((TOKEN_TILE, LANES), lambda b, i: (i, 0)),
            pl.BlockSpec((TOKEN_TILE, LANES), lambda b, i: (i, 0)),
            pl.BlockSpec((HEAD_DIM, TOKEN_TILE), lambda b, i: (0, i)),
            pl.BlockSpec((HEAD_DIM, TOKEN_TILE), lambda b, i: (0, i)),
        ],
        out_specs=out_specs,
        out_shape=out_shapes,
        compiler_params=_params(("parallel", "arbitrary")),
        name="in_projection",
    )(*streams, mod, gain, w_tok, w_feat, rope_cs, rope_sn, rope_cst, rope_snt)


def _attn_block(j, nblk, sink_ref, qt_ref, kp_ref, kc_ref, kn_ref, kx_ref, vp_ref, vc_ref, vn_ref, vx_ref, o_ref):
    ctx_blocks = CTX_LEN // ATT_BLOCK
    nq = ATT_BLOCK
    cols = ATT_GROUP * nq
    c = lax.broadcasted_iota(jnp.int32, (ATT_BLOCK, cols), 0)
    r = lax.broadcasted_iota(jnp.int32, (ATT_BLOCK, cols), 1) & (nq - 1)
    m_prev = (c >= r) & (j >= ctx_blocks + 1)
    m_next = (c <= r) & (j >= ctx_blocks) & (j <= nblk - 2)
    is_lat = j >= ctx_blocks
    col = lax.broadcasted_iota(jnp.int32, (1, cols), 1)
    neg = -jnp.inf
    kp = kp_ref[...].astype(BF16)
    kc = kc_ref[...].astype(BF16)
    kn = kn_ref[...].astype(BF16)
    kx = kx_ref[...].astype(BF16)
    zeros = jnp.zeros((HEAD_DIM, cols), BF16)
    scores = []
    for kh in range(ATT_KV_HEADS):
        hd0 = kh * ATT_GROUP
        qg = jnp.concatenate([qt_ref[(hd0 + g) * HEAD_DIM:(hd0 + g + 1) * HEAD_DIM, :] for g in range(ATT_GROUP)],
                             axis=1).astype(BF16)
        rhs = jnp.concatenate([qg, zeros] if kh == 0 else [zeros, qg], axis=0)
        s_x = _dot(kx, rhs)
        scores.append((jnp.where(m_prev, _dot(kp, rhs), neg), jnp.where(is_lat, _dot(kc, rhs), neg),
                       jnp.where(m_next, _dot(kn, rhs), neg), s_x[:ATT_BLOCK], s_x[ATT_BLOCK:]))
    for kh in range(ATT_KV_HEADS):
        hd0 = kh * ATT_GROUP
        sink = jnp.full((1, cols), sink_ref[hd0], F32)
        for g in range(1, ATT_GROUP):
            sink = jnp.where(col >= g * nq, sink_ref[hd0 + g], sink)
        sink = sink * LOG2_E
        s_p, s_c, s_n, s_x0, s_x1 = scores[kh]
        m_el = jnp.maximum(jnp.maximum(jnp.maximum(s_p, s_c), jnp.maximum(s_n, s_x0)), s_x1)
        m = jnp.maximum(jnp.max(m_el, axis=0, keepdims=True), sink)
        p_p = jnp.exp2(s_p - m)
        p_c = jnp.exp2(s_c - m)
        p_n = jnp.exp2(s_n - m)
        p_x0 = jnp.exp2(s_x0 - m)
        p_x1 = jnp.exp2(s_x1 - m)
        den = jnp.sum((p_p + p_c) + (p_n + p_x0) + p_x1, axis=0, keepdims=True) + jnp.exp2(sink - m)
        vs = slice(kh * HEAD_DIM, (kh + 1) * HEAD_DIM)
        o = (_dot(vp_ref[vs, :].astype(BF16), p_p.astype(BF16)) + _dot(vc_ref[vs, :].astype(BF16), p_c.astype(BF16))
             + _dot(vn_ref[vs, :].astype(BF16), p_n.astype(BF16))
             + _dot(vx_ref[0, vs, :].astype(BF16), p_x0.astype(BF16)) + _dot(vx_ref[1, vs, :].astype(BF16), p_x1.astype(BF16)))
        o = o / den
        for pair in range(ATT_GROUP // 2):
            two = jnp.concatenate([o[:, (2 * pair) * nq:(2 * pair + 1) * nq], o[:, (2 * pair + 1) * nq:(2 * pair + 2) * nq]],
                                  axis=0)
            lo = (hd0 + 2 * pair) * HEAD_DIM
            o_ref[:, lo:lo + 2 * HEAD_DIM] = two.T


def _attention_specs(aqt, ak, avt, sink, qblock):
    nb, t, _ = ak.shape
    nblk = t // ATT_BLOCK
    prev = lambda q: jnp.maximum(q - 1, 0)
    cur = lambda q: q
    nxt = lambda q: jnp.minimum(q + 1, nblk - 1)
    kb = lambda f: pl.BlockSpec((None, ATT_BLOCK, ATT_KV), lambda b, *g: (b, f(qblock(*g)), 0))
    vb = lambda f: pl.BlockSpec((None, None, ATT_KV, ATT_BLOCK), lambda b, *g: (b, f(qblock(*g)), 0, 0))
    in_specs = [
        pl.BlockSpec(memory_space=pltpu.SMEM),
        pl.BlockSpec((None, None, ATT_Q, ATT_BLOCK), lambda b, *g: (b, qblock(*g), 0, 0)),
        kb(prev), kb(cur), kb(nxt), pl.BlockSpec((None, CTX_LEN, ATT_KV), lambda b, *g: (b, 0, 0)),
        vb(prev), vb(cur), vb(nxt),
        pl.BlockSpec((None, CTX_LEN // ATT_BLOCK, ATT_KV, ATT_BLOCK), lambda b, *g: (b, 0, 0, 0)),
    ]
    args = [sink, aqt, ak, ak, ak, ak, avt, avt, avt, avt]
    out_spec = pl.BlockSpec((None, ATT_BLOCK, ATT_Q), lambda b, *g: (b, qblock(*g), 0))
    return in_specs, args, out_spec, jax.ShapeDtypeStruct((nb, t, ATT_Q), F32)


def _seq_block(d, j, nblk):
    return jnp.where((d == 0) | (j == 0), j, nblk - j)


def _lane_scan(x, op, fill, rev):
    n = x.shape[-1]
    lane = lax.broadcasted_iota(jnp.int32, x.shape, 1)
    step = 1
    while step < n:
        if rev:
            shifted = jnp.where(lane < n - step, pltpu.roll(x, n - step, 1), fill)
        else:
            shifted = jnp.where(lane >= step, pltpu.roll(x, step, 1), fill)
        x = op(x, shifted)
        step *= 2
    return x


def _split3(x):
    hi = x.astype(BF16).astype(F32)
    mid = (x - hi).astype(BF16).astype(F32)
    lo = x - hi - mid
    return hi, mid, lo


def _mlstm_tables():
    H, dh = MLSTM_HEADS, MLSTM_HEAD_DIM
    n_pairs = H // 2
    n_slabs = H + 3 * n_pairs
    sel = np.zeros((LANES, n_slabs * LANES), np.float32)
    for part in range(3):
        base = part * 4 * H
        for h in range(H):
            sel[base + h, h * LANES:(h + 1) * LANES] = 1.0
        for qty in range(3):
            for p in range(n_pairs):
                slab = H + qty * n_pairs + p
                for half in range(2):
                    row = base + H * qty + 2 * p + half
                    sel[row, slab * LANES + half * dh:slab * LANES + (half + 1) * dh] = 1.0
    ones_bd = np.zeros((2 * TOKEN_TILE, LANES), np.float32)
    ones_bd[:TOKEN_TILE, :dh] = 1.0
    ones_bd[TOKEN_TILE:, dh:] = 1.0
    avg = np.zeros((3 * LANES, LANES), np.float32)
    for part in range(3):
        for half in range(2):
            avg[part * LANES + half * dh:part * LANES + (half + 1) * dh, half * dh:(half + 1) * dh] = 1.0 / dh
    return jnp.asarray(sel, BF16), jnp.asarray(ones_bd, BF16), jnp.asarray(avg, BF16)


def _mlstm_block_terms(rev, g, tri_ref):
    L = TOKEN_TILE
    H = MLSTM_HEADS
    ioff = 2 * H if rev else 0
    li = g[ioff:ioff + H, :]
    lf = -_softplus(-g[ioff + H:ioff + 2 * H, :])
    parts = _dot(jnp.concatenate(_split3(lf), axis=0).astype(BF16), tri_ref[...])
    b = parts[0:H] + parts[H:2 * H] + parts[2 * H:3 * H]
    r = li - b
    cm = _lane_scan(r, jnp.maximum, -jnp.inf, rev)
    last = 0 if rev else L - 1
    g_tot = b[:, last:last + 1]
    a = g_tot - b + li
    m_loc = jnp.max(a, axis=1, keepdims=True)
    w = jnp.exp(a - m_loc)
    return r, cm, b, w, g_tot, m_loc


def _mlstm_stash_terms(terms, sel_ref, cols_ref, r_ref, gm_ref, slot):
    L = TOKEN_TILE
    H, dh = MLSTM_HEADS, MLSTM_HEAD_DIM
    r, cm, b, w, g_tot, m_loc = terms
    stacked = jnp.concatenate([cm, b, w, jnp.zeros((H, L), F32)], axis=0)
    hi, mid, lo = _split3(stacked)
    pad = jnp.zeros((LANES - 3 * 4 * H, L), F32)
    cols_ref[slot] = _dot(jnp.concatenate([hi, mid, lo, pad], axis=0).T.astype(BF16), sel_ref[...])
    r_ref[slot] = jnp.concatenate([r, jnp.zeros((SUBLANES - H, L), F32)], axis=0)
    lane_row = lax.broadcasted_iota(jnp.int32, (1, LANES), 1)

    def layout(col):
        rows = [jnp.broadcast_to(col, (H, LANES))]
        rows += [jnp.where(lane_row < dh, col[2 * p:2 * p + 1, :], col[2 * p + 1:2 * p + 2, :]) for p in range(H // 2)]
        rows += [jnp.zeros((SUBLANES - H - H // 2, LANES), F32)]
        return jnp.concatenate(rows, axis=0)

    gm_ref[slot] = jnp.concatenate([layout(g_tot), layout(m_loc)], axis=0)


def _mlstm_kernel(q_ref, k_ref, v_ref, o_ref, gr_ref, gnext_ref, gb_ref, ng_ref, tri_ref, sel_ref, ones_ref, avg_ref, out_ref,
                  hf_ref, c_ref, n_ref, m_ref, cols_ref, r_ref, gm_ref, rev, init=False):
    d = pl.program_id(1)
    j = pl.program_id(2)
    nblk = pl.num_programs(2)
    L = TOKEN_TILE
    H, dh = MLSTM_HEADS, MLSTM_HEAD_DIM
    n_pairs = H // 2
    row0 = pl.multiple_of(_seq_block(d, j, nblk) * L, L)
    slot = j & 1

    if init:
        c_ref[...] = jnp.zeros_like(c_ref)
        n_ref[...] = jnp.zeros_like(n_ref)
        m_ref[...] = jnp.zeros_like(m_ref)
        _mlstm_stash_terms(_mlstm_block_terms(rev, gr_ref[...] + gb_ref[...], tri_ref), sel_ref, cols_ref, r_ref, gm_ref, 0)
        return

    def body(rev):
        next_terms = _mlstm_block_terms(rev, gnext_ref[...] + gb_ref[...], tri_ref)

        r = r_ref[slot]
        gm = gm_ref[slot]
        slab = lambda i: cols_ref[slot, :, i * LANES:(i + 1) * LANES]
        m0 = m_ref[...]
        g_tot, m_loc = gm[0:SUBLANES], gm[SUBLANES:]
        m_new = jnp.maximum(g_tot + m0, m_loc)
        f_prev = jnp.exp(g_tot + m0 - m_new)
        f_loc = jnp.exp(m_loc - m_new)
        m_ref[...] = m_new
        lane = lax.broadcasted_iota(jnp.int32, (L, LANES), 1)
        first_head = lane < dh
        t_idx = lax.broadcasted_iota(jnp.int32, (L, LANES), 0)
        kt = (k_ref[...] * (dh ** -0.5)).T.astype(BF16)
        zeros_half = jnp.zeros((dh, L), BF16)
        bd_mask = ((lax.broadcasted_iota(jnp.int32, (LANES, LANES), 0) < dh)
                   == (lax.broadcasted_iota(jnp.int32, (LANES, LANES), 1) < dh))
        for p in range(n_pairs):
            ps = slice(p * LANES, (p + 1) * LANES)
            qb = q_ref[:, ps].astype(BF16)
            v = v_ref[:, ps]
            kt_pair = kt[ps, :]
            sm = []
            for half in range(2):
                h = 2 * p + half
                kt_h = kt[h * dh:(h + 1) * dh, :]
                kt_masked = jnp.concatenate([kt_h, zeros_half] if half == 0 else [zeros_half, kt_h], axis=0)
                s = _dot(qb, kt_masked)
                m_b = jnp.maximum(slab(h), m0[h:h + 1, :])
                for lt in range(L // LANES):
                    s_idx = lane + lt * LANES
                    vis = (s_idx >= t_idx) if rev else (s_idx <= t_idx)
                    e = jnp.exp(jnp.where(vis, r[h:h + 1, lt * LANES:(lt + 1) * LANES] - m_b, -jnp.inf))
                    sm.append((s[:, lt * LANES:(lt + 1) * LANES] * e).astype(BF16))
            sm = jnp.concatenate(sm, axis=1)
            v_bd = jnp.concatenate([jnp.where(first_head, v, 0.0), jnp.where(first_head, 0.0, v)], axis=0).astype(BF16)
            m0_pair = m0[H + p:H + p + 1, :]
            m_pair = jnp.maximum(slab(H + p), m0_pair)
            f_pair = jnp.exp(m0_pair - m_pair)
            num = _dot(sm, v_bd) + f_pair * _dot(qb, c_ref[p].astype(BF16))
            den = _dot(sm, ones_ref[...]) + f_pair * _dot(qb, n_ref[p].astype(BF16))
            hh = num / jnp.maximum(jnp.abs(den), jnp.exp(-(slab(H + n_pairs + p) + m_pair)))
            w_pair = slab(H + 2 * n_pairs + p)
            c_loc = jnp.where(bd_mask, _dot(kt_pair, (w_pair * v).astype(BF16)), 0.0)
            n_loc = jnp.where(bd_mask, _dot(kt_pair, w_pair.astype(BF16)), 0.0)
            c_ref[p] = f_prev[H + p:H + p + 1, :] * c_ref[p] + f_loc[H + p:H + p + 1, :] * c_loc
            n_ref[p] = f_prev[H + p:H + p + 1, :] * n_ref[p] + f_loc[H + p:H + p + 1, :] * n_loc
            if not rev:
                hf_ref[pl.ds(row0, L), ps] = hh
            else:
                ht = hf_ref[pl.ds(row0, L), ps] + hh
                mu = _dot(jnp.concatenate(_split3(ht), axis=1).astype(BF16), avg_ref[...])
                cen = ht - mu
                var = _dot(jnp.concatenate(_split3(cen * cen), axis=1).astype(BF16), avg_ref[...])
                y = cen * lax.rsqrt(var + NORM_EPS) * ng_ref[:, ps]
                out_ref[:, ps] = _sigmoid(o_ref[:, ps]) * y
        _mlstm_stash_terms(next_terms, sel_ref, cols_ref, r_ref, gm_ref, 1 - slot)

    body(rev)


def _mlstm_specs(mq, mk, mv, mo, grow, gate_b, norm_g):
    nb, t, w = mq.shape
    nblk = t // TOKEN_TILE
    blk = lambda b, d, j: (b, _seq_block(d, j, nblk), 0)
    tile = pl.BlockSpec((None, TOKEN_TILE, w), blk)
    gb = gate_b.reshape(M_GATES, 1)
    ng = norm_g.reshape(1, w)
    sel, ones_bd, avg = _mlstm_tables()
    upper = np.triu(np.ones((TOKEN_TILE, TOKEN_TILE), np.float32))
    tri = jnp.asarray(np.stack([upper, upper.T]), BF16)
    const = lambda a: pl.BlockSpec(a.shape, lambda b, d, j: (0,) * a.ndim)
    n_pairs = MLSTM_HEADS // 2
    in_specs = [
        tile, tile, tile, tile,
        pl.BlockSpec((None, None, M_GATES, TOKEN_TILE), lambda b, d, j: (b, _seq_block(d, j, nblk), 0, 0)),
        pl.BlockSpec((None, None, M_GATES, TOKEN_TILE),
                     lambda b, d, j: (b, _seq_block(d, jnp.minimum(j + 1, nblk - 1), nblk), 0, 0)),
        const(gb), const(ng),
        pl.BlockSpec((None, TOKEN_TILE, TOKEN_TILE), lambda b, d, j: (d, 0, 0)),
        const(sel), const(ones_bd), const(avg),
    ]
    args = [mq, mk, mv, mo, grow, grow, gb, ng, tri, sel, ones_bd, avg]
    out_spec = pl.BlockSpec((None, TOKEN_TILE, w), lambda b, d, j: (b, jnp.where(d == 0, 0, _seq_block(d, j, nblk)), 0))
    scratch = [
        pltpu.VMEM((t, w), F32),
        pltpu.VMEM((n_pairs, LANES, LANES), F32),
        pltpu.VMEM((n_pairs, LANES, LANES), F32),
        pltpu.VMEM((SUBLANES, LANES), F32),
        pltpu.VMEM((2, TOKEN_TILE, sel.shape[1]), F32),
        pltpu.VMEM((2, SUBLANES, TOKEN_TILE), F32),
        pltpu.VMEM((2, 2 * SUBLANES, LANES), F32),
    ]
    return in_specs, args, out_spec, jax.ShapeDtypeStruct((nb, t, w), F32), scratch


def _lru_kernel(x_ref, xp_ref, xn_ref, y_ref, cw_ref, cb_ref, gw_ref, gb_ref, lam_ref, out_ref,
                hf_ref, xe_ref, carry_ref, a_ref, u_ref, hb_ref, hl_ref, al_ref, rev, init=False):
    d = pl.program_id(1)
    j = pl.program_id(2)
    nblk = pl.num_programs(2)
    L = TOKEN_TILE
    pos = _seq_block(d, j, nblk)
    row0 = pl.multiple_of(pos * L, L)

    if init:
        carry_ref[...] = jnp.zeros_like(carry_ref)
        return

    has_prev = pos >= 2
    has_next = (pos >= 1) & (pos <= nblk - 2)
    xe_ref[0:SUBLANES, :] = jnp.where(has_prev, xp_ref[...], 0.0)
    xe_ref[SUBLANES:SUBLANES + L, :] = x_ref[...]
    xe_ref[SUBLANES + L:, :] = jnp.where(has_next, xn_ref[...], 0.0)
    seq = cb_ref[...]
    for tap in range(CONV_WIDTH):
        off = SUBLANES - CONV_LEFT + tap
        seq = seq + cw_ref[tap:tap + 1, :] * xe_ref[off:off + L, :]
    sb = seq.astype(BF16)
    r = 0.5 + 0.5 * jnp.tanh(0.5 * (_dot(sb, gw_ref[0]) + gb_ref[0:1, :]))
    i = 0.5 + 0.5 * jnp.tanh(0.5 * (_dot(sb, gw_ref[1]) + gb_ref[1:2, :]))
    log_a = (-LRU_C * _softplus(-lam_ref[...])) * r
    a0 = jnp.exp(log_a)
    th = jnp.tanh(log_a)
    u0 = jnp.sqrt(-2.0 * th) * lax.rsqrt(1.0 - th) * (i * seq)
    n_lt = LRU_WIDTH // LANES
    for lt in range(n_lt):
        a_ref[lt] = a0[:, lt * LANES:(lt + 1) * LANES]
        u_ref[lt] = u0[:, lt * LANES:(lt + 1) * LANES]
    seg = L // (LRU_CHAINS * SUBLANES)
    sub = lax.broadcasted_iota(jnp.int32, (SUBLANES, LRU_WIDTH), 0)
    rows = lambda c, i: pl.ds(c * seg * SUBLANES + i, SUBLANES, stride=seg)
    strided = lambda ref, c, i: jnp.concatenate([ref[lt, rows(c, i), :] for lt in range(n_lt)], axis=1)

    def scan(rev):
        steps = range(seg - 1, -1, -1) if rev else range(seg)
        chains = range(LRU_CHAINS - 1, -1, -1) if rev else range(LRU_CHAINS)
        edge = SUBLANES - 1 if rev else 0
        shift = lambda x, n: pltpu.roll(x, SUBLANES - n if rev else n, 0)
        ends, prods = {}, {}
        for c in chains:
            h = jnp.zeros((SUBLANES, LRU_WIDTH), F32)
            ac = jnp.ones((SUBLANES, LRU_WIDTH), F32)
            for i in steps:
                ai = strided(a_ref, c, i)
                h = ai * h + strided(u_ref, c, i)
                ac = ai * ac
                hl_ref[c * seg + i] = h
                al_ref[c * seg + i] = ac
            step = 1
            while step < SUBLANES:
                ok = (sub < SUBLANES - step) if rev else (sub >= step)
                h = jnp.where(ok, ac * shift(h, step) + h, h)
                ac = jnp.where(ok, ac * shift(ac, step), ac)
                step *= 2
            ends[c], prods[c] = h, ac
        carry = carry_ref[...]
        for c in chains:
            true_ends = ends[c] + prods[c] * carry
            carry_in = jnp.where(sub == edge, carry, shift(true_ends, 1))
            carry = true_ends[SUBLANES - 1 - edge:SUBLANES - edge, :]
            for i in range(seg):
                hi = hl_ref[c * seg + i] + al_ref[c * seg + i] * carry_in
                for lt in range(n_lt):
                    hb_ref[lt, rows(c, i), :] = hi[:, lt * LANES:(lt + 1) * LANES]
        carry_ref[...] = carry
        return jnp.concatenate([hb_ref[lt] for lt in range(n_lt)], axis=1)

    def forward():
        hf_ref[pl.ds(row0, L), :] = scan(False)

    def backward():
        h = hf_ref[pl.ds(row0, L), :] + scan(True)
        y = y_ref[...]
        gelu = 0.5 * y * (1.0 + jnp.tanh(np.sqrt(2.0 / np.pi).astype(np.float32) * (y + 0.044715 * (y * y * y))))
        out_ref[...] = h * gelu

    backward() if rev else forward()


def _rglru_specs(rx, ry, conv_w, conv_b, gate_w, gate_b, lam):
    nb, t, w = rx.shape
    nblk = t // TOKEN_TILE
    per_tile = TOKEN_TILE // SUBLANES
    n8 = t // SUBLANES
    blk = lambda b, d, j: (b, _seq_block(d, j, nblk), 0)
    tile = pl.BlockSpec((None, TOKEN_TILE, w), blk)
    halo = lambda f: pl.BlockSpec((None, SUBLANES, w), lambda b, d, j: (b, f(_seq_block(d, j, nblk)), 0))
    prev8 = lambda p: jnp.maximum(p * per_tile - 1, 0)
    next8 = lambda p: jnp.minimum((p + 1) * per_tile, n8 - 1)
    eye = jnp.eye(LRU_BLOCKS, dtype=gate_w.dtype)
    gw = jnp.einsum('dgnij,nm->dgnimj', gate_w, eye).reshape(2, 2, w, w).astype(BF16)
    cb = conv_b.reshape(1, w)
    in_specs = [
        tile, halo(prev8), halo(next8), tile,
        pl.BlockSpec(conv_w.shape, lambda b, d, j: (0, 0)),
        pl.BlockSpec(cb.shape, lambda b, d, j: (0, 0)),
        pl.BlockSpec((None, 2, w, w), lambda b, d, j: (d, 0, 0, 0)),
        pl.BlockSpec((None, 2, w), lambda b, d, j: (d, 0, 0)),
        pl.BlockSpec((None, 1, w), lambda b, d, j: (d, 0, 0)),
    ]
    args = [rx, rx, rx, ry, conv_w, cb, gw, gate_b, lam.reshape(2, 1, w)]
    out_spec = pl.BlockSpec((None, TOKEN_TILE, w), lambda b, d, j: (b, jnp.where(d == 0, 0, _seq_block(d, j, nblk)), 0))
    scratch = [
        pltpu.VMEM((t, w), F32),
        pltpu.VMEM((TOKEN_TILE + 2 * SUBLANES, w), F32),
        pltpu.VMEM((1, w), F32),
        pltpu.VMEM((w // LANES, TOKEN_TILE, LANES), F32),
        pltpu.VMEM((w // LANES, TOKEN_TILE, LANES), F32),
        pltpu.VMEM((w // LANES, TOKEN_TILE, LANES), F32),
        pltpu.VMEM((TOKEN_TILE // SUBLANES, SUBLANES, w), F32),
        pltpu.VMEM((TOKEN_TILE // SUBLANES, SUBLANES, w), F32),
    ]
    return in_specs, args, out_spec, jax.ShapeDtypeStruct((nb, t, w), F32), scratch


def _mixers_kernel(n_att, n_mem, n_rec, n_mem_scratch, *refs):
    d = pl.program_id(1)
    j = pl.program_id(2)
    nblk = pl.num_programs(2)
    n_in = n_att + n_mem + n_rec
    att_in, mem_in, rec_in = refs[:n_att], refs[n_att:n_att + n_mem], refs[n_att + n_mem:n_in]
    att_out, mem_out, rec_out = refs[n_in:n_in + 3]
    mem_scratch = refs[n_in + 3:n_in + 3 + n_mem_scratch]
    rec_scratch = refs[n_in + 3 + n_mem_scratch:]
    q_per_tile = TOKEN_TILE // ATT_BLOCK

    def body(rev):
        @pl.when(j == 0)
        def _():
            _mlstm_kernel(*mem_in, mem_out, *mem_scratch, rev=rev, init=True)
            _lru_kernel(*rec_in, rec_out, *rec_scratch, rev=rev, init=True)

        _attn_block(q_per_tile * _seq_block(d, j, nblk) + int(rev), q_per_tile * nblk, *att_in, att_out)
        _mlstm_kernel(*mem_in, mem_out, *mem_scratch, rev=rev)
        _lru_kernel(*rec_in, rec_out, *rec_scratch, rev=rev)

    pl.when(d == 0)(functools.partial(body, False))
    pl.when(d == 1)(functools.partial(body, True))


def _mixers(aqt, ak, avt, sink, mlstm_args, rglru_args):
    nb, t, _ = ak.shape
    nblk = t // TOKEN_TILE
    assert TOKEN_TILE == 2 * ATT_BLOCK
    qblock = lambda d, j: (TOKEN_TILE // ATT_BLOCK) * _seq_block(d, j, nblk) + d
    att_specs, att_args, att_out, att_shape = _attention_specs(aqt, ak, avt, sink, qblock)
    mem_specs, mem_args, mem_out, mem_shape, mem_scratch = _mlstm_specs(*mlstm_args)
    rec_specs, rec_args, rec_out, rec_shape, rec_scratch = _rglru_specs(*rglru_args)
    return pl.pallas_call(
        functools.partial(_mixers_kernel, len(att_args), len(mem_args), len(rec_args), len(mem_scratch)),
        grid=(nb, 2, nblk),
        in_specs=att_specs + mem_specs + rec_specs,
        out_specs=[att_out, mem_out, rec_out],
        out_shape=[att_shape, mem_shape, rec_shape],
        scratch_shapes=mem_scratch + rec_scratch,
        compiler_params=_params(("parallel", "arbitrary", "arbitrary")),
        name="mixers",
    )(*att_args, *mem_args, *rec_args)


def _outffn_kernel(n_streams, is_ctx_tile, *refs):
    k = pl.program_id(0)
    streams = refs[:n_streams]
    (att_ref, mem_ref, rec_ref, mod_ref, modp_ref, g_ref, wo_ref, wi_ref, wd_ref, out_ref,
     act_ref, h0_ref, h1_ref, x0_ref, x1_ref) = refs[n_streams:]

    @pl.when(k == 0)
    def _():
        h1_ref[...] = jnp.zeros_like(h1_ref)
        x1_ref[...] = jnp.zeros_like(x1_ref)

    def body(h_prev, x_prev, h_next, x_next):
        x = streams[0][...] if n_streams == 1 else jnp.where(is_ctx_tile(k), streams[0][...], streams[1][...])
        mix = (_dot(att_ref[...].astype(BF16), wo_ref[0:ATT_Q, :])
               + _dot(mem_ref[...].astype(BF16), wo_ref[ATT_Q:ATT_Q + M_W, :])
               + _dot(rec_ref[...].astype(BF16), wo_ref[ATT_Q + M_W:, :]))
        x1 = x + mod_ref[2:3, :] * _rms(mix, g_ref[1:2, :])
        x_next[...] = x1
        h_next[...] = (_rms(x1, g_ref[2:3, :]) * (1.0 + mod_ref[4:5, :]) + mod_ref[3:4, :]).astype(BF16)
        h = h_prev[...]
        for c0 in range(0, D_FF, FF_CHUNK):
            gate = _dot(h, wi_ref[:, c0:c0 + FF_CHUNK])
            up = _dot(h, wi_ref[:, D_FF + c0:D_FF + c0 + FF_CHUNK])
            half = 0.5 * gate
            act_ref[:, c0:c0 + FF_CHUNK] = ((half + half * jnp.tanh(half)) * up).astype(BF16)
        f = _dot(act_ref[...], wd_ref[...])
        out_ref[...] = x_prev[...] + modp_ref[5:6, :] * _rms(f, g_ref[3:4, :])

    pl.when((k & 1) == 0)(functools.partial(body, h1_ref, x1_ref, h0_ref, x0_ref))
    pl.when((k & 1) == 1)(functools.partial(body, h0_ref, x0_ref, h1_ref, x1_ref))


def _out_ffn(xs, att, mem, rec, mod, gain, w_out, w_ffn_in, w_down, first_tile=0):
    streams = xs if isinstance(xs, tuple) else (xs,)
    assert len(streams) == 1 or first_tile == 0
    nb, _, d = streams[0].shape
    t = sum(s.shape[1] for s in streams)
    per_row = t // TOKEN_TILE - first_tile
    n_tiles = nb * per_row

    def tile_of(k):
        return k // per_row, k % per_row + first_tile

    cur = lambda k: tile_of(jnp.minimum(k, n_tiles - 1))
    prev = lambda k: tile_of(jnp.maximum(k - 1, 0))
    tile = lambda n: pl.BlockSpec((None, TOKEN_TILE, n), lambda k: (*cur(k), 0))
    mod_row = lambda bi: (jnp.where(bi[1] == 0, nb, bi[0]), 0, 0)
    const = lambda a: pl.BlockSpec(a.shape, lambda k: (0,) * a.ndim, pipeline_mode=pl.Buffered(1))
    if len(streams) == 1:
        stream_specs = [tile(d)]
    else:
        stream_specs = [pl.BlockSpec((None, TOKEN_TILE, d), lambda k: (cur(k)[0], 0, 0)),
                        pl.BlockSpec((None, TOKEN_TILE, d), lambda k: (cur(k)[0], jnp.maximum(cur(k)[1] - 1, 0), 0))]
    is_ctx_tile = lambda k: cur(k)[1] == 0
    return pl.pallas_call(
        functools.partial(_outffn_kernel, len(streams), is_ctx_tile),
        grid=(n_tiles + 1,),
        in_specs=stream_specs + [
            tile(ATT_Q), tile(M_W), tile(LRU_WIDTH),
            pl.BlockSpec((None, N_MOD, d), lambda k: mod_row(cur(k))),
            pl.BlockSpec((None, N_MOD, d), lambda k: mod_row(prev(k))),
            const(gain), const(w_out), const(w_ffn_in), const(w_down),
        ],
        out_specs=pl.BlockSpec((None, TOKEN_TILE, d), lambda k: (prev(k)[0], prev(k)[1] - first_tile, 0)),
        out_shape=jax.ShapeDtypeStruct((nb, per_row * TOKEN_TILE, d), F32),
        scratch_shapes=[pltpu.VMEM((TOKEN_TILE, D_FF), BF16),
                        pltpu.VMEM((TOKEN_TILE, d), BF16), pltpu.VMEM((TOKEN_TILE, d), BF16),
                        pltpu.VMEM((TOKEN_TILE, d), F32), pltpu.VMEM((TOKEN_TILE, d), F32)],
        compiler_params=_params(("arbitrary",)),
        name="out_ffn",
    )(*streams, att, mem, rec, mod, mod, gain, w_out, w_ffn_in, w_down)


def _rope_tables(n_lat):
    t = jnp.arange(n_lat)
    row = (t // GRID_W).astype(F32)
    col = (t % GRID_W).astype(F32)
    freqs = ROPE_BASE ** (-jnp.arange(ROPE_PAIRS, dtype=F32) / ROPE_PAIRS)
    ang_r = row[:, None] * freqs
    ang_c = col[:, None] * freqs
    cs = jnp.concatenate([jnp.cos(ang_r), jnp.cos(ang_r), jnp.cos(ang_c), jnp.cos(ang_c)], axis=-1)
    sn = jnp.concatenate([-jnp.sin(ang_r), jnp.sin(ang_r), -jnp.sin(ang_c), jnp.sin(ang_c)], axis=-1)
    cs = jnp.concatenate([jnp.ones((CTX_LEN, HEAD_DIM), F32), cs], axis=0)
    sn = jnp.concatenate([jnp.zeros((CTX_LEN, HEAD_DIM), F32), sn], axis=0)
    return jnp.tile(cs, (1, LANES // HEAD_DIM)), jnp.tile(sn, (1, LANES // HEAD_DIM)), cs.T, sn.T


def kernel(x, c, ctx, c_ctx, w_ada, b_ada, norm_gain, w_in, w_out, attn_sink, mlstm_gate_b, mlstm_norm, conv_w, conv_b,
           lru_gate_w, lru_gate_b, lru_lam, w_ffn_in, w_ffn_out):
    nb, n_lat, d = x.shape
    depth = w_ada.shape[0]
    assert ctx.shape[1] == CTX_LEN and n_lat % TOKEN_TILE == 0 and nb < MOD_ROWS
    cvec = jnp.concatenate([c, c_ctx[None, :], jnp.zeros((MOD_ROWS - nb - 1, d), F32)], axis=0)
    mod = _modulation(cvec, w_ada, b_ada).reshape(depth, MOD_ROWS, N_MOD, d)
    rope = _rope_tables(n_lat)
    xs = (ctx, x) if depth > 1 else jnp.concatenate([ctx, x], axis=1)
    for l in range(depth):
        w_tok, w_feat = _split_in_weights(w_in[l])
        ak, mq, mk, mv, mo, rx, ry, aqt, avt, grow = _in_projection(xs, mod[l], norm_gain[l], w_tok, w_feat, rope)
        att, mem, rec = _mixers(aqt, ak, avt, attn_sink[l],
                                (mq, mk, mv, mo, grow, mlstm_gate_b[l], mlstm_norm[l]),
                                (rx, ry, conv_w[l], conv_b[l], lru_gate_w[l], lru_gate_b[l], lru_lam[l]))
        xs = _out_ffn(xs, att, mem, rec, mod[l], norm_gain[l], w_out[l].astype(BF16), w_ffn_in[l].astype(BF16),
                      w_ffn_out[l].astype(BF16), first_tile=int(l == depth - 1))
    return xs
```

```python
import functools

import jax
import jax.numpy as jnp
import numpy as np
from jax import lax
from jax.experimental import pallas as pl
from jax.experimental.pallas import tpu as pltpu

F32 = jnp.float32
BF16 = jnp.bfloat16

D_MODEL = 1024
GRID_W = 64
CTX_LEN = 256
N_MOD = 6
NORM_EPS = 1e-6
ATT_HEADS = 8
ATT_KV_HEADS = 2
ATT_GROUP = ATT_HEADS // ATT_KV_HEADS
HEAD_DIM = 64
WINDOW = 128
ATT_BLOCK = 128
ROPE_BASE = 10000.0
ROPE_PAIRS = HEAD_DIM // 4
ATT_Q = ATT_HEADS * HEAD_DIM
ATT_KV = ATT_KV_HEADS * HEAD_DIM
MLSTM_HEADS = 4
MLSTM_HEAD_DIM = 64
M_W = MLSTM_HEADS * MLSTM_HEAD_DIM
M_GATES = 4 * MLSTM_HEADS
LRU_WIDTH = 256
LRU_BLOCKS = 4
LRU_BW = LRU_WIDTH // LRU_BLOCKS
LRU_C = 8.0
CONV_WIDTH = 4
CONV_LEFT = CONV_WIDTH // 2
D_FF = -(-8 * D_MODEL // (3 * 256)) * 256
LOG2_E = float(np.log2(np.e))
Q_SCALE = HEAD_DIM ** -0.5 * LOG2_E

LANES = 128
SUBLANES = 8
TOKEN_TILE = CTX_LEN
FF_CHUNK = 256
LRU_CHAINS = 8
MOD_ROWS = 16
VMEM_LIMIT = 56 * 1024 * 1024


def _params(sem):
    return pltpu.CompilerParams(dimension_semantics=sem, vmem_limit_bytes=VMEM_LIMIT)


def _dot(a, b):
    return jnp.dot(a, b, preferred_element_type=F32)


def _dot_nt(a, b):
    return lax.dot_general(a, b, (((1,), (1,)), ((), ())), preferred_element_type=F32)


def _dot_exact(a, b):
    return jnp.dot(a, b, preferred_element_type=F32, precision=lax.Precision.HIGHEST)


def _sigmoid(x):
    return 1.0 / (1.0 + jnp.exp(-x))


def _softplus(x):
    return jnp.maximum(x, 0.0) + jnp.log1p(jnp.exp(-jnp.abs(x)))


def _rms(x, g):
    return x * lax.rsqrt(jnp.mean(x * x, axis=-1, keepdims=True) + NORM_EPS) * g


def _mod_kernel(c_ref, w_ref, b_ref, o_ref):
    c = c_ref[...]
    s = (c * _sigmoid(c)).astype(BF16)
    o_ref[...] = _dot(s, w_ref[...].astype(BF16)) + b_ref[...]


def _modulation(cvec, w_ada, b_ada):
    depth, d, n = w_ada.shape
    tn = 1536
    return pl.pallas_call(
        _mod_kernel,
        grid=(depth, n // tn),
        in_specs=[
            pl.BlockSpec((MOD_ROWS, d), lambda l, j: (0, 0)),
            pl.BlockSpec((None, d, tn), lambda l, j: (l, 0, j)),
            pl.BlockSpec((None, 1, tn), lambda l, j: (l, 0, j)),
        ],
        out_specs=pl.BlockSpec((None, MOD_ROWS, tn), lambda l, j: (l, 0, j)),
        out_shape=jax.ShapeDtypeStruct((depth, MOD_ROWS, n), F32),
        compiler_params=_params(("arbitrary", "arbitrary")),
        name="modulation",
    )(cvec, w_ada, b_ada.reshape(depth, 1, n))


_IN_COLS = {}
_col = 0
for _name, _width in (("aq", ATT_Q), ("ak", ATT_KV), ("av", ATT_KV), ("mq", M_W), ("mk", M_W), ("mv", M_W), ("mo", M_W),
                      ("mg", M_GATES), ("rx", LRU_WIDTH), ("ry", LRU_WIDTH)):
    _IN_COLS[_name] = (_col, _col + _width)
    _col += _width
_TOKEN_MAJOR = ("ak", "mq", "mk", "mv", "mo", "rx", "ry")
_FEATURE_MAJOR = ("aq", "av", "mg")


def _split_in_weights(w):
    tok = jnp.concatenate([w[:, _IN_COLS[n][0]:_IN_COLS[n][1]] for n in _TOKEN_MAJOR], axis=1)
    feat = jnp.concatenate([w[:, _IN_COLS[n][0]:_IN_COLS[n][1]] for n in _FEATURE_MAJOR], axis=1).T
    return tok.astype(BF16), feat.astype(BF16)


def _rope_slab(x, cs, sn, first):
    swapped = jnp.where(first, pltpu.roll(x, LANES - ROPE_PAIRS, 1), pltpu.roll(x, ROPE_PAIRS, 1))
    return x * cs + swapped * sn


class _TileWalk:
    def __init__(self, nb, per_row, first_tile=0):
        self.nb, self.per_row, self.first_tile, self.n_tiles = nb, per_row, first_tile, nb * per_row

    def _tile(self, k):
        return k // self.per_row, k % self.per_row + self.first_tile

    def cur(self, k):
        return self._tile(jnp.minimum(k, self.n_tiles - 1))

    def prev(self, k):
        return self._tile(jnp.maximum(k - 1, 0))

    def mod_row(self, bi):
        return jnp.where(bi[1] == 0, self.nb, bi[0]), 0, 0

    def stream_specs(self, streams):
        d = streams[0].shape[-1]
        if len(streams) == 1:
            return [pl.BlockSpec((None, TOKEN_TILE, d), lambda k: (*self.cur(k), 0))]
        return [pl.BlockSpec((None, TOKEN_TILE, d), lambda k: (self.cur(k)[0], 0, 0)),
                pl.BlockSpec((None, TOKEN_TILE, d), lambda k: (self.cur(k)[0], jnp.maximum(self.cur(k)[1] - 1, 0), 0))]

    def stream_tile(self, stream_refs):
        if len(stream_refs) == 1:
            return stream_refs[0][...]
        return jnp.where(self.cur(pl.program_id(0))[1] == 0, stream_refs[0][...], stream_refs[1][...])


def _inproj_kernel(n_streams, *refs):
    (mod_ref, g_ref, w_ref, wt_ref, cs_ref, sn_ref, cst_ref, snt_ref,
     ak_ref, mq_ref, mk_ref, mv_ref, mo_ref, rx_ref, ry_ref, aqt_ref, avt_ref, gr_ref) = refs[n_streams:]
    x = refs[0][...] if n_streams == 1 else jnp.where(pl.program_id(1) == 0, refs[0][...], refs[1][...])
    hb = (_rms(x, g_ref[0:1, :]) * (1.0 + mod_ref[1:2, :]) + mod_ref[0:1, :]).astype(BF16)
    pt = _dot_nt(wt_ref[...], hb)
    cst = cst_ref[...]
    snt = snt_ref[...]
    rp = ROPE_PAIRS
    for hd in range(ATT_HEADS):
        xs = pt[hd * HEAD_DIM:(hd + 1) * HEAD_DIM, :]
        swapped = jnp.concatenate([xs[rp:2 * rp], xs[0:rp], xs[3 * rp:4 * rp], xs[2 * rp:3 * rp]], axis=0)
        roped = (xs * cst + swapped * snt) * Q_SCALE
        for qb in range(TOKEN_TILE // ATT_BLOCK):
            aqt_ref[qb, hd * HEAD_DIM:(hd + 1) * HEAD_DIM, :] = roped[:, qb * ATT_BLOCK:(qb + 1) * ATT_BLOCK]
    for qb in range(TOKEN_TILE // ATT_BLOCK):
        avt_ref[qb] = pt[ATT_Q:ATT_Q + ATT_KV, qb * ATT_BLOCK:(qb + 1) * ATT_BLOCK]
    gr_ref[...] = pt[ATT_Q + ATT_KV:, :]
    lane = lax.broadcasted_iota(jnp.int32, (hb.shape[0], LANES), 1)
    first = (lane & (2 * ROPE_PAIRS - 1)) < ROPE_PAIRS
    ak_ref[...] = _rope_slab(_dot(hb, w_ref[:, 0:ATT_KV]), cs_ref[...], sn_ref[...], first)
    col = ATT_KV
    for ref in (mq_ref, mk_ref, mv_ref, mo_ref, rx_ref, ry_ref):
        n = ref.shape[-1]
        ref[...] = _dot(hb, w_ref[:, col:col + n])
        col += n


def _in_projection(xs, mod, gain, w_tok, w_feat, rope):
    streams = xs if isinstance(xs, tuple) else (xs,)
    nb = streams[0].shape[0]
    t = sum(s.shape[1] for s in streams)
    nt = t // TOKEN_TILE
    d = streams[0].shape[-1]
    tile = lambda n: pl.BlockSpec((None, TOKEN_TILE, n), lambda b, i: (b, i, 0))
    const = lambda a: pl.BlockSpec(a.shape, lambda b, i: (0,) * a.ndim)
    if len(streams) == 1:
        stream_specs = [tile(d)]
    else:
        stream_specs = [pl.BlockSpec((None, TOKEN_TILE, d), lambda b, i: (b, 0, 0)),
                        pl.BlockSpec((None, TOKEN_TILE, d), lambda b, i: (b, jnp.maximum(i - 1, 0), 0))]
    width = lambda n: _IN_COLS[n][1] - _IN_COLS[n][0]
    out_shapes = [jax.ShapeDtypeStruct((nb, t, width(n)), F32) for n in _TOKEN_MAJOR]
    per_tile = TOKEN_TILE // ATT_BLOCK
    out_shapes += [jax.ShapeDtypeStruct((nb, nt * per_tile, width(n), ATT_BLOCK), F32) for n in ("aq", "av")]
    out_shapes += [jax.ShapeDtypeStruct((nb, nt, M_GATES, TOKEN_TILE), F32)]
    out_specs = [tile(width(n)) for n in _TOKEN_MAJOR]
    out_specs += [pl.BlockSpec((None, per_tile, width(n), ATT_BLOCK), lambda b, i: (b, i, 0, 0)) for n in ("aq", "av")]
    out_specs += [pl.BlockSpec((None, None, M_GATES, TOKEN_TILE), lambda b, i: (b, i, 0, 0))]
    rope_cs, rope_sn, rope_cst, rope_snt = rope
    return pl.pallas_call(
        functools.partial(_inproj_kernel, len(streams)),
        grid=(nb, nt),
        in_specs=stream_specs + [
            pl.BlockSpec((None, N_MOD, d), lambda b, i: (jnp.where(i == 0, nb, b), 0, 0)),
            const(gain), const(w_tok), const(w_feat),
            pl.BlockSpec((TOKEN_TILE, LANES), lambda b, i: (i, 0)),
            pl.BlockSpec((TOKEN_TILE, LANES), lambda b, i: (i, 0)),
            pl.BlockSpec((HEAD_DIM, TOKEN_TILE), lambda b, i: (0, i)),
            pl.BlockSpec((HEAD_DIM, TOKEN_TILE), lambda b, i: (0, i)),
        ],
        out_specs=out_specs,
        out_shape=out_shapes,
        compiler_params=_params(("parallel", "arbitrary")),
        name="in_projection",
    )(*streams, mod, gain, w_tok, w_feat, rope_cs, rope_sn, rope_cst, rope_snt)


def _attn_block(j, nblk, sink_ref, qt_ref, kp_ref, kc_ref, kn_ref, kx_ref, vp_ref, vc_ref, vn_ref, vx_ref, o_ref):
    ctx_blocks = CTX_LEN // ATT_BLOCK
    nq = ATT_BLOCK
    cols = ATT_GROUP * nq
    c = lax.broadcasted_iota(jnp.int32, (ATT_BLOCK, cols), 0)
    r = lax.broadcasted_iota(jnp.int32, (ATT_BLOCK, cols), 1) & (nq - 1)
    m_prev = (c >= r) & (j >= ctx_blocks + 1)
    m_next = (c <= r) & (j >= ctx_blocks) & (j <= nblk - 2)
    is_lat = j >= ctx_blocks
    col = lax.broadcasted_iota(jnp.int32, (1, cols), 1)
    neg = -jnp.inf
    kp = kp_ref[...].astype(BF16)
    kc = kc_ref[...].astype(BF16)
    kn = kn_ref[...].astype(BF16)
    kx = kx_ref[...].astype(BF16)
    zeros = jnp.zeros((HEAD_DIM, cols), BF16)
    scores = []
    for kh in range(ATT_KV_HEADS):
        hd0 = kh * ATT_GROUP
        qg = jnp.concatenate([qt_ref[(hd0 + g) * HEAD_DIM:(hd0 + g + 1) * HEAD_DIM, :] for g in range(ATT_GROUP)],
                             axis=1).astype(BF16)
        rhs = jnp.concatenate([qg, zeros] if kh == 0 else [zeros, qg], axis=0)
        s_x = _dot(kx, rhs)
        scores.append((jnp.where(m_prev, _dot(kp, rhs), neg), jnp.where(is_lat, _dot(kc, rhs), neg),
                       jnp.where(m_next, _dot(kn, rhs), neg), s_x[:ATT_BLOCK], s_x[ATT_BLOCK:]))
    for kh in range(ATT_KV_HEADS):
        hd0 = kh * ATT_GROUP
        sink = jnp.full((1, cols), sink_ref[hd0], F32)
        for g in range(1, ATT_GROUP):
            sink = jnp.where(col >= g * nq, sink_ref[hd0 + g], sink)
        sink = sink * LOG2_E
        s_p, s_c, s_n, s_x0, s_x1 = scores[kh]
        m_el = jnp.maximum(jnp.maximum(jnp.maximum(s_p, s_c), jnp.maximum(s_n, s_x0)), s_x1)
        m = jnp.maximum(jnp.max(m_el, axis=0, keepdims=True), sink)
        p_p = jnp.exp2(s_p - m)
        p_c = jnp.exp2(s_c - m)
        p_n = jnp.exp2(s_n - m)
        p_x0 = jnp.exp2(s_x0 - m)
        p_x1 = jnp.exp2(s_x1 - m)
        den = jnp.sum((p_p + p_c) + (p_n + p_x0) + p_x1, axis=0, keepdims=True) + jnp.exp2(sink - m)
        vs = slice(kh * HEAD_DIM, (kh + 1) * HEAD_DIM)
        o = (_dot(vp_ref[vs, :].astype(BF16), p_p.astype(BF16)) + _dot(vc_ref[vs, :].astype(BF16), p_c.astype(BF16))
             + _dot(vn_ref[vs, :].astype(BF16), p_n.astype(BF16))
             + _dot(vx_ref[0, vs, :].astype(BF16), p_x0.astype(BF16)) + _dot(vx_ref[1, vs, :].astype(BF16), p_x1.astype(BF16)))
        o = o / den
        for pair in range(ATT_GROUP // 2):
            two = jnp.concatenate([o[:, (2 * pair) * nq:(2 * pair + 1) * nq], o[:, (2 * pair + 1) * nq:(2 * pair + 2) * nq]],
                                  axis=0)
            lo = (hd0 + 2 * pair) * HEAD_DIM
            o_ref[:, lo:lo + 2 * HEAD_DIM] = two.T


def _attention_specs(aqt, ak, avt, sink, qblock):
    nb, t, _ = ak.shape
    nblk = t // ATT_BLOCK
    prev = lambda q: jnp.maximum(q - 1, 0)
    cur = lambda q: q
    nxt = lambda q: jnp.minimum(q + 1, nblk - 1)
    kb = lambda f: pl.BlockSpec((None, ATT_BLOCK, ATT_KV), lambda b, *g: (b, f(qblock(*g)), 0))
    vb = lambda f: pl.BlockSpec((None, None, ATT_KV, ATT_BLOCK), lambda b, *g: (b, f(qblock(*g)), 0, 0))
    in_specs = [
        pl.BlockSpec(memory_space=pltpu.SMEM),
        pl.BlockSpec((None, None, ATT_Q, ATT_BLOCK), lambda b, *g: (b, qblock(*g), 0, 0)),
        kb(prev), kb(cur), kb(nxt), pl.BlockSpec((None, CTX_LEN, ATT_KV), lambda b, *g: (b, 0, 0)),
        vb(prev), vb(cur), vb(nxt),
        pl.BlockSpec((None, CTX_LEN // ATT_BLOCK, ATT_KV, ATT_BLOCK), lambda b, *g: (b, 0, 0, 0)),
    ]
    args = [sink, aqt, ak, ak, ak, ak, avt, avt, avt, avt]
    out_spec = pl.BlockSpec((None, ATT_BLOCK, ATT_Q), lambda b, *g: (b, qblock(*g), 0))
    return in_specs, args, out_spec, jax.ShapeDtypeStruct((nb, t, ATT_Q), F32)


def _seq_block(d, j, nblk):
    return jnp.where((d == 0) | (j == 0), j, nblk - j)


def _lane_scan(x, op, fill, rev):
    n = x.shape[-1]
    lane = lax.broadcasted_iota(jnp.int32, x.shape, 1)
    step = 1
    while step < n:
        if rev:
            shifted = jnp.where(lane < n - step, pltpu.roll(x, n - step, 1), fill)
        else:
            shifted = jnp.where(lane >= step, pltpu.roll(x, step, 1), fill)
        x = op(x, shifted)
        step *= 2
    return x


def _split3(x):
    hi = x.astype(BF16).astype(F32)
    mid = (x - hi).astype(BF16).astype(F32)
    lo = x - hi - mid
    return hi, mid, lo


def _mlstm_tables():
    H, dh = MLSTM_HEADS, MLSTM_HEAD_DIM
    n_pairs = H // 2
    n_slabs = H + 3 * n_pairs
    sel = np.zeros((LANES, n_slabs * LANES), np.float32)
    for part in range(3):
        base = part * 4 * H
        for h in range(H):
            sel[base + h, h * LANES:(h + 1) * LANES] = 1.0
        for qty in range(3):
            for p in range(n_pairs):
                slab = H + qty * n_pairs + p
                for half in range(2):
                    row = base + H * qty + 2 * p + half
                    sel[row, slab * LANES + half * dh:slab * LANES + (half + 1) * dh] = 1.0
    ones_bd = np.zeros((2 * TOKEN_TILE, LANES), np.float32)
    ones_bd[:TOKEN_TILE, :dh] = 1.0
    ones_bd[TOKEN_TILE:, dh:] = 1.0
    avg = np.zeros((3 * LANES, LANES), np.float32)
    for part in range(3):
        for half in range(2):
            avg[part * LANES + half * dh:part * LANES + (half + 1) * dh, half * dh:(half + 1) * dh] = 1.0 / dh
    return jnp.asarray(sel, BF16), jnp.asarray(ones_bd, BF16), jnp.asarray(avg, BF16)


def _mlstm_block_terms(rev, g, tri_ref):
    L = TOKEN_TILE
    H = MLSTM_HEADS
    ioff = 2 * H if rev else 0
    li = g[ioff:ioff + H, :]
    lf = -_softplus(-g[ioff + H:ioff + 2 * H, :])
    parts = _dot(jnp.concatenate(_split3(lf), axis=0).astype(BF16), tri_ref[...])
    b = parts[0:H] + parts[H:2 * H] + parts[2 * H:3 * H]
    r = li - b
    cm = _lane_scan(r, jnp.maximum, -jnp.inf, rev)
    last = 0 if rev else L - 1
    g_tot = b[:, last:last + 1]
    a = g_tot - b + li
    m_loc = jnp.max(a, axis=1, keepdims=True)
    w = jnp.exp(a - m_loc)
    return r, cm, b, w, g_tot, m_loc


def _mlstm_stash_terms(terms, sel_ref, cols_ref, r_ref, gm_ref, slot):
    L = TOKEN_TILE
    H, dh = MLSTM_HEADS, MLSTM_HEAD_DIM
    r, cm, b, w, g_tot, m_loc = terms
    stacked = jnp.concatenate([cm, b, w, jnp.zeros((H, L), F32)], axis=0)
    hi, mid, lo = _split3(stacked)
    pad = jnp.zeros((LANES - 3 * 4 * H, L), F32)
    cols_ref[slot] = _dot(jnp.concatenate([hi, mid, lo, pad], axis=0).T.astype(BF16), sel_ref[...])
    r_ref[slot] = jnp.concatenate([r, jnp.zeros((SUBLANES - H, L), F32)], axis=0)
    lane_row = lax.broadcasted_iota(jnp.int32, (1, LANES), 1)

    def layout(col):
        rows = [jnp.broadcast_to(col, (H, LANES))]
        rows += [jnp.where(lane_row < dh, col[2 * p:2 * p + 1, :], col[2 * p + 1:2 * p + 2, :]) for p in range(H // 2)]
        rows += [jnp.zeros((SUBLANES - H - H // 2, LANES), F32)]
        return jnp.concatenate(rows, axis=0)

    gm_ref[slot] = jnp.concatenate([layout(g_tot), layout(m_loc)], axis=0)


def _mlstm_kernel(q_ref, k_ref, v_ref, o_ref, gr_ref, gnext_ref, gb_ref, ng_ref, tri_ref, sel_ref, ones_ref, avg_ref, out_ref,
                  hf_ref, c_ref, n_ref, m_ref, cols_ref, r_ref, gm_ref, rev, init=False):
    d = pl.program_id(1)
    j = pl.program_id(2)
    nblk = pl.num_programs(2)
    L = TOKEN_TILE
    H, dh = MLSTM_HEADS, MLSTM_HEAD_DIM
    n_pairs = H // 2
    row0 = pl.multiple_of(_seq_block(d, j, nblk) * L, L)
    slot = j & 1

    if init:
        c_ref[...] = jnp.zeros_like(c_ref)
        n_ref[...] = jnp.zeros_like(n_ref)
        m_ref[...] = jnp.zeros_like(m_ref)
        _mlstm_stash_terms(_mlstm_block_terms(rev, gr_ref[...] + gb_ref[...], tri_ref), sel_ref, cols_ref, r_ref, gm_ref, 0)
        return

    def body(rev):
        next_terms = _mlstm_block_terms(rev, gnext_ref[...] + gb_ref[...], tri_ref)

        r = r_ref[slot]
        gm = gm_ref[slot]
        slab = lambda i: cols_ref[slot, :, i * LANES:(i + 1) * LANES]
        m0 = m_ref[...]
        g_tot, m_loc = gm[0:SUBLANES], gm[SUBLANES:]
        m_new = jnp.maximum(g_tot + m0, m_loc)
        f_prev = jnp.exp(g_tot + m0 - m_new)
        f_loc = jnp.exp(m_loc - m_new)
        m_ref[...] = m_new
        lane = lax.broadcasted_iota(jnp.int32, (L, LANES), 1)
        first_head = lane < dh
        sq = (LANES, LANES)
        tri_vis = ((lax.broadcasted_iota(jnp.int32, sq, 1) >= lax.broadcasted_iota(jnp.int32, sq, 0)) if rev
                   else (lax.broadcasted_iota(jnp.int32, sq, 1) <= lax.broadcasted_iota(jnp.int32, sq, 0)))
        kt = (k_ref[...] * (dh ** -0.5)).T.astype(BF16)
        zeros_half = jnp.zeros((dh, L), BF16)
        bd_mask = ((lax.broadcasted_iota(jnp.int32, (LANES, LANES), 0) < dh)
                   == (lax.broadcasted_iota(jnp.int32, (LANES, LANES), 1) < dh))
        for p in range(n_pairs):
            ps = slice(p * LANES, (p + 1) * LANES)
            qb = q_ref[:, ps].astype(BF16)
            v = v_ref[:, ps]
            kt_pair = kt[ps, :]
            sm = []
            for half in range(2):
                h = 2 * p + half
                kt_h = kt[h * dh:(h + 1) * dh, :]
                kt_masked = jnp.concatenate([kt_h, zeros_half] if half == 0 else [zeros_half, kt_h], axis=0)
                s = _dot(qb, kt_masked)
                m_b = jnp.maximum(slab(h), m0[h:h + 1, :])
                for lt in range(L // LANES):
                    quads = []
                    for qt in range(L // LANES):
                        rows = slice(qt * LANES, (qt + 1) * LANES)
                        if (lt < qt) if rev else (lt > qt):
                            quads.append(jnp.zeros((LANES, LANES), BF16))
                            continue
                        expo = r[h:h + 1, lt * LANES:(lt + 1) * LANES] - m_b[rows]
                        if lt == qt:
                            expo = jnp.where(tri_vis, expo, -jnp.inf)
                        quads.append((s[rows, lt * LANES:(lt + 1) * LANES] * jnp.exp(expo)).astype(BF16))
                    sm.append(jnp.concatenate(quads, axis=0))
            sm = jnp.concatenate(sm, axis=1)
            v_bd = jnp.concatenate([jnp.where(first_head, v, 0.0), jnp.where(first_head, 0.0, v)], axis=0).astype(BF16)
            m0_pair = m0[H + p:H + p + 1, :]
            m_pair = jnp.maximum(slab(H + p), m0_pair)
            f_pair = jnp.exp(m0_pair - m_pair)
            num = _dot(sm, v_bd) + f_pair * _dot(qb, c_ref[p].astype(BF16))
            den = _dot(sm, ones_ref[...]) + f_pair * _dot(qb, n_ref[p].astype(BF16))
            hh = num / jnp.maximum(jnp.abs(den), jnp.exp(-(slab(H + n_pairs + p) + m_pair)))
            w_pair = slab(H + 2 * n_pairs + p)
            c_loc = jnp.where(bd_mask, _dot(kt_pair, (w_pair * v).astype(BF16)), 0.0)
            n_loc = jnp.where(bd_mask, _dot(kt_pair, w_pair.astype(BF16)), 0.0)
            c_ref[p] = f_prev[H + p:H + p + 1, :] * c_ref[p] + f_loc[H + p:H + p + 1, :] * c_loc
            n_ref[p] = f_prev[H + p:H + p + 1, :] * n_ref[p] + f_loc[H + p:H + p + 1, :] * n_loc
            if not rev:
                hf_ref[pl.ds(row0, L), ps] = hh
            else:
                ht = hf_ref[pl.ds(row0, L), ps] + hh
                mu = _dot(jnp.concatenate(_split3(ht), axis=1).astype(BF16), avg_ref[...])
                cen = ht - mu
                var = _dot(jnp.concatenate(_split3(cen * cen), axis=1).astype(BF16), avg_ref[...])
                y = cen * lax.rsqrt(var + NORM_EPS) * ng_ref[:, ps]
                out_ref[:, ps] = _sigmoid(o_ref[:, ps]) * y
        _mlstm_stash_terms(next_terms, sel_ref, cols_ref, r_ref, gm_ref, 1 - slot)

    body(rev)


def _mlstm_specs(mq, mk, mv, mo, grow, gate_b, norm_g):
    nb, t, w = mq.shape
    nblk = t // TOKEN_TILE
    blk = lambda b, d, j: (b, _seq_block(d, j, nblk), 0)
    tile = pl.BlockSpec((None, TOKEN_TILE, w), blk)
    gb = gate_b.reshape(M_GATES, 1)
    ng = norm_g.reshape(1, w)
    sel, ones_bd, avg = _mlstm_tables()
    upper = np.triu(np.ones((TOKEN_TILE, TOKEN_TILE), np.float32))
    tri = jnp.asarray(np.stack([upper, upper.T]), BF16)
    const = lambda a: pl.BlockSpec(a.shape, lambda b, d, j: (0,) * a.ndim)
    n_pairs = MLSTM_HEADS // 2
    in_specs = [
        tile, tile, tile, tile,
        pl.BlockSpec((None, None, M_GATES, TOKEN_TILE), lambda b, d, j: (b, _seq_block(d, j, nblk), 0, 0)),
        pl.BlockSpec((None, None, M_GATES, TOKEN_TILE),
                     lambda b, d, j: (b, _seq_block(d, jnp.minimum(j + 1, nblk - 1), nblk), 0, 0)),
        const(gb), const(ng),
        pl.BlockSpec((None, TOKEN_TILE, TOKEN_TILE), lambda b, d, j: (d, 0, 0)),
        const(sel), const(ones_bd), const(avg),
    ]
    args = [mq, mk, mv, mo, grow, grow, gb, ng, tri, sel, ones_bd, avg]
    out_spec = pl.BlockSpec((None, TOKEN_TILE, w), lambda b, d, j: (b, jnp.where(d == 0, 0, _seq_block(d, j, nblk)), 0))
    scratch = [
        pltpu.VMEM((t, w), F32),
        pltpu.VMEM((n_pairs, LANES, LANES), F32),
        pltpu.VMEM((n_pairs, LANES, LANES), F32),
        pltpu.VMEM((SUBLANES, LANES), F32),
        pltpu.VMEM((2, TOKEN_TILE, sel.shape[1]), F32),
        pltpu.VMEM((2, SUBLANES, TOKEN_TILE), F32),
        pltpu.VMEM((2, 2 * SUBLANES, LANES), F32),
    ]
    return in_specs, args, out_spec, jax.ShapeDtypeStruct((nb, t, w), F32), scratch


def _lru_kernel(x_ref, xp_ref, xn_ref, y_ref, cw_ref, cb_ref, gw_ref, gb_ref, lam_ref, out_ref,
                hf_ref, xe_ref, carry_ref, a_ref, u_ref, hb_ref, hl_ref, al_ref, rev, init=False):
    d = pl.program_id(1)
    j = pl.program_id(2)
    nblk = pl.num_programs(2)
    L = TOKEN_TILE
    pos = _seq_block(d, j, nblk)
    row0 = pl.multiple_of(pos * L, L)

    if init:
        carry_ref[...] = jnp.zeros_like(carry_ref)
        return

    has_prev = pos >= 2
    has_next = (pos >= 1) & (pos <= nblk - 2)
    xe_ref[0:SUBLANES, :] = jnp.where(has_prev, xp_ref[...], 0.0)
    xe_ref[SUBLANES:SUBLANES + L, :] = x_ref[...]
    xe_ref[SUBLANES + L:, :] = jnp.where(has_next, xn_ref[...], 0.0)
    seq = cb_ref[...]
    for tap in range(CONV_WIDTH):
        off = SUBLANES - CONV_LEFT + tap
        seq = seq + cw_ref[tap:tap + 1, :] * xe_ref[off:off + L, :]
    sb = seq.astype(BF16)
    r = 0.5 + 0.5 * jnp.tanh(0.5 * (_dot(sb, gw_ref[0]) + gb_ref[0:1, :]))
    i = 0.5 + 0.5 * jnp.tanh(0.5 * (_dot(sb, gw_ref[1]) + gb_ref[1:2, :]))
    log_a = (-LRU_C * _softplus(-lam_ref[...])) * r
    a0 = jnp.exp(log_a)
    th = jnp.tanh(log_a)
    u0 = jnp.sqrt(-2.0 * th) * lax.rsqrt(1.0 - th) * (i * seq)
    n_lt = LRU_WIDTH // LANES
    for lt in range(n_lt):
        a_ref[lt] = a0[:, lt * LANES:(lt + 1) * LANES]
        u_ref[lt] = u0[:, lt * LANES:(lt + 1) * LANES]
    seg = L // (LRU_CHAINS * SUBLANES)
    sub = lax.broadcasted_iota(jnp.int32, (SUBLANES, LRU_WIDTH), 0)
    rows = lambda c, i: pl.ds(c * seg * SUBLANES + i, SUBLANES, stride=seg)
    strided = lambda ref, c, i: jnp.concatenate([ref[lt, rows(c, i), :] for lt in range(n_lt)], axis=1)

    def scan(rev):
        steps = range(seg - 1, -1, -1) if rev else range(seg)
        chains = range(LRU_CHAINS - 1, -1, -1) if rev else range(LRU_CHAINS)
        edge = SUBLANES - 1 if rev else 0
        shift = lambda x, n: pltpu.roll(x, SUBLANES - n if rev else n, 0)
        ends, prods = {}, {}
        for c in chains:
            h = jnp.zeros((SUBLANES, LRU_WIDTH), F32)
            ac = jnp.ones((SUBLANES, LRU_WIDTH), F32)
            for i in steps:
                ai = strided(a_ref, c, i)
                h = ai * h + strided(u_ref, c, i)
                ac = ai * ac
                hl_ref[c * seg + i] = h
                al_ref[c * seg + i] = ac
            step = 1
            while step < SUBLANES:
                ok = (sub < SUBLANES - step) if rev else (sub >= step)
                h = jnp.where(ok, ac * shift(h, step) + h, h)
                ac = jnp.where(ok, ac * shift(ac, step), ac)
                step *= 2
            ends[c], prods[c] = h, ac
        carry = carry_ref[...]
        for c in chains:
            true_ends = ends[c] + prods[c] * carry
            carry_in = jnp.where(sub == edge, carry, shift(true_ends, 1))
            carry = true_ends[SUBLANES - 1 - edge:SUBLANES - edge, :]
            for i in range(seg):
                hi = hl_ref[c * seg + i] + al_ref[c * seg + i] * carry_in
                for lt in range(n_lt):
                    hb_ref[lt, rows(c, i), :] = hi[:, lt * LANES:(lt + 1) * LANES]
        carry_ref[...] = carry
        return jnp.concatenate([hb_ref[lt] for lt in range(n_lt)], axis=1)

    def forward():
        hf_ref[pl.ds(row0, L), :] = scan(False)

    def backward():
        h = hf_ref[pl.ds(row0, L), :] + scan(True)
        y = y_ref[...]
        gelu = 0.5 * y * (1.0 + jnp.tanh(np.sqrt(2.0 / np.pi).astype(np.float32) * (y + 0.044715 * (y * y * y))))
        out_ref[...] = h * gelu

    backward() if rev else forward()


def _rglru_specs(rx, ry, conv_w, conv_b, gate_w, gate_b, lam):
    nb, t, w = rx.shape
    nblk = t // TOKEN_TILE
    per_tile = TOKEN_TILE // SUBLANES
    n8 = t // SUBLANES
    blk = lambda b, d, j: (b, _seq_block(d, j, nblk), 0)
    tile = pl.BlockSpec((None, TOKEN_TILE, w), blk)
    halo = lambda f: pl.BlockSpec((None, SUBLANES, w), lambda b, d, j: (b, f(_seq_block(d, j, nblk)), 0))
    prev8 = lambda p: jnp.maximum(p * per_tile - 1, 0)
    next8 = lambda p: jnp.minimum((p + 1) * per_tile, n8 - 1)
    eye = jnp.eye(LRU_BLOCKS, dtype=gate_w.dtype)
    gw = jnp.einsum('dgnij,nm->dgnimj', gate_w, eye).reshape(2, 2, w, w).astype(BF16)
    cb = conv_b.reshape(1, w)
    in_specs = [
        tile, halo(prev8), halo(next8), tile,
        pl.BlockSpec(conv_w.shape, lambda b, d, j: (0, 0)),
        pl.BlockSpec(cb.shape, lambda b, d, j: (0, 0)),
        pl.BlockSpec((None, 2, w, w), lambda b, d, j: (d, 0, 0, 0)),
        pl.BlockSpec((None, 2, w), lambda b, d, j: (d, 0, 0)),
        pl.BlockSpec((None, 1, w), lambda b, d, j: (d, 0, 0)),
    ]
    args = [rx, rx, rx, ry, conv_w, cb, gw, gate_b, lam.reshape(2, 1, w)]
    out_spec = pl.BlockSpec((None, TOKEN_TILE, w), lambda b, d, j: (b, jnp.where(d == 0, 0, _seq_block(d, j, nblk)), 0))
    scratch = [
        pltpu.VMEM((t, w), F32),
        pltpu.VMEM((TOKEN_TILE + 2 * SUBLANES, w), F32),
        pltpu.VMEM((1, w), F32),
        pltpu.VMEM((w // LANES, TOKEN_TILE, LANES), F32),
        pltpu.VMEM((w // LANES, TOKEN_TILE, LANES), F32),
        pltpu.VMEM((w // LANES, TOKEN_TILE, LANES), F32),
        pltpu.VMEM((TOKEN_TILE // SUBLANES, SUBLANES, w), F32),
        pltpu.VMEM((TOKEN_TILE // SUBLANES, SUBLANES, w), F32),
    ]
    return in_specs, args, out_spec, jax.ShapeDtypeStruct((nb, t, w), F32), scratch


def _mixers_kernel(n_att, n_mem, n_rec, n_mem_scratch, *refs):
    d = pl.program_id(1)
    j = pl.program_id(2)
    nblk = pl.num_programs(2)
    n_in = n_att + n_mem + n_rec
    att_in, mem_in, rec_in = refs[:n_att], refs[n_att:n_att + n_mem], refs[n_att + n_mem:n_in]
    att_out, mem_out, rec_out = refs[n_in:n_in + 3]
    mem_scratch = refs[n_in + 3:n_in + 3 + n_mem_scratch]
    rec_scratch = refs[n_in + 3 + n_mem_scratch:]
    q_per_tile = TOKEN_TILE // ATT_BLOCK

    def body(rev):
        @pl.when(j == 0)
        def _():
            _mlstm_kernel(*mem_in, mem_out, *mem_scratch, rev=rev, init=True)
            _lru_kernel(*rec_in, rec_out, *rec_scratch, rev=rev, init=True)

        _attn_block(q_per_tile * _seq_block(d, j, nblk) + int(rev), q_per_tile * nblk, *att_in, att_out)
        _mlstm_kernel(*mem_in, mem_out, *mem_scratch, rev=rev)
        _lru_kernel(*rec_in, rec_out, *rec_scratch, rev=rev)

    pl.when(d == 0)(functools.partial(body, False))
    pl.when(d == 1)(functools.partial(body, True))


def _mixers(aqt, ak, avt, sink, mlstm_args, rglru_args):
    nb, t, _ = ak.shape
    nblk = t // TOKEN_TILE
    assert TOKEN_TILE == 2 * ATT_BLOCK
    qblock = lambda d, j: (TOKEN_TILE // ATT_BLOCK) * _seq_block(d, j, nblk) + d
    att_specs, att_args, att_out, att_shape = _attention_specs(aqt, ak, avt, sink, qblock)
    mem_specs, mem_args, mem_out, mem_shape, mem_scratch = _mlstm_specs(*mlstm_args)
    rec_specs, rec_args, rec_out, rec_shape, rec_scratch = _rglru_specs(*rglru_args)
    return pl.pallas_call(
        functools.partial(_mixers_kernel, len(att_args), len(mem_args), len(rec_args), len(mem_scratch)),
        grid=(nb, 2, nblk),
        in_specs=att_specs + mem_specs + rec_specs,
        out_specs=[att_out, mem_out, rec_out],
        out_shape=[att_shape, mem_shape, rec_shape],
        scratch_shapes=mem_scratch + rec_scratch,
        compiler_params=_params(("parallel", "arbitrary", "arbitrary")),
        name="mixers",
    )(*att_args, *mem_args, *rec_args)


def _outffn_kernel(n_streams, walk, *refs):
    k = pl.program_id(0)
    streams = refs[:n_streams]
    (att_ref, mem_ref, rec_ref, mod_ref, modp_ref, g_ref, wo_ref, wi_ref, wd_ref, out_ref,
     act_ref, h0_ref, h1_ref, x0_ref, x1_ref) = refs[n_streams:]

    @pl.when(k == 0)
    def _():
        h1_ref[...] = jnp.zeros_like(h1_ref)
        x1_ref[...] = jnp.zeros_like(x1_ref)

    def body(h_prev, x_prev, h_next, x_next):
        x = walk.stream_tile(streams)
        mix = (_dot(att_ref[...].astype(BF16), wo_ref[0:ATT_Q, :])
               + _dot(mem_ref[...].astype(BF16), wo_ref[ATT_Q:ATT_Q + M_W, :])
               + _dot(rec_ref[...].astype(BF16), wo_ref[ATT_Q + M_W:, :]))
        x1 = x + mod_ref[2:3, :] * _rms(mix, g_ref[1:2, :])
        x_next[...] = x1
        h_next[...] = (_rms(x1, g_ref[2:3, :]) * (1.0 + mod_ref[4:5, :]) + mod_ref[3:4, :]).astype(BF16)
        h = h_prev[...]
        for c0 in range(0, D_FF, FF_CHUNK):
            gate = _dot(h, wi_ref[:, c0:c0 + FF_CHUNK])
            up = _dot(h, wi_ref[:, D_FF + c0:D_FF + c0 + FF_CHUNK])
            half = 0.5 * gate
            act_ref[:, c0:c0 + FF_CHUNK] = ((half + half * jnp.tanh(half)) * up).astype(BF16)
        f = _dot(act_ref[...], wd_ref[...])
        out_ref[...] = x_prev[...] + modp_ref[5:6, :] * _rms(f, g_ref[3:4, :])

    pl.when((k & 1) == 0)(functools.partial(body, h1_ref, x1_ref, h0_ref, x0_ref))
    pl.when((k & 1) == 1)(functools.partial(body, h0_ref, x0_ref, h1_ref, x1_ref))


def _out_ffn(xs, att, mem, rec, mod, gain, w_out, w_ffn_in, w_down, first_tile=0):
    streams = xs if isinstance(xs, tuple) else (xs,)
    assert len(streams) == 1 or first_tile == 0
    nb, _, d = streams[0].shape
    t = sum(s.shape[1] for s in streams)
    per_row = t // TOKEN_TILE - first_tile
    walk = _TileWalk(nb, per_row, first_tile)
    cur, prev = walk.cur, walk.prev
    tile = lambda n: pl.BlockSpec((None, TOKEN_TILE, n), lambda k: (*cur(k), 0))
    const = lambda a: pl.BlockSpec(a.shape, lambda k: (0,) * a.ndim, pipeline_mode=pl.Buffered(1))
    return pl.pallas_call(
        functools.partial(_outffn_kernel, len(streams), walk),
        grid=(walk.n_tiles + 1,),
        in_specs=walk.stream_specs(streams) + [
            tile(ATT_Q), tile(M_W), tile(LRU_WIDTH),
            pl.BlockSpec((None, N_MOD, d), lambda k: walk.mod_row(cur(k))),
            pl.BlockSpec((None, N_MOD, d), lambda k: walk.mod_row(prev(k))),
            const(gain), const(w_out), const(w_ffn_in), const(w_down),
        ],
        out_specs=pl.BlockSpec((None, TOKEN_TILE, d), lambda k: (prev(k)[0], prev(k)[1] - first_tile, 0)),
        out_shape=jax.ShapeDtypeStruct((nb, per_row * TOKEN_TILE, d), F32),
        scratch_shapes=[pltpu.VMEM((TOKEN_TILE, D_FF), BF16),
                        pltpu.VMEM((TOKEN_TILE, d), BF16), pltpu.VMEM((TOKEN_TILE, d), BF16),
                        pltpu.VMEM((TOKEN_TILE, d), F32), pltpu.VMEM((TOKEN_TILE, d), F32)],
        compiler_params=_params(("arbitrary",)),
        name="out_ffn",
    )(*streams, att, mem, rec, mod, mod, gain, w_out, w_ffn_in, w_down)


def _rope_tables(n_lat):
    t = jnp.arange(n_lat)
    row = (t // GRID_W).astype(F32)
    col = (t % GRID_W).astype(F32)
    freqs = ROPE_BASE ** (-jnp.arange(ROPE_PAIRS, dtype=F32) / ROPE_PAIRS)
    ang_r = row[:, None] * freqs
    ang_c = col[:, None] * freqs
    cs = jnp.concatenate([jnp.cos(ang_r), jnp.cos(ang_r), jnp.cos(ang_c), jnp.cos(ang_c)], axis=-1)
    sn = jnp.concatenate([-jnp.sin(ang_r), jnp.sin(ang_r), -jnp.sin(ang_c), jnp.sin(ang_c)], axis=-1)
    cs = jnp.concatenate([jnp.ones((CTX_LEN, HEAD_DIM), F32), cs], axis=0)
    sn = jnp.concatenate([jnp.zeros((CTX_LEN, HEAD_DIM), F32), sn], axis=0)
    return jnp.tile(cs, (1, LANES // HEAD_DIM)), jnp.tile(sn, (1, LANES // HEAD_DIM)), cs.T, sn.T


def kernel(x, c, ctx, c_ctx, w_ada, b_ada, norm_gain, w_in, w_out, attn_sink, mlstm_gate_b, mlstm_norm, conv_w, conv_b,
           lru_gate_w, lru_gate_b, lru_lam, w_ffn_in, w_ffn_out):
    nb, n_lat, d = x.shape
    depth = w_ada.shape[0]
    assert ctx.shape[1] == CTX_LEN and n_lat % TOKEN_TILE == 0 and nb < MOD_ROWS
    cvec = jnp.concatenate([c, c_ctx[None, :], jnp.zeros((MOD_ROWS - nb - 1, d), F32)], axis=0)
    mod = _modulation(cvec, w_ada, b_ada).reshape(depth, MOD_ROWS, N_MOD, d)
    rope = _rope_tables(n_lat)
    xs = (ctx, x) if depth > 1 else jnp.concatenate([ctx, x], axis=1)
    for l in range(depth):
        w_tok, w_feat = _split_in_weights(w_in[l])
        ak, mq, mk, mv, mo, rx, ry, aqt, avt, grow = _in_projection(xs, mod[l], norm_gain[l], w_tok, w_feat, rope)
        att, mem, rec = _mixers(aqt, ak, avt, attn_sink[l],
                                (mq, mk, mv, mo, grow, mlstm_gate_b[l], mlstm_norm[l]),
                                (rx, ry, conv_w[l], conv_b[l], lru_gate_w[l], lru_gate_b[l], lru_lam[l]))
        xs = _out_ffn(xs, att, mem, rec, mod[l], norm_gain[l], w_out[l].astype(BF16), w_ffn_in[l].astype(BF16),
                      w_ffn_out[l].astype(BF16), first_tile=int(l == depth - 1))
    return xs
```

```python
import functools

import jax
import jax.numpy as jnp
import numpy as np
from jax import lax
from jax.experimental import pallas as pl
from jax.experimental.pallas import tpu as pltpu

F32 = jnp.float32
BF16 = jnp.bfloat16

D_MODEL = 1024
GRID_W = 64
CTX_LEN = 256
N_MOD = 6
NORM_EPS = 1e-6
ATT_HEADS = 8
ATT_KV_HEADS = 2
ATT_GROUP = ATT_HEADS // ATT_KV_HEADS
HEAD_DIM = 64
WINDOW = 128
ATT_BLOCK = 128
ROPE_BASE = 10000.0
ROPE_PAIRS = HEAD_DIM // 4
ATT_Q = ATT_HEADS * HEAD_DIM
ATT_KV = ATT_KV_HEADS * HEAD_DIM
MLSTM_HEADS = 4
MLSTM_HEAD_DIM = 64
M_W = MLSTM_HEADS * MLSTM_HEAD_DIM
M_GATES = 4 * MLSTM_HEADS
LRU_WIDTH = 256
LRU_BLOCKS = 4
LRU_BW = LRU_WIDTH // LRU_BLOCKS
LRU_C = 8.0
CONV_WIDTH = 4
CONV_LEFT = CONV_WIDTH // 2
D_FF = -(-8 * D_MODEL // (3 * 256)) * 256
LOG2_E = float(np.log2(np.e))
Q_SCALE = HEAD_DIM ** -0.5 * LOG2_E

LANES = 128
SUBLANES = 8
TOKEN_TILE = CTX_LEN
FF_CHUNK = 256
LRU_CHAINS = 8
MOD_ROWS = 16
VMEM_LIMIT = 56 * 1024 * 1024


def _params(sem):
    return pltpu.CompilerParams(dimension_semantics=sem, vmem_limit_bytes=VMEM_LIMIT)


def _dot(a, b):
    return jnp.dot(a, b, preferred_element_type=F32)


def _dot_nt(a, b):
    return lax.dot_general(a, b, (((1,), (1,)), ((), ())), preferred_element_type=F32)


def _dot_exact(a, b):
    return jnp.dot(a, b, preferred_element_type=F32, precision=lax.Precision.HIGHEST)


def _sigmoid(x):
    return 1.0 / (1.0 + jnp.exp(-x))


def _softplus(x):
    return jnp.maximum(x, 0.0) + jnp.log1p(jnp.exp(-jnp.abs(x)))


def _rms(x, g):
    return x * lax.rsqrt(jnp.mean(x * x, axis=-1, keepdims=True) + NORM_EPS) * g


def _mod_kernel(c_ref, w_ref, b_ref, o_ref):
    c = c_ref[...]
    s = (c * _sigmoid(c)).astype(BF16)
    o_ref[...] = _dot(s, w_ref[...].astype(BF16)) + b_ref[...]


def _modulation(cvec, w_ada, b_ada):
    depth, d, n = w_ada.shape
    tn = 1536
    return pl.pallas_call(
        _mod_kernel,
        grid=(depth, n // tn),
        in_specs=[
            pl.BlockSpec((MOD_ROWS, d), lambda l, j: (0, 0)),
            pl.BlockSpec((None, d, tn), lambda l, j: (l, 0, j)),
            pl.BlockSpec((None, 1, tn), lambda l, j: (l, 0, j)),
        ],
        out_specs=pl.BlockSpec((None, MOD_ROWS, tn), lambda l, j: (l, 0, j)),
        out_shape=jax.ShapeDtypeStruct((depth, MOD_ROWS, n), F32),
        compiler_params=_params(("arbitrary", "arbitrary")),
        name="modulation",
    )(cvec, w_ada, b_ada.reshape(depth, 1, n))


_IN_COLS = {}
_col = 0
for _name, _width in (("aq", ATT_Q), ("ak", ATT_KV), ("av", ATT_KV), ("mq", M_W), ("mk", M_W), ("mv", M_W), ("mo", M_W),
                      ("mg", M_GATES), ("rx", LRU_WIDTH), ("ry", LRU_WIDTH)):
    _IN_COLS[_name] = (_col, _col + _width)
    _col += _width
_TOKEN_MAJOR = ("ak", "mq", "mk", "mv", "mo", "rx", "ry")
_FEATURE_MAJOR = ("aq", "av", "mg")


def _split_in_weights(w):
    tok = jnp.concatenate([w[:, _IN_COLS[n][0]:_IN_COLS[n][1]] for n in _TOKEN_MAJOR], axis=1)
    feat = jnp.concatenate([w[:, _IN_COLS[n][0]:_IN_COLS[n][1]] for n in _FEATURE_MAJOR], axis=1).T
    return tok.astype(BF16), feat.astype(BF16)


def _rope_slab(x, cs, sn, first):
    swapped = jnp.where(first, pltpu.roll(x, LANES - ROPE_PAIRS, 1), pltpu.roll(x, ROPE_PAIRS, 1))
    return x * cs + swapped * sn


class _TileWalk:
    def __init__(self, nb, per_row, first_tile=0):
        self.nb, self.per_row, self.first_tile, self.n_tiles = nb, per_row, first_tile, nb * per_row

    def at(self, k, lag):
        t = jnp.clip(k - lag, 0, self.n_tiles - 1)
        return t // self.per_row, t % self.per_row + self.first_tile

    def mod_row(self, bi):
        return jnp.where(bi[1] == 0, self.nb, bi[0]), 0, 0

    def stream_specs(self, streams, lag):
        d = streams[0].shape[-1]
        at = lambda k: self.at(k, lag)
        if len(streams) == 1:
            return [pl.BlockSpec((None, TOKEN_TILE, d), lambda k: (*at(k), 0))]
        return [pl.BlockSpec((None, TOKEN_TILE, d), lambda k: (at(k)[0], 0, 0)),
                pl.BlockSpec((None, TOKEN_TILE, d), lambda k: (at(k)[0], jnp.maximum(at(k)[1] - 1, 0), 0))]

    def stream_tile(self, stream_refs, lag):
        if len(stream_refs) == 1:
            return stream_refs[0][...]
        return jnp.where(self.at(pl.program_id(0), lag)[1] == 0, stream_refs[0][...], stream_refs[1][...])


def _inproj_kernel(n_streams, *refs):
    (mod_ref, g_ref, w_ref, wt_ref, cs_ref, sn_ref, cst_ref, snt_ref,
     ak_ref, mq_ref, mk_ref, mv_ref, mo_ref, rx_ref, ry_ref, aqt_ref, avt_ref, gr_ref) = refs[n_streams:]
    x = refs[0][...] if n_streams == 1 else jnp.where(pl.program_id(1) == 0, refs[0][...], refs[1][...])
    hb = (_rms(x, g_ref[0:1, :]) * (1.0 + mod_ref[1:2, :]) + mod_ref[0:1, :]).astype(BF16)
    pt = _dot_nt(wt_ref[...], hb)
    cst = cst_ref[...]
    snt = snt_ref[...]
    rp = ROPE_PAIRS
    for hd in range(ATT_HEADS):
        xs = pt[hd * HEAD_DIM:(hd + 1) * HEAD_DIM, :]
        swapped = jnp.concatenate([xs[rp:2 * rp], xs[0:rp], xs[3 * rp:4 * rp], xs[2 * rp:3 * rp]], axis=0)
        roped = (xs * cst + swapped * snt) * Q_SCALE
        for qb in range(TOKEN_TILE // ATT_BLOCK):
            aqt_ref[qb, hd * HEAD_DIM:(hd + 1) * HEAD_DIM, :] = roped[:, qb * ATT_BLOCK:(qb + 1) * ATT_BLOCK]
    for qb in range(TOKEN_TILE // ATT_BLOCK):
        avt_ref[qb] = pt[ATT_Q:ATT_Q + ATT_KV, qb * ATT_BLOCK:(qb + 1) * ATT_BLOCK]
    gr_ref[...] = pt[ATT_Q + ATT_KV:, :]
    lane = lax.broadcasted_iota(jnp.int32, (hb.shape[0], LANES), 1)
    first = (lane & (2 * ROPE_PAIRS - 1)) < ROPE_PAIRS
    ak_ref[...] = _rope_slab(_dot(hb, w_ref[:, 0:ATT_KV]), cs_ref[...], sn_ref[...], first)
    col = ATT_KV
    for ref in (mq_ref, mk_ref, mv_ref, mo_ref, rx_ref, ry_ref):
        n = ref.shape[-1]
        ref[...] = _dot(hb, w_ref[:, col:col + n])
        col += n


def _in_projection(xs, mod, gain, w_tok, w_feat, rope):
    streams = xs if isinstance(xs, tuple) else (xs,)
    nb = streams[0].shape[0]
    t = sum(s.shape[1] for s in streams)
    nt = t // TOKEN_TILE
    d = streams[0].shape[-1]
    tile = lambda n: pl.BlockSpec((None, TOKEN_TILE, n), lambda b, i: (b, i, 0))
    const = lambda a: pl.BlockSpec(a.shape, lambda b, i: (0,) * a.ndim)
    if len(streams) == 1:
        stream_specs = [tile(d)]
    else:
        stream_specs = [pl.BlockSpec((None, TOKEN_TILE, d), lambda b, i: (b, 0, 0)),
                        pl.BlockSpec((None, TOKEN_TILE, d), lambda b, i: (b, jnp.maximum(i - 1, 0), 0))]
    width = lambda n: _IN_COLS[n][1] - _IN_COLS[n][0]
    out_shapes = [jax.ShapeDtypeStruct((nb, t, width(n)), F32) for n in _TOKEN_MAJOR]
    per_tile = TOKEN_TILE // ATT_BLOCK
    out_shapes += [jax.ShapeDtypeStruct((nb, nt * per_tile, width(n), ATT_BLOCK), F32) for n in ("aq", "av")]
    out_shapes += [jax.ShapeDtypeStruct((nb, nt, M_GATES, TOKEN_TILE), F32)]
    out_specs = [tile(width(n)) for n in _TOKEN_MAJOR]
    out_specs += [pl.BlockSpec((None, per_tile, width(n), ATT_BLOCK), lambda b, i: (b, i, 0, 0)) for n in ("aq", "av")]
    out_specs += [pl.BlockSpec((None, None, M_GATES, TOKEN_TILE), lambda b, i: (b, i, 0, 0))]
    rope_cs, rope_sn, rope_cst, rope_snt = rope
    return pl.pallas_call(
        functools.partial(_inproj_kernel, len(streams)),
        grid=(nb, nt),
        in_specs=stream_specs + [
            pl.BlockSpec((None, N_MOD, d), lambda b, i: (jnp.where(i == 0, nb, b), 0, 0)),
            const(gain), const(w_tok), const(w_feat),
            pl.BlockSpec((TOKEN_TILE, LANES), lambda b, i: (i, 0)),
            pl.BlockSpec((TOKEN_TILE, LANES), lambda b, i: (i, 0)),
            pl.BlockSpec((HEAD_DIM, TOKEN_TILE), lambda b, i: (0, i)),
            pl.BlockSpec((HEAD_DIM, TOKEN_TILE), lambda b, i: (0, i)),
        ],
        out_specs=out_specs,
        out_shape=out_shapes,
        compiler_params=_params(("parallel", "arbitrary")),
        name="in_projection",
    )(*streams, mod, gain, w_tok, w_feat, rope_cs, rope_sn, rope_cst, rope_snt)


def _attn_block(j, nblk, sink_ref, qt_ref, kp_ref, kc_ref, kn_ref, kx_ref, vp_ref, vc_ref, vn_ref, vx_ref, o_ref):
    ctx_blocks = CTX_LEN // ATT_BLOCK
    nq = ATT_BLOCK
    cols = ATT_GROUP * nq
    c = lax.broadcasted_iota(jnp.int32, (ATT_BLOCK, cols), 0)
    r = lax.broadcasted_iota(jnp.int32, (ATT_BLOCK, cols), 1) & (nq - 1)
    m_prev = (c >= r) & (j >= ctx_blocks + 1)
    m_next = (c <= r) & (j >= ctx_blocks) & (j <= nblk - 2)
    is_lat = j >= ctx_blocks
    col = lax.broadcasted_iota(jnp.int32, (1, cols), 1)
    neg = -jnp.inf
    kp = kp_ref[...].astype(BF16)
    kc = kc_ref[...].astype(BF16)
    kn = kn_ref[...].astype(BF16)
    kx = kx_ref[...].astype(BF16)
    zeros = jnp.zeros((HEAD_DIM, cols), BF16)
    scores = []
    for kh in range(ATT_KV_HEADS):
        hd0 = kh * ATT_GROUP
        qg = jnp.concatenate([qt_ref[(hd0 + g) * HEAD_DIM:(hd0 + g + 1) * HEAD_DIM, :] for g in range(ATT_GROUP)],
                             axis=1).astype(BF16)
        rhs = jnp.concatenate([qg, zeros] if kh == 0 else [zeros, qg], axis=0)
        s_x = _dot(kx, rhs)
        scores.append((jnp.where(m_prev, _dot(kp, rhs), neg), jnp.where(is_lat, _dot(kc, rhs), neg),
                       jnp.where(m_next, _dot(kn, rhs), neg), s_x[:ATT_BLOCK], s_x[ATT_BLOCK:]))
    yield
    for kh in range(ATT_KV_HEADS):
        hd0 = kh * ATT_GROUP
        sink = jnp.full((1, cols), sink_ref[hd0], F32)
        for g in range(1, ATT_GROUP):
            sink = jnp.where(col >= g * nq, sink_ref[hd0 + g], sink)
        sink = sink * LOG2_E
        s_p, s_c, s_n, s_x0, s_x1 = scores[kh]
        m_el = jnp.maximum(jnp.maximum(jnp.maximum(s_p, s_c), jnp.maximum(s_n, s_x0)), s_x1)
        m = jnp.maximum(jnp.max(m_el, axis=0, keepdims=True), sink)
        p_p = jnp.exp2(s_p - m)
        p_c = jnp.exp2(s_c - m)
        p_n = jnp.exp2(s_n - m)
        p_x0 = jnp.exp2(s_x0 - m)
        p_x1 = jnp.exp2(s_x1 - m)
        den = jnp.sum((p_p + p_c) + (p_n + p_x0) + p_x1, axis=0, keepdims=True) + jnp.exp2(sink - m)
        vs = slice(kh * HEAD_DIM, (kh + 1) * HEAD_DIM)
        o = (_dot(vp_ref[vs, :].astype(BF16), p_p.astype(BF16)) + _dot(vc_ref[vs, :].astype(BF16), p_c.astype(BF16))
             + _dot(vn_ref[vs, :].astype(BF16), p_n.astype(BF16))
             + _dot(vx_ref[0, vs, :].astype(BF16), p_x0.astype(BF16)) + _dot(vx_ref[1, vs, :].astype(BF16), p_x1.astype(BF16)))
        o = o / den
        for pair in range(ATT_GROUP // 2):
            two = jnp.concatenate([o[:, (2 * pair) * nq:(2 * pair + 1) * nq], o[:, (2 * pair + 1) * nq:(2 * pair + 2) * nq]],
                                  axis=0)
            lo = (hd0 + 2 * pair) * HEAD_DIM
            o_ref[:, lo:lo + 2 * HEAD_DIM] = two.T.astype(o_ref.dtype)


def _seq_block(d, j, nblk):
    return jnp.where((d == 0) | (j == 0), j, nblk - j)


def _lane_scan(x, op, fill, rev):
    n = x.shape[-1]
    lane = lax.broadcasted_iota(jnp.int32, x.shape, 1)
    step = 1
    while step < n:
        if rev:
            shifted = jnp.where(lane < n - step, pltpu.roll(x, n - step, 1), fill)
        else:
            shifted = jnp.where(lane >= step, pltpu.roll(x, step, 1), fill)
        x = op(x, shifted)
        step *= 2
    return x


def _split3(x):
    hi = x.astype(BF16).astype(F32)
    mid = (x - hi).astype(BF16).astype(F32)
    lo = x - hi - mid
    return hi, mid, lo


def _mlstm_tables():
    H, dh = MLSTM_HEADS, MLSTM_HEAD_DIM
    n_pairs = H // 2
    n_slabs = H + 3 * n_pairs
    sel = np.zeros((LANES, n_slabs * LANES), np.float32)
    for part in range(3):
        base = part * 4 * H
        for h in range(H):
            sel[base + h, h * LANES:(h + 1) * LANES] = 1.0
        for qty in range(3):
            for p in range(n_pairs):
                slab = H + qty * n_pairs + p
                for half in range(2):
                    row = base + H * qty + 2 * p + half
                    sel[row, slab * LANES + half * dh:slab * LANES + (half + 1) * dh] = 1.0
    ones_bd = np.zeros((2 * TOKEN_TILE, LANES), np.float32)
    ones_bd[:TOKEN_TILE, :dh] = 1.0
    ones_bd[TOKEN_TILE:, dh:] = 1.0
    avg = np.zeros((3 * LANES, LANES), np.float32)
    for part in range(3):
        for half in range(2):
            avg[part * LANES + half * dh:part * LANES + (half + 1) * dh, half * dh:(half + 1) * dh] = 1.0 / dh
    return jnp.asarray(sel, BF16), jnp.asarray(ones_bd, BF16), jnp.asarray(avg, BF16)


def _mlstm_block_terms(rev, g, tri_ref):
    L = TOKEN_TILE
    H = MLSTM_HEADS
    ioff = 2 * H if rev else 0
    li = g[ioff:ioff + H, :]
    lf = -_softplus(-g[ioff + H:ioff + 2 * H, :])
    parts = _dot(jnp.concatenate(_split3(lf), axis=0).astype(BF16), tri_ref[...])
    b = parts[0:H] + parts[H:2 * H] + parts[2 * H:3 * H]
    r = li - b
    cm = _lane_scan(r, jnp.maximum, -jnp.inf, rev)
    last = 0 if rev else L - 1
    g_tot = b[:, last:last + 1]
    a = g_tot - b + li
    m_loc = jnp.max(a, axis=1, keepdims=True)
    w = jnp.exp(a - m_loc)
    return r, cm, b, w, g_tot, m_loc


def _mlstm_stash_terms(terms, sel_ref, cols_ref, r_ref, gm_ref, slot):
    L = TOKEN_TILE
    H, dh = MLSTM_HEADS, MLSTM_HEAD_DIM
    r, cm, b, w, g_tot, m_loc = terms
    stacked = jnp.concatenate([cm, b, w, jnp.zeros((H, L), F32)], axis=0)
    hi, mid, lo = _split3(stacked)
    pad = jnp.zeros((LANES - 3 * 4 * H, L), F32)
    cols_ref[slot] = _dot(jnp.concatenate([hi, mid, lo, pad], axis=0).T.astype(BF16), sel_ref[...])
    r_ref[slot] = jnp.concatenate([r, jnp.zeros((SUBLANES - H, L), F32)], axis=0)
    lane_row = lax.broadcasted_iota(jnp.int32, (1, LANES), 1)

    def layout(col):
        rows = [jnp.broadcast_to(col, (H, LANES))]
        rows += [jnp.where(lane_row < dh, col[2 * p:2 * p + 1, :], col[2 * p + 1:2 * p + 2, :]) for p in range(H // 2)]
        rows += [jnp.zeros((SUBLANES - H - H // 2, LANES), F32)]
        return jnp.concatenate(rows, axis=0)

    gm_ref[slot] = jnp.concatenate([layout(g_tot), layout(m_loc)], axis=0)


def _mlstm_finish(h_ref, o_ref, ng_ref, avg_ref):
    parts = []
    for p in range(M_W // LANES):
        ps = slice(p * LANES, (p + 1) * LANES)
        ht = h_ref[:, ps]
        mu = _dot(jnp.concatenate(_split3(ht), axis=1).astype(BF16), avg_ref[...])
        cen = ht - mu
        var = _dot(jnp.concatenate(_split3(cen * cen), axis=1).astype(BF16), avg_ref[...])
        parts.append(_sigmoid(o_ref[:, ps]) * (cen * lax.rsqrt(var + NORM_EPS) * ng_ref[:, ps]))
    return jnp.concatenate(parts, axis=1)


def _mlstm_kernel(q_ref, k_ref, v_ref, gr_ref, gnext_ref, gb_ref, tri_ref, sel_ref, ones_ref, out_ref,
                  hf_ref, c_ref, n_ref, m_ref, cols_ref, r_ref, gm_ref, rev, init=False):
    d = pl.program_id(1)
    j = pl.program_id(2)
    nblk = pl.num_programs(2)
    L = TOKEN_TILE
    H, dh = MLSTM_HEADS, MLSTM_HEAD_DIM
    n_pairs = H // 2
    row0 = pl.multiple_of(_seq_block(d, j, nblk) * L, L)
    slot = j & 1

    if init:
        c_ref[...] = jnp.zeros_like(c_ref)
        n_ref[...] = jnp.zeros_like(n_ref)
        m_ref[...] = jnp.zeros_like(m_ref)
        _mlstm_stash_terms(_mlstm_block_terms(rev, gr_ref[...] + gb_ref[...], tri_ref), sel_ref, cols_ref, r_ref, gm_ref, 0)
        return

    def body(rev):
        next_terms = _mlstm_block_terms(rev, gnext_ref[...] + gb_ref[...], tri_ref)

        r = r_ref[slot]
        gm = gm_ref[slot]
        slab = lambda i: cols_ref[slot, :, i * LANES:(i + 1) * LANES]
        m0 = m_ref[...]
        g_tot, m_loc = gm[0:SUBLANES], gm[SUBLANES:]
        m_new = jnp.maximum(g_tot + m0, m_loc)
        f_prev = jnp.exp(g_tot + m0 - m_new)
        f_loc = jnp.exp(m_loc - m_new)
        m_ref[...] = m_new
        lane = lax.broadcasted_iota(jnp.int32, (L, LANES), 1)
        first_head = lane < dh
        sq = (LANES, LANES)
        tri_vis = ((lax.broadcasted_iota(jnp.int32, sq, 1) >= lax.broadcasted_iota(jnp.int32, sq, 0)) if rev
                   else (lax.broadcasted_iota(jnp.int32, sq, 1) <= lax.broadcasted_iota(jnp.int32, sq, 0)))
        kt = (k_ref[...] * (dh ** -0.5)).T.astype(BF16)
        zeros_half = jnp.zeros((dh, L), BF16)
        bd_mask = ((lax.broadcasted_iota(jnp.int32, (LANES, LANES), 0) < dh)
                   == (lax.broadcasted_iota(jnp.int32, (LANES, LANES), 1) < dh))
        for p in range(n_pairs):
            ps = slice(p * LANES, (p + 1) * LANES)
            qb = q_ref[:, ps].astype(BF16)
            v = v_ref[:, ps]
            kt_pair = kt[ps, :]
            sm = []
            for half in range(2):
                h = 2 * p + half
                kt_h = kt[h * dh:(h + 1) * dh, :]
                kt_masked = jnp.concatenate([kt_h, zeros_half] if half == 0 else [zeros_half, kt_h], axis=0)
                s = _dot(qb, kt_masked)
                m_b = jnp.maximum(slab(h), m0[h:h + 1, :])
                for lt in range(L // LANES):
                    quads = []
                    for qt in range(L // LANES):
                        rows = slice(qt * LANES, (qt + 1) * LANES)
                        if (lt < qt) if rev else (lt > qt):
                            quads.append(jnp.zeros((LANES, LANES), BF16))
                            continue
                        expo = r[h:h + 1, lt * LANES:(lt + 1) * LANES] - m_b[rows]
                        if lt == qt:
                            expo = jnp.where(tri_vis, expo, -jnp.inf)
                        quads.append((s[rows, lt * LANES:(lt + 1) * LANES] * jnp.exp(expo)).astype(BF16))
                    sm.append(jnp.concatenate(quads, axis=0))
            sm = jnp.concatenate(sm, axis=1)
            v_bd = jnp.concatenate([jnp.where(first_head, v, 0.0), jnp.where(first_head, 0.0, v)], axis=0).astype(BF16)
            m0_pair = m0[H + p:H + p + 1, :]
            m_pair = jnp.maximum(slab(H + p), m0_pair)
            f_pair = jnp.exp(m0_pair - m_pair)
            num = _dot(sm, v_bd) + f_pair * _dot(qb, c_ref[p].astype(BF16))
            den = _dot(sm, ones_ref[...]) + f_pair * _dot(qb, n_ref[p].astype(BF16))
            hh = num / jnp.maximum(jnp.abs(den), jnp.exp(-(slab(H + n_pairs + p) + m_pair)))
            w_pair = slab(H + 2 * n_pairs + p)
            c_loc = jnp.where(bd_mask, _dot(kt_pair, (w_pair * v).astype(BF16)), 0.0)
            n_loc = jnp.where(bd_mask, _dot(kt_pair, w_pair.astype(BF16)), 0.0)
            c_ref[p] = f_prev[H + p:H + p + 1, :] * c_ref[p] + f_loc[H + p:H + p + 1, :] * c_loc
            n_ref[p] = f_prev[H + p:H + p + 1, :] * n_ref[p] + f_loc[H + p:H + p + 1, :] * n_loc
            if not rev:
                hf_ref[pl.ds(row0, L), ps] = hh
            else:
                out_ref[:, ps] = hf_ref[pl.ds(row0, L), ps] + hh
        _mlstm_stash_terms(next_terms, sel_ref, cols_ref, r_ref, gm_ref, 1 - slot)

    body(rev)


def _mlstm_specs(mq, mk, mv, grow, gate_b):
    nb, t, w = mq.shape
    nblk = t // TOKEN_TILE
    blk = lambda b, d, j: (b, _seq_block(d, j, nblk), 0)
    tile = pl.BlockSpec((None, TOKEN_TILE, w), blk)
    gb = gate_b.reshape(M_GATES, 1)
    sel, ones_bd, _ = _mlstm_tables()
    upper = np.triu(np.ones((TOKEN_TILE, TOKEN_TILE), np.float32))
    tri = jnp.asarray(np.stack([upper, upper.T]), BF16)
    const = lambda a: pl.BlockSpec(a.shape, lambda b, d, j: (0,) * a.ndim)
    n_pairs = MLSTM_HEADS // 2
    in_specs = [
        tile, tile, tile,
        pl.BlockSpec((None, None, M_GATES, TOKEN_TILE), lambda b, d, j: (b, _seq_block(d, j, nblk), 0, 0)),
        pl.BlockSpec((None, None, M_GATES, TOKEN_TILE),
                     lambda b, d, j: (b, _seq_block(d, jnp.minimum(j + 1, nblk - 1), nblk), 0, 0)),
        const(gb),
        pl.BlockSpec((None, TOKEN_TILE, TOKEN_TILE), lambda b, d, j: (d, 0, 0)),
        const(sel), const(ones_bd),
    ]
    args = [mq, mk, mv, grow, grow, gb, tri, sel, ones_bd]
    out_spec = pl.BlockSpec((None, TOKEN_TILE, w), lambda b, d, j: (b, jnp.where(d == 0, 0, _seq_block(d, j, nblk)), 0))
    scratch = [
        pltpu.VMEM((t, w), F32),
        pltpu.VMEM((n_pairs, LANES, LANES), F32),
        pltpu.VMEM((n_pairs, LANES, LANES), F32),
        pltpu.VMEM((SUBLANES, LANES), F32),
        pltpu.VMEM((2, TOKEN_TILE, sel.shape[1]), F32),
        pltpu.VMEM((2, SUBLANES, TOKEN_TILE), F32),
        pltpu.VMEM((2, 2 * SUBLANES, LANES), F32),
    ]
    return in_specs, args, out_spec, jax.ShapeDtypeStruct((nb, t, w), F32), scratch


def _lru_finish(h_ref, y_ref):
    y = y_ref[...]
    gelu = 0.5 * y * (1.0 + jnp.tanh(np.sqrt(2.0 / np.pi).astype(np.float32) * (y + 0.044715 * (y * y * y))))
    return h_ref[...] * gelu


def _lru_kernel(x_ref, xp_ref, xn_ref, cw_ref, cb_ref, gw_ref, gb_ref, lam_ref, out_ref,
                hf_ref, xe_ref, carry_ref, a_ref, u_ref, hb_ref, hl_ref, al_ref, rev, init=False):
    d = pl.program_id(1)
    j = pl.program_id(2)
    nblk = pl.num_programs(2)
    L = TOKEN_TILE
    pos = _seq_block(d, j, nblk)
    row0 = pl.multiple_of(pos * L, L)

    if init:
        carry_ref[...] = jnp.zeros_like(carry_ref)
        return

    has_prev = pos >= 2
    has_next = (pos >= 1) & (pos <= nblk - 2)
    xe_ref[0:SUBLANES, :] = jnp.where(has_prev, xp_ref[...], 0.0)
    xe_ref[SUBLANES:SUBLANES + L, :] = x_ref[...]
    xe_ref[SUBLANES + L:, :] = jnp.where(has_next, xn_ref[...], 0.0)
    seq = cb_ref[...]
    for tap in range(CONV_WIDTH):
        off = SUBLANES - CONV_LEFT + tap
        seq = seq + cw_ref[tap:tap + 1, :] * xe_ref[off:off + L, :]
    sb = seq.astype(BF16)
    r = 0.5 + 0.5 * jnp.tanh(0.5 * (_dot(sb, gw_ref[0]) + gb_ref[0:1, :]))
    i = 0.5 + 0.5 * jnp.tanh(0.5 * (_dot(sb, gw_ref[1]) + gb_ref[1:2, :]))
    log_a = (-LRU_C * _softplus(-lam_ref[...])) * r
    a0 = jnp.exp(log_a)
    th = jnp.tanh(log_a)
    u0 = jnp.sqrt(-2.0 * th) * lax.rsqrt(1.0 - th) * (i * seq)
    n_lt = LRU_WIDTH // LANES
    for lt in range(n_lt):
        a_ref[lt] = a0[:, lt * LANES:(lt + 1) * LANES]
        u_ref[lt] = u0[:, lt * LANES:(lt + 1) * LANES]
    seg = L // (LRU_CHAINS * SUBLANES)
    sub = lax.broadcasted_iota(jnp.int32, (SUBLANES, LRU_WIDTH), 0)
    rows = lambda c, i: pl.ds(c * seg * SUBLANES + i, SUBLANES, stride=seg)
    strided = lambda ref, c, i: jnp.concatenate([ref[lt, rows(c, i), :] for lt in range(n_lt)], axis=1)

    def scan(rev):
        steps = range(seg - 1, -1, -1) if rev else range(seg)
        chains = range(LRU_CHAINS - 1, -1, -1) if rev else range(LRU_CHAINS)
        edge = SUBLANES - 1 if rev else 0
        shift = lambda x, n: pltpu.roll(x, SUBLANES - n if rev else n, 0)
        ends, prods = {}, {}
        for c in chains:
            h = jnp.zeros((SUBLANES, LRU_WIDTH), F32)
            ac = jnp.ones((SUBLANES, LRU_WIDTH), F32)
            for i in steps:
                ai = strided(a_ref, c, i)
                h = ai * h + strided(u_ref, c, i)
                ac = ai * ac
                hl_ref[c * seg + i] = h
                al_ref[c * seg + i] = ac
            step = 1
            while step < SUBLANES:
                ok = (sub < SUBLANES - step) if rev else (sub >= step)
                h = jnp.where(ok, ac * shift(h, step) + h, h)
                ac = jnp.where(ok, ac * shift(ac, step), ac)
                step *= 2
            ends[c], prods[c] = h, ac
        carry = carry_ref[...]
        for c in chains:
            true_ends = ends[c] + prods[c] * carry
            carry_in = jnp.where(sub == edge, carry, shift(true_ends, 1))
            carry = true_ends[SUBLANES - 1 - edge:SUBLANES - edge, :]
            for i in range(seg):
                hi = hl_ref[c * seg + i] + al_ref[c * seg + i] * carry_in
                for lt in range(n_lt):
                    hb_ref[lt, rows(c, i), :] = hi[:, lt * LANES:(lt + 1) * LANES]
        carry_ref[...] = carry
        return jnp.concatenate([hb_ref[lt] for lt in range(n_lt)], axis=1)

    def forward():
        hf_ref[pl.ds(row0, L), :] = scan(False)

    def backward():
        out_ref[...] = hf_ref[pl.ds(row0, L), :] + scan(True)

    backward() if rev else forward()


def _rglru_specs(rx, conv_w, conv_b, gate_w, gate_b, lam):
    nb, t, w = rx.shape
    nblk = t // TOKEN_TILE
    per_tile = TOKEN_TILE // SUBLANES
    n8 = t // SUBLANES
    blk = lambda b, d, j: (b, _seq_block(d, j, nblk), 0)
    tile = pl.BlockSpec((None, TOKEN_TILE, w), blk)
    halo = lambda f: pl.BlockSpec((None, SUBLANES, w), lambda b, d, j: (b, f(_seq_block(d, j, nblk)), 0))
    prev8 = lambda p: jnp.maximum(p * per_tile - 1, 0)
    next8 = lambda p: jnp.minimum((p + 1) * per_tile, n8 - 1)
    eye = jnp.eye(LRU_BLOCKS, dtype=gate_w.dtype)
    gw = jnp.einsum('dgnij,nm->dgnimj', gate_w, eye).reshape(2, 2, w, w).astype(BF16)
    cb = conv_b.reshape(1, w)
    in_specs = [
        tile, halo(prev8), halo(next8),
        pl.BlockSpec(conv_w.shape, lambda b, d, j: (0, 0)),
        pl.BlockSpec(cb.shape, lambda b, d, j: (0, 0)),
        pl.BlockSpec((None, 2, w, w), lambda b, d, j: (d, 0, 0, 0)),
        pl.BlockSpec((None, 2, w), lambda b, d, j: (d, 0, 0)),
        pl.BlockSpec((None, 1, w), lambda b, d, j: (d, 0, 0)),
    ]
    args = [rx, rx, rx, conv_w, cb, gw, gate_b, lam.reshape(2, 1, w)]
    out_spec = pl.BlockSpec((None, TOKEN_TILE, w), lambda b, d, j: (b, jnp.where(d == 0, 0, _seq_block(d, j, nblk)), 0))
    scratch = [
        pltpu.VMEM((t, w), F32),
        pltpu.VMEM((TOKEN_TILE + 2 * SUBLANES, w), F32),
        pltpu.VMEM((1, w), F32),
        pltpu.VMEM((w // LANES, TOKEN_TILE, LANES), F32),
        pltpu.VMEM((w // LANES, TOKEN_TILE, LANES), F32),
        pltpu.VMEM((w // LANES, TOKEN_TILE, LANES), F32),
        pltpu.VMEM((TOKEN_TILE // SUBLANES, SUBLANES, w), F32),
        pltpu.VMEM((TOKEN_TILE // SUBLANES, SUBLANES, w), F32),
    ]
    return in_specs, args, out_spec, jax.ShapeDtypeStruct((nb, t, w), F32), scratch


def _mixers_kernel(n_mem, n_rec, n_mem_scratch, *refs):
    d = pl.program_id(1)
    j = pl.program_id(2)
    n_in = n_mem + n_rec
    mem_in, rec_in = refs[:n_mem], refs[n_mem:n_in]
    mem_out, rec_out = refs[n_in:n_in + 2]
    mem_scratch = refs[n_in + 2:n_in + 2 + n_mem_scratch]
    rec_scratch = refs[n_in + 2 + n_mem_scratch:]

    def body(rev):
        @pl.when(j == 0)
        def _():
            _mlstm_kernel(*mem_in, mem_out, *mem_scratch, rev=rev, init=True)
            _lru_kernel(*rec_in, rec_out, *rec_scratch, rev=rev, init=True)

        _mlstm_kernel(*mem_in, mem_out, *mem_scratch, rev=rev)
        _lru_kernel(*rec_in, rec_out, *rec_scratch, rev=rev)

    pl.when(d == 0)(functools.partial(body, False))
    pl.when(d == 1)(functools.partial(body, True))


def _mixers(mlstm_args, rglru_args):
    nb, t, _ = mlstm_args[0].shape
    mem_specs, mem_args, mem_out, mem_shape, mem_scratch = _mlstm_specs(*mlstm_args)
    rec_specs, rec_args, rec_out, rec_shape, rec_scratch = _rglru_specs(*rglru_args)
    return pl.pallas_call(
        functools.partial(_mixers_kernel, len(mem_args), len(rec_args), len(mem_scratch)),
        grid=(nb, 2, t // TOKEN_TILE),
        in_specs=mem_specs + rec_specs,
        out_specs=[mem_out, rec_out],
        out_shape=[mem_shape, rec_shape],
        scratch_shapes=mem_scratch + rec_scratch,
        compiler_params=_params(("parallel", "arbitrary", "arbitrary")),
        name="mixers",
    )(*mem_args, *rec_args)


def _outffn_kernel(n_streams, walk, n_att_blocks, *refs):
    k = pl.program_id(0)
    streams = refs[:n_streams]
    (sink_ref, qt_ref, kp_ref, kt_ref, kn_ref, kx_ref, vp_ref, vt_ref, vn_ref, vx_ref,
     mem_ref, mo_ref, rec_ref, ry_ref, ng_ref, avg_ref, mod_ref, modf_ref, g_ref, wo_ref, wi_ref, wd_ref, out_ref,
     act_ref, a0_ref, a1_ref, h0_ref, h1_ref, x0_ref, x1_ref) = refs[n_streams:]

    @pl.when(k == 0)
    def _():
        a1_ref[...] = jnp.zeros_like(a1_ref)
        h0_ref[...] = jnp.zeros_like(h0_ref)
        x0_ref[...] = jnp.zeros_like(x0_ref)

    def body(att_w, att_r, h_w, x_w, h_r, x_r):
        q0 = (TOKEN_TILE // ATT_BLOCK) * walk.at(k, 0)[1]
        lo, hi = pl.ds(0, ATT_BLOCK), pl.ds(ATT_BLOCK, ATT_BLOCK)
        att_blocks = [
            _attn_block(q0, n_att_blocks, sink_ref, qt_ref.at[0], kp_ref, kt_ref.at[lo], kt_ref.at[hi], kx_ref,
                        vp_ref, vt_ref.at[0], vt_ref.at[1], vx_ref, att_w.at[lo]),
            _attn_block(q0 + 1, n_att_blocks, sink_ref, qt_ref.at[1], kt_ref.at[lo], kt_ref.at[hi], kn_ref, kx_ref,
                        vt_ref.at[0], vt_ref.at[1], vn_ref, vx_ref, att_w.at[hi])]
        for blk in att_blocks:
            next(blk)
        x = walk.stream_tile(streams, 1)
        x1 = x + mod_ref[2:3, :] * _rms(_dot(att_r[...], wo_ref[...]), g_ref[1:2, :])
        x_w[...] = x1
        h_w[...] = (_rms(x1, g_ref[2:3, :]) * (1.0 + mod_ref[4:5, :]) + mod_ref[3:4, :]).astype(BF16)
        h = h_r[...]
        for c0 in range(0, D_FF, FF_CHUNK):
            gate = _dot(h, wi_ref[:, c0:c0 + FF_CHUNK])
            up = _dot(h, wi_ref[:, D_FF + c0:D_FF + c0 + FF_CHUNK])
            half = 0.5 * gate
            act_ref[:, c0:c0 + FF_CHUNK] = ((half + half * jnp.tanh(half)) * up).astype(BF16)
        for blk in att_blocks:
            for _ in blk:
                pass
        att_w[:, ATT_Q:ATT_Q + M_W] = _mlstm_finish(mem_ref, mo_ref, ng_ref, avg_ref).astype(BF16)
        att_w[:, ATT_Q + M_W:] = _lru_finish(rec_ref, ry_ref).astype(BF16)
        f = _dot(act_ref[...], wd_ref[...])
        out_ref[...] = x_r[...] + modf_ref[5:6, :] * _rms(f, g_ref[3:4, :])

    pl.when((k & 1) == 0)(functools.partial(body, a0_ref, a1_ref, h1_ref, x1_ref, h0_ref, x0_ref))
    pl.when((k & 1) == 1)(functools.partial(body, a1_ref, a0_ref, h0_ref, x0_ref, h1_ref, x1_ref))


def _attn_out_ffn(xs, aqt, ak, avt, sink, mem, mo, norm_g, rec, ry, mod, gain, w_out, w_ffn_in, w_down, first_tile=0):
    streams = xs if isinstance(xs, tuple) else (xs,)
    assert len(streams) == 1 or first_tile == 0
    nb, _, d = streams[0].shape
    t = ak.shape[1]
    per_row = t // TOKEN_TILE - first_tile
    walk = _TileWalk(nb, per_row, first_tile)
    n_att_blocks = t // ATT_BLOCK
    per_tile = TOKEN_TILE // ATT_BLOCK
    ctx_blocks = CTX_LEN // ATT_BLOCK
    s0 = lambda k: walk.at(k, 0)
    s1 = lambda k: walk.at(k, 1)
    s2 = lambda k: walk.at(k, 2)
    before = lambda k: jnp.maximum(per_tile * s0(k)[1] - 1, 0)
    after = lambda k: jnp.minimum(per_tile * (s0(k)[1] + 1), n_att_blocks - 1)
    tile0 = lambda n: pl.BlockSpec((None, TOKEN_TILE, n), lambda k: (*s0(k), 0))
    const = lambda a: pl.BlockSpec(a.shape, lambda k: (0,) * a.ndim, pipeline_mode=pl.Buffered(1))
    ng = norm_g.reshape(1, M_W)
    avg = _mlstm_tables()[2]
    att_specs = [
        pl.BlockSpec(memory_space=pltpu.SMEM),
        pl.BlockSpec((None, per_tile, ATT_Q, ATT_BLOCK), lambda k: (*s0(k), 0, 0)),
        pl.BlockSpec((None, ATT_BLOCK, ATT_KV), lambda k: (s0(k)[0], before(k), 0)),
        pl.BlockSpec((None, TOKEN_TILE, ATT_KV), lambda k: (*s0(k), 0)),
        pl.BlockSpec((None, ATT_BLOCK, ATT_KV), lambda k: (s0(k)[0], after(k), 0)),
        pl.BlockSpec((None, CTX_LEN, ATT_KV), lambda k: (s0(k)[0], 0, 0)),
        pl.BlockSpec((None, None, ATT_KV, ATT_BLOCK), lambda k: (s0(k)[0], before(k), 0, 0)),
        pl.BlockSpec((None, per_tile, ATT_KV, ATT_BLOCK), lambda k: (*s0(k), 0, 0)),
        pl.BlockSpec((None, None, ATT_KV, ATT_BLOCK), lambda k: (s0(k)[0], after(k), 0, 0)),
        pl.BlockSpec((None, ctx_blocks, ATT_KV, ATT_BLOCK), lambda k: (s0(k)[0], 0, 0, 0)),
    ]
    return pl.pallas_call(
        functools.partial(_outffn_kernel, len(streams), walk, n_att_blocks),
        grid=(walk.n_tiles + 2,),
        in_specs=walk.stream_specs(streams, 1) + att_specs + [
            tile0(M_W), tile0(M_W), tile0(LRU_WIDTH), tile0(LRU_WIDTH), const(ng), const(avg),
            pl.BlockSpec((None, N_MOD, d), lambda k: walk.mod_row(s1(k))),
            pl.BlockSpec((None, N_MOD, d), lambda k: walk.mod_row(s2(k))),
            const(gain), const(w_out), const(w_ffn_in), const(w_down),
        ],
        out_specs=pl.BlockSpec((None, TOKEN_TILE, d), lambda k: (s2(k)[0], s2(k)[1] - first_tile, 0)),
        out_shape=jax.ShapeDtypeStruct((nb, per_row * TOKEN_TILE, d), F32),
        scratch_shapes=[pltpu.VMEM((TOKEN_TILE, D_FF), BF16),
                        pltpu.VMEM((TOKEN_TILE, d), BF16), pltpu.VMEM((TOKEN_TILE, d), BF16),
                        pltpu.VMEM((TOKEN_TILE, d), BF16), pltpu.VMEM((TOKEN_TILE, d), BF16),
                        pltpu.VMEM((TOKEN_TILE, d), F32), pltpu.VMEM((TOKEN_TILE, d), F32)],
        compiler_params=_params(("arbitrary",)),
        name="attn_out_ffn",
    )(*streams, sink, aqt, ak, ak, ak, ak, avt, avt, avt, avt, mem, mo, rec, ry, ng, avg, mod, mod, gain, w_out, w_ffn_in,
      w_down)


def _rope_tables(n_lat):
    t = jnp.arange(n_lat)
    row = (t // GRID_W).astype(F32)
    col = (t % GRID_W).astype(F32)
    freqs = ROPE_BASE ** (-jnp.arange(ROPE_PAIRS, dtype=F32) / ROPE_PAIRS)
    ang_r = row[:, None] * freqs
    ang_c = col[:, None] * freqs
    cs = jnp.concatenate([jnp.cos(ang_r), jnp.cos(ang_r), jnp.cos(ang_c), jnp.cos(ang_c)], axis=-1)
    sn = jnp.concatenate([-jnp.sin(ang_r), jnp.sin(ang_r), -jnp.sin(ang_c), jnp.sin(ang_c)], axis=-1)
    cs = jnp.concatenate([jnp.ones((CTX_LEN, HEAD_DIM), F32), cs], axis=0)
    sn = jnp.concatenate([jnp.zeros((CTX_LEN, HEAD_DIM), F32), sn], axis=0)
    return jnp.tile(cs, (1, LANES // HEAD_DIM)), jnp.tile(sn, (1, LANES // HEAD_DIM)), cs.T, sn.T


def kernel(x, c, ctx, c_ctx, w_ada, b_ada, norm_gain, w_in, w_out, attn_sink, mlstm_gate_b, mlstm_norm, conv_w, conv_b,
           lru_gate_w, lru_gate_b, lru_lam, w_ffn_in, w_ffn_out):
    nb, n_lat, d = x.shape
    depth = w_ada.shape[0]
    assert ctx.shape[1] == CTX_LEN and n_lat % TOKEN_TILE == 0 and nb < MOD_ROWS
    cvec = jnp.concatenate([c, c_ctx[None, :], jnp.zeros((MOD_ROWS - nb - 1, d), F32)], axis=0)
    mod = _modulation(cvec, w_ada, b_ada).reshape(depth, MOD_ROWS, N_MOD, d)
    rope = _rope_tables(n_lat)
    xs = (ctx, x) if depth > 1 else jnp.concatenate([ctx, x], axis=1)
    for l in range(depth):
        w_tok, w_feat = _split_in_weights(w_in[l])
        ak, mq, mk, mv, mo, rx, ry, aqt, avt, grow = _in_projection(xs, mod[l], norm_gain[l], w_tok, w_feat, rope)
        mem, rec = _mixers((mq, mk, mv, grow, mlstm_gate_b[l]),
                           (rx, conv_w[l], conv_b[l], lru_gate_w[l], lru_gate_b[l], lru_lam[l]))
        xs = _attn_out_ffn(xs, aqt, ak, avt, attn_sink[l], mem, mo, mlstm_norm[l], rec, ry, mod[l], norm_gain[l],
                           w_out[l].astype(BF16), w_ffn_in[l].astype(BF16), w_ffn_out[l].astype(BF16),
                           first_tile=int(l == depth - 1))
    return xs
```

```python
import functools

import jax
import jax.numpy as jnp
import numpy as np
from jax import lax
from jax.experimental import pallas as pl
from jax.experimental.pallas import tpu as pltpu

F32 = jnp.float32
BF16 = jnp.bfloat16

D_MODEL = 1024
GRID_W = 64
CTX_LEN = 256
N_MOD = 6
NORM_EPS = 1e-6
ATT_HEADS = 8
ATT_KV_HEADS = 2
ATT_GROUP = ATT_HEADS // ATT_KV_HEADS
HEAD_DIM = 64
WINDOW = 128
ATT_BLOCK = 128
ROPE_BASE = 10000.0
ROPE_PAIRS = HEAD_DIM // 4
ATT_Q = ATT_HEADS * HEAD_DIM
ATT_KV = ATT_KV_HEADS * HEAD_DIM
MLSTM_HEADS = 4
MLSTM_HEAD_DIM = 64
M_W = MLSTM_HEADS * MLSTM_HEAD_DIM
M_GATES = 4 * MLSTM_HEADS
LRU_WIDTH = 256
LRU_BLOCKS = 4
LRU_BW = LRU_WIDTH // LRU_BLOCKS
LRU_C = 8.0
CONV_WIDTH = 4
CONV_LEFT = CONV_WIDTH // 2
D_FF = -(-8 * D_MODEL // (3 * 256)) * 256
LOG2_E = float(np.log2(np.e))
Q_SCALE = HEAD_DIM ** -0.5 * LOG2_E

LANES = 128
SUBLANES = 8
TOKEN_TILE = CTX_LEN
FF_CHUNK = 256
LRU_CHAINS = 8
MOD_ROWS = 16
VMEM_LIMIT = 56 * 1024 * 1024


def _params(sem):
    return pltpu.CompilerParams(dimension_semantics=sem, vmem_limit_bytes=VMEM_LIMIT)


def _dot(a, b):
    return jnp.dot(a, b, preferred_element_type=F32)


def _dot_nt(a, b):
    return lax.dot_general(a, b, (((1,), (1,)), ((), ())), preferred_element_type=F32)


def _sigmoid(x):
    return 1.0 / (1.0 + jnp.exp(-x))


def _softplus(x):
    return jnp.maximum(x, 0.0) + jnp.log1p(jnp.exp(-jnp.abs(x)))


def _rms(x, g):
    return x * lax.rsqrt(jnp.mean(x * x, axis=-1, keepdims=True) + NORM_EPS) * g


def _mod_kernel(c_ref, w_ref, b_ref, o_ref):
    c = c_ref[...]
    s = (c * _sigmoid(c)).astype(BF16)
    o_ref[...] = _dot(s, w_ref[...].astype(BF16)) + b_ref[...]


def _modulation(cvec, w_ada, b_ada):
    depth, d, n = w_ada.shape
    tn = 1536
    return pl.pallas_call(
        _mod_kernel,
        grid=(depth, n // tn),
        in_specs=[
            pl.BlockSpec((MOD_ROWS, d), lambda l, j: (0, 0)),
            pl.BlockSpec((None, d, tn), lambda l, j: (l, 0, j)),
            pl.BlockSpec((None, 1, tn), lambda l, j: (l, 0, j)),
        ],
        out_specs=pl.BlockSpec((None, MOD_ROWS, tn), lambda l, j: (l, 0, j)),
        out_shape=jax.ShapeDtypeStruct((depth, MOD_ROWS, n), F32),
        compiler_params=_params(("arbitrary", "arbitrary")),
        name="modulation",
    )(cvec, w_ada, b_ada.reshape(depth, 1, n))


_IN_COLS = {}
_col = 0
for _name, _width in (("aq", ATT_Q), ("ak", ATT_KV), ("av", ATT_KV), ("mq", M_W), ("mk", M_W), ("mv", M_W), ("mo", M_W),
                      ("mg", M_GATES), ("rx", LRU_WIDTH), ("ry", LRU_WIDTH)):
    _IN_COLS[_name] = (_col, _col + _width)
    _col += _width
_TOKEN_MAJOR = ("ak", "mq", "mk", "mv", "mo", "rx", "ry")
_FEATURE_MAJOR = ("aq", "av", "mg")


def _split_in_weights(w):
    tok = jnp.concatenate([w[:, _IN_COLS[n][0]:_IN_COLS[n][1]] for n in _TOKEN_MAJOR], axis=1)
    feat = jnp.concatenate([w[:, _IN_COLS[n][0]:_IN_COLS[n][1]] for n in _FEATURE_MAJOR], axis=1).T
    return tok.astype(BF16), feat.astype(BF16)


def _rope_slab(x, cs, sn, first):
    swapped = jnp.where(first, pltpu.roll(x, LANES - ROPE_PAIRS, 1), pltpu.roll(x, ROPE_PAIRS, 1))
    return x * cs + swapped * sn


class _TileWalk:
    def __init__(self, nb, per_row, first_tile=0):
        self.nb, self.per_row, self.first_tile, self.n_tiles = nb, per_row, first_tile, nb * per_row

    def at(self, k, lag):
        t = jnp.clip(k - lag, 0, self.n_tiles - 1)
        return t // self.per_row, t % self.per_row + self.first_tile

    def mod_row(self, bi):
        return jnp.where(bi[1] == 0, self.nb, bi[0]), 0, 0

    def stream_specs(self, streams, lag):
        d = streams[0].shape[-1]
        at = lambda k: self.at(k, lag)
        if len(streams) == 1:
            return [pl.BlockSpec((None, TOKEN_TILE, d), lambda k: (*at(k), 0))]
        return [pl.BlockSpec((None, TOKEN_TILE, d), lambda k: (at(k)[0], 0, 0)),
                pl.BlockSpec((None, TOKEN_TILE, d), lambda k: (at(k)[0], jnp.maximum(at(k)[1] - 1, 0), 0))]

    def stream_tile(self, stream_refs, lag):
        if len(stream_refs) == 1:
            return stream_refs[0][...]
        return jnp.where(self.at(pl.program_id(0), lag)[1] == 0, stream_refs[0][...], stream_refs[1][...])


def _inproj_kernel(n_streams, *refs):
    (mod_ref, g_ref, w_ref, wt_ref, cs_ref, sn_ref, cst_ref, snt_ref,
     ak_ref, mq_ref, mk_ref, mv_ref, mo_ref, rx_ref, ry_ref, aqt_ref, avt_ref, gr_ref) = refs[n_streams:]
    x = refs[0][...] if n_streams == 1 else jnp.where(pl.program_id(1) == 0, refs[0][...], refs[1][...])
    hb = (_rms(x, g_ref[0:1, :]) * (1.0 + mod_ref[1:2, :]) + mod_ref[0:1, :]).astype(BF16)
    pt = _dot_nt(wt_ref[...], hb)
    cst = cst_ref[...]
    snt = snt_ref[...]
    rp = ROPE_PAIRS
    for hd in range(ATT_HEADS):
        xs = pt[hd * HEAD_DIM:(hd + 1) * HEAD_DIM, :]
        swapped = jnp.concatenate([xs[rp:2 * rp], xs[0:rp], xs[3 * rp:4 * rp], xs[2 * rp:3 * rp]], axis=0)
        roped = (xs * cst + swapped * snt) * Q_SCALE
        for qb in range(TOKEN_TILE // ATT_BLOCK):
            aqt_ref[qb, hd * HEAD_DIM:(hd + 1) * HEAD_DIM, :] = roped[:, qb * ATT_BLOCK:(qb + 1) * ATT_BLOCK]
    for qb in range(TOKEN_TILE // ATT_BLOCK):
        avt_ref[qb] = pt[ATT_Q:ATT_Q + ATT_KV, qb * ATT_BLOCK:(qb + 1) * ATT_BLOCK]
    gr_ref[...] = pt[ATT_Q + ATT_KV:, :]
    lane = lax.broadcasted_iota(jnp.int32, (hb.shape[0], LANES), 1)
    first = (lane & (2 * ROPE_PAIRS - 1)) < ROPE_PAIRS
    ak_ref[...] = _rope_slab(_dot(hb, w_ref[:, 0:ATT_KV]), cs_ref[...], sn_ref[...], first)
    col = ATT_KV
    for ref in (mq_ref, mk_ref, mv_ref, mo_ref, rx_ref, ry_ref):
        n = ref.shape[-1]
        ref[...] = _dot(hb, w_ref[:, col:col + n])
        col += n


def _in_projection(xs, mod, gain, w_tok, w_feat, rope):
    streams = xs if isinstance(xs, tuple) else (xs,)
    nb = streams[0].shape[0]
    t = sum(s.shape[1] for s in streams)
    nt = t // TOKEN_TILE
    d = streams[0].shape[-1]
    tile = lambda n: pl.BlockSpec((None, TOKEN_TILE, n), lambda b, i: (b, i, 0))
    const = lambda a: pl.BlockSpec(a.shape, lambda b, i: (0,) * a.ndim)
    if len(streams) == 1:
        stream_specs = [tile(d)]
    else:
        stream_specs = [pl.BlockSpec((None, TOKEN_TILE, d), lambda b, i: (b, 0, 0)),
                        pl.BlockSpec((None, TOKEN_TILE, d), lambda b, i: (b, jnp.maximum(i - 1, 0), 0))]
    width = lambda n: _IN_COLS[n][1] - _IN_COLS[n][0]
    out_shapes = [jax.ShapeDtypeStruct((nb, t, width(n)), F32) for n in _TOKEN_MAJOR]
    per_tile = TOKEN_TILE // ATT_BLOCK
    out_shapes += [jax.ShapeDtypeStruct((nb, nt * per_tile, width(n), ATT_BLOCK), F32) for n in ("aq", "av")]
    out_shapes += [jax.ShapeDtypeStruct((nb, nt, M_GATES, TOKEN_TILE), F32)]
    out_specs = [tile(width(n)) for n in _TOKEN_MAJOR]
    out_specs += [pl.BlockSpec((None, per_tile, width(n), ATT_BLOCK), lambda b, i: (b, i, 0, 0)) for n in ("aq", "av")]
    out_specs += [pl.BlockSpec((None, None, M_GATES, TOKEN_TILE), lambda b, i: (b, i, 0, 0))]
    rope_cs, rope_sn, rope_cst, rope_snt = rope
    return pl.pallas_call(
        functools.partial(_inproj_kernel, len(streams)),
        grid=(nb, nt),
        in_specs=stream_specs + [
            pl.BlockSpec((None, N_MOD, d), lambda b, i: (jnp.where(i == 0, nb, b), 0, 0)),
            const(gain), const(w_tok), const(w_feat),
            pl.BlockSpec((TOKEN_TILE, LANES), lambda b, i: (i, 0)),
            pl.BlockSpec((TOKEN_TILE, LANES), lambda b, i: (i, 0)),
            pl.BlockSpec((HEAD_DIM, TOKEN_TILE), lambda b, i: (0, i)),
            pl.BlockSpec((HEAD_DIM, TOKEN_TILE), lambda b, i: (0, i)),
        ],
        out_specs=out_specs,
        out_shape=out_shapes,
        compiler_params=_params(("parallel", "arbitrary")),
        name="in_projection",
    )(*streams, mod, gain, w_tok, w_feat, rope_cs, rope_sn, rope_cst, rope_snt)


def _attn_block(j, nblk, sink_ref, qt_ref, kp_ref, kc_ref, kn_ref, kx_ref, vp_ref, vc_ref, vn_ref, vx_ref, o_ref):
    ctx_blocks = CTX_LEN // ATT_BLOCK
    nq = ATT_BLOCK
    cols = ATT_GROUP * nq
    c = lax.broadcasted_iota(jnp.int32, (ATT_BLOCK, cols), 0)
    r = lax.broadcasted_iota(jnp.int32, (ATT_BLOCK, cols), 1) & (nq - 1)
    m_prev = (c >= r) & (j >= ctx_blocks + 1)
    m_next = (c <= r) & (j >= ctx_blocks) & (j <= nblk - 2)
    is_lat = j >= ctx_blocks
    col = lax.broadcasted_iota(jnp.int32, (1, cols), 1)
    neg = -jnp.inf
    kp = kp_ref[...].astype(BF16)
    kc = kc_ref[...].astype(BF16)
    kn = kn_ref[...].astype(BF16)
    kx = kx_ref[...].astype(BF16)
    zeros = jnp.zeros((HEAD_DIM, cols), BF16)
    scores = []
    for kh in range(ATT_KV_HEADS):
        hd0 = kh * ATT_GROUP
        qg = jnp.concatenate([qt_ref[(hd0 + g) * HEAD_DIM:(hd0 + g + 1) * HEAD_DIM, :] for g in range(ATT_GROUP)],
                             axis=1).astype(BF16)
        rhs = jnp.concatenate([qg, zeros] if kh == 0 else [zeros, qg], axis=0)
        s_x = _dot(kx, rhs)
        scores.append((jnp.where(m_prev, _dot(kp, rhs), neg), jnp.where(is_lat, _dot(kc, rhs), neg),
                       jnp.where(m_next, _dot(kn, rhs), neg), s_x[:ATT_BLOCK], s_x[ATT_BLOCK:]))
    yield
    for kh in range(ATT_KV_HEADS):
        hd0 = kh * ATT_GROUP
        sink = jnp.full((1, cols), sink_ref[hd0], F32)
        for g in range(1, ATT_GROUP):
            sink = jnp.where(col >= g * nq, sink_ref[hd0 + g], sink)
        sink = sink * LOG2_E
        s_p, s_c, s_n, s_x0, s_x1 = scores[kh]
        m_el = jnp.maximum(jnp.maximum(jnp.maximum(s_p, s_c), jnp.maximum(s_n, s_x0)), s_x1)
        m = jnp.maximum(jnp.max(m_el, axis=0, keepdims=True), sink)
        p_p = jnp.exp2(s_p - m)
        p_c = jnp.exp2(s_c - m)
        p_n = jnp.exp2(s_n - m)
        p_x0 = jnp.exp2(s_x0 - m)
        p_x1 = jnp.exp2(s_x1 - m)
        den = jnp.sum((p_p + p_c) + (p_n + p_x0) + p_x1, axis=0, keepdims=True) + jnp.exp2(sink - m)
        vs = slice(kh * HEAD_DIM, (kh + 1) * HEAD_DIM)
        o = (_dot(vp_ref[vs, :].astype(BF16), p_p.astype(BF16)) + _dot(vc_ref[vs, :].astype(BF16), p_c.astype(BF16))
             + _dot(vn_ref[vs, :].astype(BF16), p_n.astype(BF16))
             + _dot(vx_ref[0, vs, :].astype(BF16), p_x0.astype(BF16)) + _dot(vx_ref[1, vs, :].astype(BF16), p_x1.astype(BF16)))
        o = o / den
        for pair in range(ATT_GROUP // 2):
            two = jnp.concatenate([o[:, (2 * pair) * nq:(2 * pair + 1) * nq], o[:, (2 * pair + 1) * nq:(2 * pair + 2) * nq]],
                                  axis=0)
            lo = (hd0 + 2 * pair) * HEAD_DIM
            o_ref[:, lo:lo + 2 * HEAD_DIM] = two.T


def _seq_block(d, j, nblk):
    return jnp.where((d == 0) | (j == 0), j, nblk - j)


def _lane_scan(x, op, fill, rev):
    n = x.shape[-1]
    lane = lax.broadcasted_iota(jnp.int32, x.shape, 1)
    step = 1
    while step < n:
        if rev:
            shifted = jnp.where(lane < n - step, pltpu.roll(x, n - step, 1), fill)
        else:
            shifted = jnp.where(lane >= step, pltpu.roll(x, step, 1), fill)
        x = op(x, shifted)
        step *= 2
    return x


def _split3(x):
    hi = x.astype(BF16).astype(F32)
    mid = (x - hi).astype(BF16).astype(F32)
    lo = x - hi - mid
    return hi, mid, lo


def _mlstm_tables():
    H, dh = MLSTM_HEADS, MLSTM_HEAD_DIM
    n_pairs = H // 2
    n_slabs = H + 3 * n_pairs
    sel = np.zeros((LANES, n_slabs * LANES), np.float32)
    for part in range(3):
        base = part * 4 * H
        for h in range(H):
            sel[base + h, h * LANES:(h + 1) * LANES] = 1.0
        for qty in range(3):
            for p in range(n_pairs):
                slab = H + qty * n_pairs + p
                for half in range(2):
                    row = base + H * qty + 2 * p + half
                    sel[row, slab * LANES + half * dh:slab * LANES + (half + 1) * dh] = 1.0
    ones_bd = np.zeros((2 * TOKEN_TILE, LANES), np.float32)
    ones_bd[:TOKEN_TILE, :dh] = 1.0
    ones_bd[TOKEN_TILE:, dh:] = 1.0
    avg = np.zeros((3 * LANES, LANES), np.float32)
    for part in range(3):
        for half in range(2):
            avg[part * LANES + half * dh:part * LANES + (half + 1) * dh, half * dh:(half + 1) * dh] = 1.0 / dh
    return jnp.asarray(sel, BF16), jnp.asarray(ones_bd, BF16), jnp.asarray(avg, BF16)


def _mlstm_block_terms(rev, g, tri_ref):
    L = TOKEN_TILE
    H = MLSTM_HEADS
    ioff = 2 * H if rev else 0
    li = g[ioff:ioff + H, :]
    lf = -_softplus(-g[ioff + H:ioff + 2 * H, :])
    parts = _dot(jnp.concatenate(_split3(lf), axis=0).astype(BF16), tri_ref[...])
    b = parts[0:H] + parts[H:2 * H] + parts[2 * H:3 * H]
    r = li - b
    cm = _lane_scan(r, jnp.maximum, -jnp.inf, rev)
    last = 0 if rev else L - 1
    g_tot = b[:, last:last + 1]
    a = g_tot - b + li
    m_loc = jnp.max(a, axis=1, keepdims=True)
    w = jnp.exp(a - m_loc)
    return r, cm, b, w, g_tot, m_loc


def _mlstm_stash_terms(terms, sel_ref, cols_ref, r_ref, gm_ref, slot):
    L = TOKEN_TILE
    H, dh = MLSTM_HEADS, MLSTM_HEAD_DIM
    r, cm, b, w, g_tot, m_loc = terms
    stacked = jnp.concatenate([cm, b, w, jnp.zeros((H, L), F32)], axis=0)
    hi, mid, lo = _split3(stacked)
    pad = jnp.zeros((LANES - 3 * 4 * H, L), F32)
    cols_ref[slot] = _dot(jnp.concatenate([hi, mid, lo, pad], axis=0).T.astype(BF16), sel_ref[...])
    r_ref[slot] = jnp.concatenate([r, jnp.zeros((SUBLANES - H, L), F32)], axis=0)
    lane_row = lax.broadcasted_iota(jnp.int32, (1, LANES), 1)

    def layout(col):
        rows = [jnp.broadcast_to(col, (H, LANES))]
        rows += [jnp.where(lane_row < dh, col[2 * p:2 * p + 1, :], col[2 * p + 1:2 * p + 2, :]) for p in range(H // 2)]
        rows += [jnp.zeros((SUBLANES - H - H // 2, LANES), F32)]
        return jnp.concatenate(rows, axis=0)

    gm_ref[slot] = jnp.concatenate([layout(g_tot), layout(m_loc)], axis=0)


def _mlstm_kernel(q_ref, k_ref, v_ref, o_ref, gr_ref, gnext_ref, gb_ref, ng_ref, tri_ref, sel_ref, ones_ref, avg_ref, out_ref,
                  hf_ref, c_ref, n_ref, m_ref, cols_ref, r_ref, gm_ref, rev, init=False):
    d = pl.program_id(1)
    j = pl.program_id(2)
    nblk = pl.num_programs(2)
    L = TOKEN_TILE
    H, dh = MLSTM_HEADS, MLSTM_HEAD_DIM
    n_pairs = H // 2
    row0 = pl.multiple_of(_seq_block(d, j, nblk) * L, L)
    slot = j & 1

    if init:
        c_ref[...] = jnp.zeros_like(c_ref)
        n_ref[...] = jnp.zeros_like(n_ref)
        m_ref[...] = jnp.zeros_like(m_ref)
        _mlstm_stash_terms(_mlstm_block_terms(rev, gr_ref[...] + gb_ref[...], tri_ref), sel_ref, cols_ref, r_ref, gm_ref, 0)
        return

    def body(rev):
        next_terms = _mlstm_block_terms(rev, gnext_ref[...] + gb_ref[...], tri_ref)

        r = r_ref[slot]
        gm = gm_ref[slot]
        slab = lambda i: cols_ref[slot, :, i * LANES:(i + 1) * LANES]
        m0 = m_ref[...]
        g_tot, m_loc = gm[0:SUBLANES], gm[SUBLANES:]
        m_new = jnp.maximum(g_tot + m0, m_loc)
        f_prev = jnp.exp(g_tot + m0 - m_new)
        f_loc = jnp.exp(m_loc - m_new)
        m_ref[...] = m_new
        lane = lax.broadcasted_iota(jnp.int32, (L, LANES), 1)
        first_head = lane < dh
        sq = (LANES, LANES)
        tri_vis = ((lax.broadcasted_iota(jnp.int32, sq, 1) >= lax.broadcasted_iota(jnp.int32, sq, 0)) if rev
                   else (lax.broadcasted_iota(jnp.int32, sq, 1) <= lax.broadcasted_iota(jnp.int32, sq, 0)))
        kt = (k_ref[...] * (dh ** -0.5)).T.astype(BF16)
        zeros_half = jnp.zeros((dh, L), BF16)
        bd_mask = ((lax.broadcasted_iota(jnp.int32, (LANES, LANES), 0) < dh)
                   == (lax.broadcasted_iota(jnp.int32, (LANES, LANES), 1) < dh))
        for p in range(n_pairs):
            ps = slice(p * LANES, (p + 1) * LANES)
            qb = q_ref[:, ps].astype(BF16)
            v = v_ref[:, ps]
            kt_pair = kt[ps, :]
            sm = []
            for half in range(2):
                h = 2 * p + half
                kt_h = kt[h * dh:(h + 1) * dh, :]
                kt_masked = jnp.concatenate([kt_h, zeros_half] if half == 0 else [zeros_half, kt_h], axis=0)
                s = _dot(qb, kt_masked)
                m_b = jnp.maximum(slab(h), m0[h:h + 1, :])
                for lt in range(L // LANES):
                    quads = []
                    for qt in range(L // LANES):
                        rows = slice(qt * LANES, (qt + 1) * LANES)
                        if (lt < qt) if rev else (lt > qt):
                            quads.append(jnp.zeros((LANES, LANES), BF16))
                            continue
                        expo = r[h:h + 1, lt * LANES:(lt + 1) * LANES] - m_b[rows]
                        if lt == qt:
                            expo = jnp.where(tri_vis, expo, -jnp.inf)
                        quads.append((s[rows, lt * LANES:(lt + 1) * LANES] * jnp.exp(expo)).astype(BF16))
                    sm.append(jnp.concatenate(quads, axis=0))
            sm = jnp.concatenate(sm, axis=1)
            v_bd = jnp.concatenate([jnp.where(first_head, v, 0.0), jnp.where(first_head, 0.0, v)], axis=0).astype(BF16)
            m0_pair = m0[H + p:H + p + 1, :]
            m_pair = jnp.maximum(slab(H + p), m0_pair)
            f_pair = jnp.exp(m0_pair - m_pair)
            num = _dot(sm, v_bd) + f_pair * _dot(qb, c_ref[p].astype(BF16))
            den = _dot(sm, ones_ref[...]) + f_pair * _dot(qb, n_ref[p].astype(BF16))
            hh = num / jnp.maximum(jnp.abs(den), jnp.exp(-(slab(H + n_pairs + p) + m_pair)))
            w_pair = slab(H + 2 * n_pairs + p)
            c_loc = jnp.where(bd_mask, _dot(kt_pair, (w_pair * v).astype(BF16)), 0.0)
            n_loc = jnp.where(bd_mask, _dot(kt_pair, w_pair.astype(BF16)), 0.0)
            c_ref[p] = f_prev[H + p:H + p + 1, :] * c_ref[p] + f_loc[H + p:H + p + 1, :] * c_loc
            n_ref[p] = f_prev[H + p:H + p + 1, :] * n_ref[p] + f_loc[H + p:H + p + 1, :] * n_loc
            if not rev:
                hf_ref[pl.ds(row0, L), ps] = hh
            else:
                ht = hf_ref[pl.ds(row0, L), ps] + hh
                mu = _dot(jnp.concatenate(_split3(ht), axis=1).astype(BF16), avg_ref[...])
                cen = ht - mu
                var = _dot(jnp.concatenate(_split3(cen * cen), axis=1).astype(BF16), avg_ref[...])
                y = cen * lax.rsqrt(var + NORM_EPS) * ng_ref[:, ps]
                out_ref[:, ps] = _sigmoid(o_ref[:, ps]) * y
        _mlstm_stash_terms(next_terms, sel_ref, cols_ref, r_ref, gm_ref, 1 - slot)

    body(rev)


def _mlstm_specs(mq, mk, mv, mo, grow, gate_b, norm_g):
    nb, t, w = mq.shape
    nblk = t // TOKEN_TILE
    blk = lambda b, d, j: (b, _seq_block(d, j, nblk), 0)
    tile = pl.BlockSpec((None, TOKEN_TILE, w), blk)
    gb = gate_b.reshape(M_GATES, 1)
    ng = norm_g.reshape(1, w)
    sel, ones_bd, avg = _mlstm_tables()
    upper = np.triu(np.ones((TOKEN_TILE, TOKEN_TILE), np.float32))
    tri = jnp.asarray(np.stack([upper, upper.T]), BF16)
    const = lambda a: pl.BlockSpec(a.shape, lambda b, d, j: (0,) * a.ndim)
    n_pairs = MLSTM_HEADS // 2
    in_specs = [
        tile, tile, tile, tile,
        pl.BlockSpec((None, None, M_GATES, TOKEN_TILE), lambda b, d, j: (b, _seq_block(d, j, nblk), 0, 0)),
        pl.BlockSpec((None, None, M_GATES, TOKEN_TILE),
                     lambda b, d, j: (b, _seq_block(d, jnp.minimum(j + 1, nblk - 1), nblk), 0, 0)),
        const(gb), const(ng),
        pl.BlockSpec((None, TOKEN_TILE, TOKEN_TILE), lambda b, d, j: (d, 0, 0)),
        const(sel), const(ones_bd), const(avg),
    ]
    args = [mq, mk, mv, mo, grow, grow, gb, ng, tri, sel, ones_bd, avg]
    out_spec = pl.BlockSpec((None, TOKEN_TILE, w), lambda b, d, j: (b, jnp.where(d == 0, 0, _seq_block(d, j, nblk)), 0))
    scratch = [
        pltpu.VMEM((t, w), F32),
        pltpu.VMEM((n_pairs, LANES, LANES), F32),
        pltpu.VMEM((n_pairs, LANES, LANES), F32),
        pltpu.VMEM((SUBLANES, LANES), F32),
        pltpu.VMEM((2, TOKEN_TILE, sel.shape[1]), F32),
        pltpu.VMEM((2, SUBLANES, TOKEN_TILE), F32),
        pltpu.VMEM((2, 2 * SUBLANES, LANES), F32),
    ]
    return in_specs, args, out_spec, jax.ShapeDtypeStruct((nb, t, w), F32), scratch


def _lru_kernel(x_ref, xp_ref, xn_ref, y_ref, cw_ref, cb_ref, gw_ref, gb_ref, lam_ref, out_ref,
                hf_ref, xe_ref, carry_ref, a_ref, u_ref, hb_ref, hl_ref, al_ref, rev, init=False):
    d = pl.program_id(1)
    j = pl.program_id(2)
    nblk = pl.num_programs(2)
    L = TOKEN_TILE
    pos = _seq_block(d, j, nblk)
    row0 = pl.multiple_of(pos * L, L)

    if init:
        carry_ref[...] = jnp.zeros_like(carry_ref)
        return

    has_prev = pos >= 2
    has_next = (pos >= 1) & (pos <= nblk - 2)
    xe_ref[0:SUBLANES, :] = jnp.where(has_prev, xp_ref[...], 0.0)
    xe_ref[SUBLANES:SUBLANES + L, :] = x_ref[...]
    xe_ref[SUBLANES + L:, :] = jnp.where(has_next, xn_ref[...], 0.0)
    seq = cb_ref[...]
    for tap in range(CONV_WIDTH):
        off = SUBLANES - CONV_LEFT + tap
        seq = seq + cw_ref[tap:tap + 1, :] * xe_ref[off:off + L, :]
    sb = seq.astype(BF16)
    r = 0.5 + 0.5 * jnp.tanh(0.5 * (_dot(sb, gw_ref[0]) + gb_ref[0:1, :]))
    i = 0.5 + 0.5 * jnp.tanh(0.5 * (_dot(sb, gw_ref[1]) + gb_ref[1:2, :]))
    log_a = (-LRU_C * _softplus(-lam_ref[...])) * r
    a0 = jnp.exp(log_a)
    th = jnp.tanh(log_a)
    u0 = jnp.sqrt(-2.0 * th) * lax.rsqrt(1.0 - th) * (i * seq)
    n_lt = LRU_WIDTH // LANES
    for lt in range(n_lt):
        a_ref[lt] = a0[:, lt * LANES:(lt + 1) * LANES]
        u_ref[lt] = u0[:, lt * LANES:(lt + 1) * LANES]
    seg = L // (LRU_CHAINS * SUBLANES)
    sub = lax.broadcasted_iota(jnp.int32, (SUBLANES, LRU_WIDTH), 0)
    rows = lambda c, i: pl.ds(c * seg * SUBLANES + i, SUBLANES, stride=seg)
    strided = lambda ref, c, i: jnp.concatenate([ref[lt, rows(c, i), :] for lt in range(n_lt)], axis=1)

    def scan(rev):
        steps = range(seg - 1, -1, -1) if rev else range(seg)
        chains = range(LRU_CHAINS - 1, -1, -1) if rev else range(LRU_CHAINS)
        edge = SUBLANES - 1 if rev else 0
        shift = lambda x, n: pltpu.roll(x, SUBLANES - n if rev else n, 0)
        ends, prods = {}, {}
        for c in chains:
            h = jnp.zeros((SUBLANES, LRU_WIDTH), F32)
            ac = jnp.ones((SUBLANES, LRU_WIDTH), F32)
            for i in steps:
                ai = strided(a_ref, c, i)
                h = ai * h + strided(u_ref, c, i)
                ac = ai * ac
                hl_ref[c * seg + i] = h
                al_ref[c * seg + i] = ac
            step = 1
            while step < SUBLANES:
                ok = (sub < SUBLANES - step) if rev else (sub >= step)
                h = jnp.where(ok, ac * shift(h, step) + h, h)
                ac = jnp.where(ok, ac * shift(ac, step), ac)
                step *= 2
            ends[c], prods[c] = h, ac
        carry = carry_ref[...]
        for c in chains:
            true_ends = ends[c] + prods[c] * carry
            carry_in = jnp.where(sub == edge, carry, shift(true_ends, 1))
            carry = true_ends[SUBLANES - 1 - edge:SUBLANES - edge, :]
            for i in range(seg):
                hi = hl_ref[c * seg + i] + al_ref[c * seg + i] * carry_in
                for lt in range(n_lt):
                    hb_ref[lt, rows(c, i), :] = hi[:, lt * LANES:(lt + 1) * LANES]
        carry_ref[...] = carry
        return jnp.concatenate([hb_ref[lt] for lt in range(n_lt)], axis=1)

    def forward():
        hf_ref[pl.ds(row0, L), :] = scan(False)

    def backward():
        h = hf_ref[pl.ds(row0, L), :] + scan(True)
        y = y_ref[...]
        gelu = 0.5 * y * (1.0 + jnp.tanh(np.sqrt(2.0 / np.pi).astype(np.float32) * (y + 0.044715 * (y * y * y))))
        out_ref[...] = h * gelu

    backward() if rev else forward()


def _rglru_specs(rx, ry, conv_w, conv_b, gate_w, gate_b, lam):
    nb, t, w = rx.shape
    nblk = t // TOKEN_TILE
    per_tile = TOKEN_TILE // SUBLANES
    n8 = t // SUBLANES
    blk = lambda b, d, j: (b, _seq_block(d, j, nblk), 0)
    tile = pl.BlockSpec((None, TOKEN_TILE, w), blk)
    halo = lambda f: pl.BlockSpec((None, SUBLANES, w), lambda b, d, j: (b, f(_seq_block(d, j, nblk)), 0))
    prev8 = lambda p: jnp.maximum(p * per_tile - 1, 0)
    next8 = lambda p: jnp.minimum((p + 1) * per_tile, n8 - 1)
    eye = jnp.eye(LRU_BLOCKS, dtype=gate_w.dtype)
    gw = jnp.einsum('dgnij,nm->dgnimj', gate_w, eye).reshape(2, 2, w, w).astype(BF16)
    cb = conv_b.reshape(1, w)
    in_specs = [
        tile, halo(prev8), halo(next8), tile,
        pl.BlockSpec(conv_w.shape, lambda b, d, j: (0, 0)),
        pl.BlockSpec(cb.shape, lambda b, d, j: (0, 0)),
        pl.BlockSpec((None, 2, w, w), lambda b, d, j: (d, 0, 0, 0)),
        pl.BlockSpec((None, 2, w), lambda b, d, j: (d, 0, 0)),
        pl.BlockSpec((None, 1, w), lambda b, d, j: (d, 0, 0)),
    ]
    args = [rx, rx, rx, ry, conv_w, cb, gw, gate_b, lam.reshape(2, 1, w)]
    out_spec = pl.BlockSpec((None, TOKEN_TILE, w), lambda b, d, j: (b, jnp.where(d == 0, 0, _seq_block(d, j, nblk)), 0))
    scratch = [
        pltpu.VMEM((t, w), F32),
        pltpu.VMEM((TOKEN_TILE + 2 * SUBLANES, w), F32),
        pltpu.VMEM((1, w), F32),
        pltpu.VMEM((w // LANES, TOKEN_TILE, LANES), F32),
        pltpu.VMEM((w // LANES, TOKEN_TILE, LANES), F32),
        pltpu.VMEM((w // LANES, TOKEN_TILE, LANES), F32),
        pltpu.VMEM((TOKEN_TILE // SUBLANES, SUBLANES, w), F32),
        pltpu.VMEM((TOKEN_TILE // SUBLANES, SUBLANES, w), F32),
    ]
    return in_specs, args, out_spec, jax.ShapeDtypeStruct((nb, t, w), F32), scratch


def _mixers_kernel(n_mem, n_rec, n_mem_scratch, *refs):
    d = pl.program_id(1)
    j = pl.program_id(2)
    n_in = n_mem + n_rec
    mem_in, rec_in = refs[:n_mem], refs[n_mem:n_in]
    mem_out, rec_out = refs[n_in:n_in + 2]
    mem_scratch = refs[n_in + 2:n_in + 2 + n_mem_scratch]
    rec_scratch = refs[n_in + 2 + n_mem_scratch:]

    def body(rev):
        @pl.when(j == 0)
        def _():
            _mlstm_kernel(*mem_in, mem_out, *mem_scratch, rev=rev, init=True)
            _lru_kernel(*rec_in, rec_out, *rec_scratch, rev=rev, init=True)

        _mlstm_kernel(*mem_in, mem_out, *mem_scratch, rev=rev)
        _lru_kernel(*rec_in, rec_out, *rec_scratch, rev=rev)

    pl.when(d == 0)(functools.partial(body, False))
    pl.when(d == 1)(functools.partial(body, True))


def _mixers(mlstm_args, rglru_args):
    nb, t, _ = mlstm_args[0].shape
    mem_specs, mem_args, mem_out, mem_shape, mem_scratch = _mlstm_specs(*mlstm_args)
    rec_specs, rec_args, rec_out, rec_shape, rec_scratch = _rglru_specs(*rglru_args)
    return pl.pallas_call(
        functools.partial(_mixers_kernel, len(mem_args), len(rec_args), len(mem_scratch)),
        grid=(nb, 2, t // TOKEN_TILE),
        in_specs=mem_specs + rec_specs,
        out_specs=[mem_out, rec_out],
        out_shape=[mem_shape, rec_shape],
        scratch_shapes=mem_scratch + rec_scratch,
        compiler_params=_params(("parallel", "arbitrary", "arbitrary")),
        name="mixers",
    )(*mem_args, *rec_args)


def _outffn_kernel(n_streams, walk, n_att_blocks, *refs):
    k = pl.program_id(0)
    streams = refs[:n_streams]
    (sink_ref, qt_ref, kp_ref, kt_ref, kn_ref, kx_ref, vp_ref, vt_ref, vn_ref, vx_ref,
     mem_ref, rec_ref, mod_ref, modf_ref, g_ref, wo_ref, wi_ref, wd_ref, out_ref,
     act_ref, a0_ref, a1_ref, h0_ref, h1_ref, x0_ref, x1_ref) = refs[n_streams:]

    @pl.when(k == 0)
    def _():
        a1_ref[...] = jnp.zeros_like(a1_ref)
        h0_ref[...] = jnp.zeros_like(h0_ref)
        x0_ref[...] = jnp.zeros_like(x0_ref)

    def body(att_w, att_r, h_w, x_w, h_r, x_r):
        q0 = (TOKEN_TILE // ATT_BLOCK) * walk.at(k, 0)[1]
        lo, hi = pl.ds(0, ATT_BLOCK), pl.ds(ATT_BLOCK, ATT_BLOCK)
        att_blocks = [
            _attn_block(q0, n_att_blocks, sink_ref, qt_ref.at[0], kp_ref, kt_ref.at[lo], kt_ref.at[hi], kx_ref,
                        vp_ref, vt_ref.at[0], vt_ref.at[1], vx_ref, att_w.at[lo]),
            _attn_block(q0 + 1, n_att_blocks, sink_ref, qt_ref.at[1], kt_ref.at[lo], kt_ref.at[hi], kn_ref, kx_ref,
                        vt_ref.at[0], vt_ref.at[1], vn_ref, vx_ref, att_w.at[hi])]
        for blk in att_blocks:
            next(blk)
        x = walk.stream_tile(streams, 1)
        mix = (_dot(att_r[...].astype(BF16), wo_ref[0:ATT_Q, :])
               + _dot(mem_ref[...].astype(BF16), wo_ref[ATT_Q:ATT_Q + M_W, :])
               + _dot(rec_ref[...].astype(BF16), wo_ref[ATT_Q + M_W:, :]))
        x1 = x + mod_ref[2:3, :] * _rms(mix, g_ref[1:2, :])
        x_w[...] = x1
        h_w[...] = (_rms(x1, g_ref[2:3, :]) * (1.0 + mod_ref[4:5, :]) + mod_ref[3:4, :]).astype(BF16)
        h = h_r[...]
        for c0 in range(0, D_FF, FF_CHUNK):
            gate = _dot(h, wi_ref[:, c0:c0 + FF_CHUNK])
            up = _dot(h, wi_ref[:, D_FF + c0:D_FF + c0 + FF_CHUNK])
            half = 0.5 * gate
            act_ref[:, c0:c0 + FF_CHUNK] = ((half + half * jnp.tanh(half)) * up).astype(BF16)
        for blk in att_blocks:
            for _ in blk:
                pass
        f = _dot(act_ref[...], wd_ref[...])
        out_ref[...] = x_r[...] + modf_ref[5:6, :] * _rms(f, g_ref[3:4, :])

    pl.when((k & 1) == 0)(functools.partial(body, a0_ref, a1_ref, h1_ref, x1_ref, h0_ref, x0_ref))
    pl.when((k & 1) == 1)(functools.partial(body, a1_ref, a0_ref, h0_ref, x0_ref, h1_ref, x1_ref))


def _attn_out_ffn(xs, aqt, ak, avt, sink, mem, rec, mod, gain, w_out, w_ffn_in, w_down, first_tile=0):
    streams = xs if isinstance(xs, tuple) else (xs,)
    assert len(streams) == 1 or first_tile == 0
    nb, _, d = streams[0].shape
    t = ak.shape[1]
    per_row = t // TOKEN_TILE - first_tile
    walk = _TileWalk(nb, per_row, first_tile)
    n_att_blocks = t // ATT_BLOCK
    per_tile = TOKEN_TILE // ATT_BLOCK
    ctx_blocks = CTX_LEN // ATT_BLOCK
    s0 = lambda k: walk.at(k, 0)
    s1 = lambda k: walk.at(k, 1)
    s2 = lambda k: walk.at(k, 2)
    before = lambda k: jnp.maximum(per_tile * s0(k)[1] - 1, 0)
    after = lambda k: jnp.minimum(per_tile * (s0(k)[1] + 1), n_att_blocks - 1)
    tile1 = lambda n: pl.BlockSpec((None, TOKEN_TILE, n), lambda k: (*s1(k), 0))
    const = lambda a: pl.BlockSpec(a.shape, lambda k: (0,) * a.ndim, pipeline_mode=pl.Buffered(1))
    att_specs = [
        pl.BlockSpec(memory_space=pltpu.SMEM),
        pl.BlockSpec((None, per_tile, ATT_Q, ATT_BLOCK), lambda k: (*s0(k), 0, 0)),
        pl.BlockSpec((None, ATT_BLOCK, ATT_KV), lambda k: (s0(k)[0], before(k), 0)),
        pl.BlockSpec((None, TOKEN_TILE, ATT_KV), lambda k: (*s0(k), 0)),
        pl.BlockSpec((None, ATT_BLOCK, ATT_KV), lambda k: (s0(k)[0], after(k), 0)),
        pl.BlockSpec((None, CTX_LEN, ATT_KV), lambda k: (s0(k)[0], 0, 0)),
        pl.BlockSpec((None, None, ATT_KV, ATT_BLOCK), lambda k: (s0(k)[0], before(k), 0, 0)),
        pl.BlockSpec((None, per_tile, ATT_KV, ATT_BLOCK), lambda k: (*s0(k), 0, 0)),
        pl.BlockSpec((None, None, ATT_KV, ATT_BLOCK), lambda k: (s0(k)[0], after(k), 0, 0)),
        pl.BlockSpec((None, ctx_blocks, ATT_KV, ATT_BLOCK), lambda k: (s0(k)[0], 0, 0, 0)),
    ]
    return pl.pallas_call(
        functools.partial(_outffn_kernel, len(streams), walk, n_att_blocks),
        grid=(walk.n_tiles + 2,),
        in_specs=walk.stream_specs(streams, 1) + att_specs + [
            tile1(M_W), tile1(LRU_WIDTH),
            pl.BlockSpec((None, N_MOD, d), lambda k: walk.mod_row(s1(k))),
            pl.BlockSpec((None, N_MOD, d), lambda k: walk.mod_row(s2(k))),
            const(gain), const(w_out), const(w_ffn_in), const(w_down),
        ],
        out_specs=pl.BlockSpec((None, TOKEN_TILE, d), lambda k: (s2(k)[0], s2(k)[1] - first_tile, 0)),
        out_shape=jax.ShapeDtypeStruct((nb, per_row * TOKEN_TILE, d), F32),
        scratch_shapes=[pltpu.VMEM((TOKEN_TILE, D_FF), BF16),
                        pltpu.VMEM((TOKEN_TILE, ATT_Q), F32), pltpu.VMEM((TOKEN_TILE, ATT_Q), F32),
                        pltpu.VMEM((TOKEN_TILE, d), BF16), pltpu.VMEM((TOKEN_TILE, d), BF16),
                        pltpu.VMEM((TOKEN_TILE, d), F32), pltpu.VMEM((TOKEN_TILE, d), F32)],
        compiler_params=_params(("arbitrary",)),
        name="attn_out_ffn",
    )(*streams, sink, aqt, ak, ak, ak, ak, avt, avt, avt, avt, mem, rec, mod, mod, gain, w_out, w_ffn_in, w_down)


def _rope_tables(n_lat):
    t = jnp.arange(n_lat)
    row = (t // GRID_W).astype(F32)
    col = (t % GRID_W).astype(F32)
    freqs = ROPE_BASE ** (-jnp.arange(ROPE_PAIRS, dtype=F32) / ROPE_PAIRS)
    ang_r = row[:, None] * freqs
    ang_c = col[:, None] * freqs
    cs = jnp.concatenate([jnp.cos(ang_r), jnp.cos(ang_r), jnp.cos(ang_c), jnp.cos(ang_c)], axis=-1)
    sn = jnp.concatenate([-jnp.sin(ang_r), jnp.sin(ang_r), -jnp.sin(ang_c), jnp.sin(ang_c)], axis=-1)
    cs = jnp.concatenate([jnp.ones((CTX_LEN, HEAD_DIM), F32), cs], axis=0)
    sn = jnp.concatenate([jnp.zeros((CTX_LEN, HEAD_DIM), F32), sn], axis=0)
    return jnp.tile(cs, (1, LANES // HEAD_DIM)), jnp.tile(sn, (1, LANES // HEAD_DIM)), cs.T, sn.T


def kernel(x, c, ctx, c_ctx, w_ada, b_ada, norm_gain, w_in, w_out, attn_sink, mlstm_gate_b, mlstm_norm, conv_w, conv_b,
           lru_gate_w, lru_gate_b, lru_lam, w_ffn_in, w_ffn_out):
    nb, n_lat, d = x.shape
    depth = w_ada.shape[0]
    assert ctx.shape[1] == CTX_LEN and n_lat % TOKEN_TILE == 0 and nb < MOD_ROWS
    cvec = jnp.concatenate([c, c_ctx[None, :], jnp.zeros((MOD_ROWS - nb - 1, d), F32)], axis=0)
    mod = _modulation(cvec, w_ada, b_ada).reshape(depth, MOD_ROWS, N_MOD, d)
    rope = _rope_tables(n_lat)
    xs = (ctx, x) if depth > 1 else jnp.concatenate([ctx, x], axis=1)
    for l in range(depth):
        w_tok, w_feat = _split_in_weights(w_in[l])
        ak, mq, mk, mv, mo, rx, ry, aqt, avt, grow = _in_projection(xs, mod[l], norm_gain[l], w_tok, w_feat, rope)
        mem, rec = _mixers((mq, mk, mv, mo, grow, mlstm_gate_b[l], mlstm_norm[l]),
                           (rx, ry, conv_w[l], conv_b[l], lru_gate_w[l], lru_gate_b[l], lru_lam[l]))
        xs = _attn_out_ffn(xs, aqt, ak, avt, attn_sink[l], mem, rec, mod[l], norm_gain[l], w_out[l].astype(BF16),
                           w_ffn_in[l].astype(BF16), w_ffn_out[l].astype(BF16), first_tile=int(l == depth - 1))
    return xs
```

```python
import functools

import jax
import jax.numpy as jnp
import numpy as np
from jax import lax
from jax.experimental import pallas as pl
from jax.experimental.pallas import tpu as pltpu

F32 = jnp.float32
BF16 = jnp.bfloat16

D_MODEL = 1024
GRID_W = 64
CTX_LEN = 256
N_MOD = 6
NORM_EPS = 1e-6
ATT_HEADS = 8
ATT_KV_HEADS = 2
ATT_GROUP = ATT_HEADS // ATT_KV_HEADS
HEAD_DIM = 64
WINDOW = 128
ATT_BLOCK = 128
ROPE_BASE = 10000.0
ROPE_PAIRS = HEAD_DIM // 4
ATT_Q = ATT_HEADS * HEAD_DIM
ATT_KV = ATT_KV_HEADS * HEAD_DIM
MLSTM_HEADS = 4
MLSTM_HEAD_DIM = 64
M_W = MLSTM_HEADS * MLSTM_HEAD_DIM
M_GATES = 4 * MLSTM_HEADS
LRU_WIDTH = 256
LRU_BLOCKS = 4
LRU_BW = LRU_WIDTH // LRU_BLOCKS
LRU_C = 8.0
CONV_WIDTH = 4
CONV_LEFT = CONV_WIDTH // 2
D_FF = -(-8 * D_MODEL // (3 * 256)) * 256
LOG2_E = float(np.log2(np.e))
Q_SCALE = HEAD_DIM ** -0.5 * LOG2_E

LANES = 128
SUBLANES = 8
TOKEN_TILE = CTX_LEN
FF_CHUNK = 256
LRU_CHAINS = 8
MOD_ROWS = 16
VMEM_LIMIT = 56 * 1024 * 1024


def _params(sem):
    return pltpu.CompilerParams(dimension_semantics=sem, vmem_limit_bytes=VMEM_LIMIT)


def _dot(a, b):
    return jnp.dot(a, b, preferred_element_type=F32)


def _dot_nt(a, b):
    return lax.dot_general(a, b, (((1,), (1,)), ((), ())), preferred_element_type=F32)


def _sigmoid(x):
    return 1.0 / (1.0 + jnp.exp(-x))


def _softplus(x):
    return jnp.maximum(x, 0.0) + jnp.log1p(jnp.exp(-jnp.abs(x)))


def _rms(x, g):
    return x * lax.rsqrt(jnp.mean(x * x, axis=-1, keepdims=True) + NORM_EPS) * g


def _mod_kernel(c_ref, w_ref, b_ref, o_ref):
    c = c_ref[...]
    s = (c * _sigmoid(c)).astype(BF16)
    o_ref[...] = _dot(s, w_ref[...].astype(BF16)) + b_ref[...]


def _modulation(cvec, w_ada, b_ada):
    depth, d, n = w_ada.shape
    tn = 1536
    return pl.pallas_call(
        _mod_kernel,
        grid=(depth, n // tn),
        in_specs=[
            pl.BlockSpec((MOD_ROWS, d), lambda l, j: (0, 0)),
            pl.BlockSpec((None, d, tn), lambda l, j: (l, 0, j)),
            pl.BlockSpec((None, 1, tn), lambda l, j: (l, 0, j)),
        ],
        out_specs=pl.BlockSpec((None, MOD_ROWS, tn), lambda l, j: (l, 0, j)),
        out_shape=jax.ShapeDtypeStruct((depth, MOD_ROWS, n), F32),
        compiler_params=_params(("arbitrary", "arbitrary")),
        name="modulation",
    )(cvec, w_ada, b_ada.reshape(depth, 1, n))


_IN_COLS = {}
_col = 0
for _name, _width in (("aq", ATT_Q), ("ak", ATT_KV), ("av", ATT_KV), ("mq", M_W), ("mk", M_W), ("mv", M_W), ("mo", M_W),
                      ("mg", M_GATES), ("rx", LRU_WIDTH), ("ry", LRU_WIDTH)):
    _IN_COLS[_name] = (_col, _col + _width)
    _col += _width
_TOKEN_MAJOR = ("ak", "mq", "mk", "mv", "mo", "rx", "ry")
_FEATURE_MAJOR = ("aq", "av", "mg")


def _split_in_weights(w):
    tok = jnp.concatenate([w[:, _IN_COLS[n][0]:_IN_COLS[n][1]] for n in _TOKEN_MAJOR], axis=1)
    feat = jnp.concatenate([w[:, _IN_COLS[n][0]:_IN_COLS[n][1]] for n in _FEATURE_MAJOR], axis=1).T
    return tok.astype(BF16), feat.astype(BF16)


def _rope_slab(x, cs, sn, first):
    swapped = jnp.where(first, pltpu.roll(x, LANES - ROPE_PAIRS, 1), pltpu.roll(x, ROPE_PAIRS, 1))
    return x * cs + swapped * sn


class _TileWalk:
    def __init__(self, nb, per_row, first_tile=0):
        self.nb, self.per_row, self.first_tile, self.n_tiles = nb, per_row, first_tile, nb * per_row

    def at(self, k, lag):
        t = jnp.clip(k - lag, 0, self.n_tiles - 1)
        return t // self.per_row, t % self.per_row + self.first_tile

    def mod_row(self, bi):
        return jnp.where(bi[1] == 0, self.nb, bi[0]), 0, 0

    def stream_specs(self, streams, lag):
        d = streams[0].shape[-1]
        at = lambda k: self.at(k, lag)
        if len(streams) == 1:
            return [pl.BlockSpec((None, TOKEN_TILE, d), lambda k: (*at(k), 0))]
        return [pl.BlockSpec((None, TOKEN_TILE, d), lambda k: (at(k)[0], 0, 0)),
                pl.BlockSpec((None, TOKEN_TILE, d), lambda k: (at(k)[0], jnp.maximum(at(k)[1] - 1, 0), 0))]

    def stream_tile(self, stream_refs, lag):
        if len(stream_refs) == 1:
            return stream_refs[0][...]
        return jnp.where(self.at(pl.program_id(0), lag)[1] == 0, stream_refs[0][...], stream_refs[1][...])


def _inproj_kernel(n_streams, *refs):
    (mod_ref, g_ref, w_ref, wt_ref, cs_ref, sn_ref, cst_ref, snt_ref,
     ak_ref, mq_ref, mk_ref, mv_ref, mo_ref, rx_ref, ry_ref, aqt_ref, avt_ref, gr_ref) = refs[n_streams:]
    x = refs[0][...] if n_streams == 1 else jnp.where(pl.program_id(1) == 0, refs[0][...], refs[1][...])
    hb = (_rms(x, g_ref[0:1, :]) * (1.0 + mod_ref[1:2, :]) + mod_ref[0:1, :]).astype(BF16)
    pt = _dot_nt(wt_ref[...], hb)
    cst = cst_ref[...]
    snt = snt_ref[...]
    rp = ROPE_PAIRS
    for hd in range(ATT_HEADS):
        xs = pt[hd * HEAD_DIM:(hd + 1) * HEAD_DIM, :]
        swapped = jnp.concatenate([xs[rp:2 * rp], xs[0:rp], xs[3 * rp:4 * rp], xs[2 * rp:3 * rp]], axis=0)
        roped = (xs * cst + swapped * snt) * Q_SCALE
        for qb in range(TOKEN_TILE // ATT_BLOCK):
            aqt_ref[qb, hd * HEAD_DIM:(hd + 1) * HEAD_DIM, :] = roped[:, qb * ATT_BLOCK:(qb + 1) * ATT_BLOCK]
    for qb in range(TOKEN_TILE // ATT_BLOCK):
        avt_ref[qb] = pt[ATT_Q:ATT_Q + ATT_KV, qb * ATT_BLOCK:(qb + 1) * ATT_BLOCK]
    gr_ref[...] = pt[ATT_Q + ATT_KV:, :]
    lane = lax.broadcasted_iota(jnp.int32, (hb.shape[0], LANES), 1)
    first = (lane & (2 * ROPE_PAIRS - 1)) < ROPE_PAIRS
    ak_ref[...] = _rope_slab(_dot(hb, w_ref[:, 0:ATT_KV]), cs_ref[...], sn_ref[...], first)
    col = ATT_KV
    for ref in (mq_ref, mk_ref, mv_ref, mo_ref, rx_ref, ry_ref):
        n = ref.shape[-1]
        ref[...] = _dot(hb, w_ref[:, col:col + n])
        col += n


def _in_projection(xs, mod, gain, w_tok, w_feat, rope):
    streams = xs if isinstance(xs, tuple) else (xs,)
    nb = streams[0].shape[0]
    t = sum(s.shape[1] for s in streams)
    nt = t // TOKEN_TILE
    d = streams[0].shape[-1]
    tile = lambda n: pl.BlockSpec((None, TOKEN_TILE, n), lambda b, i: (b, i, 0))
    const = lambda a: pl.BlockSpec(a.shape, lambda b, i: (0,) * a.ndim)
    if len(streams) == 1:
        stream_specs = [tile(d)]
    else:
        stream_specs = [pl.BlockSpec((None, TOKEN_TILE, d), lambda b, i: (b, 0, 0)),
                        pl.BlockSpec((None, TOKEN_TILE, d), lambda b, i: (b, jnp.maximum(i - 1, 0), 0))]
    width = lambda n: _IN_COLS[n][1] - _IN_COLS[n][0]
    out_shapes = [jax.ShapeDtypeStruct((nb, t, width(n)), F32) for n in _TOKEN_MAJOR]
    per_tile = TOKEN_TILE // ATT_BLOCK
    out_shapes += [jax.ShapeDtypeStruct((nb, nt * per_tile, width(n), ATT_BLOCK), F32) for n in ("aq", "av")]
    out_shapes += [jax.ShapeDtypeStruct((nb, nt, M_GATES, TOKEN_TILE), F32)]
    out_specs = [tile(width(n)) for n in _TOKEN_MAJOR]
    out_specs += [pl.BlockSpec((None, per_tile, width(n), ATT_BLOCK), lambda b, i: (b, i, 0, 0)) for n in ("aq", "av")]
    out_specs += [pl.BlockSpec((None, None, M_GATES, TOKEN_TILE), lambda b, i: (b, i, 0, 0))]
    rope_cs, rope_sn, rope_cst, rope_snt = rope
    return pl.pallas_call(
        functools.partial(_inproj_kernel, len(streams)),
        grid=(nb, nt),
        in_specs=stream_specs + [
            pl.BlockSpec((None, N_MOD, d), lambda b, i: (jnp.where(i == 0, nb, b), 0, 0)),
            const(gain), const(w_tok), const(w_feat),
            pl.BlockSpec((TOKEN_TILE, LANES), lambda b, i: (i, 0)),
            pl.BlockSpec((TOKEN_TILE, LANES), lambda b, i: (i, 0)),
            pl.BlockSpec((HEAD_DIM, TOKEN_TILE), lambda b, i: (0, i)),
            pl.BlockSpec((HEAD_DIM, TOKEN_TILE), lambda b, i: (0, i)),
        ],
        out_specs=out_specs,
        out_shape=out_shapes,
        compiler_params=_params(("parallel", "arbitrary")),
        name="in_projection",
    )(*streams, mod, gain, w_tok, w_feat, rope_cs, rope_sn, rope_cst, rope_snt)


def _attn_block(j, nblk, sink_ref, qt_ref, kp_ref, kc_ref, kn_ref, kx_ref, vp_ref, vc_ref, vn_ref, vx_ref, o_ref):
    ctx_blocks = CTX_LEN // ATT_BLOCK
    nq = ATT_BLOCK
    cols = ATT_GROUP * nq
    c = lax.broadcasted_iota(jnp.int32, (ATT_BLOCK, cols), 0)
    r = lax.broadcasted_iota(jnp.int32, (ATT_BLOCK, cols), 1) & (nq - 1)
    m_prev = (c >= r) & (j >= ctx_blocks + 1)
    m_next = (c <= r) & (j >= ctx_blocks) & (j <= nblk - 2)
    is_lat = j >= ctx_blocks
    col = lax.broadcasted_iota(jnp.int32, (1, cols), 1)
    neg = -jnp.inf
    kp = kp_ref[...].astype(BF16)
    kc = kc_ref[...].astype(BF16)
    kn = kn_ref[...].astype(BF16)
    kx = kx_ref[...].astype(BF16)
    zeros = jnp.zeros((HEAD_DIM, cols), BF16)
    scores = []
    for kh in range(ATT_KV_HEADS):
        hd0 = kh * ATT_GROUP
        qg = jnp.concatenate([qt_ref[(hd0 + g) * HEAD_DIM:(hd0 + g + 1) * HEAD_DIM, :] for g in range(ATT_GROUP)],
                             axis=1).astype(BF16)
        rhs = jnp.concatenate([qg, zeros] if kh == 0 else [zeros, qg], axis=0)
        s_x = _dot(kx, rhs)
        scores.append((jnp.where(m_prev, _dot(kp, rhs), neg), jnp.where(is_lat, _dot(kc, rhs), neg),
                       jnp.where(m_next, _dot(kn, rhs), neg), s_x[:ATT_BLOCK], s_x[ATT_BLOCK:]))
    yield
    for kh in range(ATT_KV_HEADS):
        hd0 = kh * ATT_GROUP
        sink = jnp.full((1, cols), sink_ref[hd0], F32)
        for g in range(1, ATT_GROUP):
            sink = jnp.where(col >= g * nq, sink_ref[hd0 + g], sink)
        sink = sink * LOG2_E
        s_p, s_c, s_n, s_x0, s_x1 = scores[kh]
        m_el = jnp.maximum(jnp.maximum(jnp.maximum(s_p, s_c), jnp.maximum(s_n, s_x0)), s_x1)
        m = jnp.maximum(jnp.max(m_el, axis=0, keepdims=True), sink)
        p_p = jnp.exp2(s_p - m)
        p_c = jnp.exp2(s_c - m)
        p_n = jnp.exp2(s_n - m)
        p_x0 = jnp.exp2(s_x0 - m)
        p_x1 = jnp.exp2(s_x1 - m)
        vs = slice(kh * HEAD_DIM, (kh + 1) * HEAD_DIM)
        aug = lambda v: jnp.concatenate([v, jnp.ones((2 * SUBLANES, ATT_BLOCK), F32)], axis=0).astype(BF16)
        o = (_dot(aug(vp_ref[vs, :]), p_p.astype(BF16)) + _dot(aug(vc_ref[vs, :]), p_c.astype(BF16))
             + _dot(aug(vn_ref[vs, :]), p_n.astype(BF16))
             + _dot(aug(vx_ref[0, vs, :]), p_x0.astype(BF16)) + _dot(aug(vx_ref[1, vs, :]), p_x1.astype(BF16)))
        den = o[HEAD_DIM:HEAD_DIM + 1, :] + jnp.exp2(sink - m)
        o = o[:HEAD_DIM] / den
        for pair in range(ATT_GROUP // 2):
            two = jnp.concatenate([o[:, (2 * pair) * nq:(2 * pair + 1) * nq], o[:, (2 * pair + 1) * nq:(2 * pair + 2) * nq]],
                                  axis=0)
            lo = (hd0 + 2 * pair) * HEAD_DIM
            o_ref[:, lo:lo + 2 * HEAD_DIM] = two.T


def _seq_block(d, j, nblk):
    return jnp.where((d == 0) | (j == 0), j, nblk - j)


def _lane_scan(x, op, fill, rev):
    n = x.shape[-1]
    lane = lax.broadcasted_iota(jnp.int32, x.shape, 1)
    step = 1
    while step < n:
        if rev:
            shifted = jnp.where(lane < n - step, pltpu.roll(x, n - step, 1), fill)
        else:
            shifted = jnp.where(lane >= step, pltpu.roll(x, step, 1), fill)
        x = op(x, shifted)
        step *= 2
    return x


def _split3(x):
    hi = x.astype(BF16).astype(F32)
    mid = (x - hi).astype(BF16).astype(F32)
    lo = x - hi - mid
    return hi, mid, lo


def _mlstm_tables():
    H, dh = MLSTM_HEADS, MLSTM_HEAD_DIM
    n_pairs = H // 2
    n_slabs = H + 3 * n_pairs
    sel = np.zeros((LANES, n_slabs * LANES), np.float32)
    for part in range(3):
        base = part * 4 * H
        for h in range(H):
            sel[base + h, h * LANES:(h + 1) * LANES] = 1.0
        for qty in range(3):
            for p in range(n_pairs):
                slab = H + qty * n_pairs + p
                for half in range(2):
                    row = base + H * qty + 2 * p + half
                    sel[row, slab * LANES + half * dh:slab * LANES + (half + 1) * dh] = 1.0
    ones_bd = np.zeros((2 * TOKEN_TILE, LANES), np.float32)
    ones_bd[:TOKEN_TILE, :dh] = 1.0
    ones_bd[TOKEN_TILE:, dh:] = 1.0
    avg = np.zeros((3 * LANES, LANES), np.float32)
    for part in range(3):
        for half in range(2):
            avg[part * LANES + half * dh:part * LANES + (half + 1) * dh, half * dh:(half + 1) * dh] = 1.0 / dh
    return jnp.asarray(sel, BF16), jnp.asarray(ones_bd, BF16), jnp.asarray(avg, BF16)


def _mlstm_block_terms(rev, g, tri_ref):
    L = TOKEN_TILE
    H = MLSTM_HEADS
    ioff = 2 * H if rev else 0
    li = g[ioff:ioff + H, :]
    lf = -_softplus(-g[ioff + H:ioff + 2 * H, :])
    parts = _dot(jnp.concatenate(_split3(lf), axis=0).astype(BF16), tri_ref[...])
    b = parts[0:H] + parts[H:2 * H] + parts[2 * H:3 * H]
    r = li - b
    cm = _lane_scan(r, jnp.maximum, -jnp.inf, rev)
    last = 0 if rev else L - 1
    g_tot = b[:, last:last + 1]
    a = g_tot - b + li
    m_loc = jnp.max(a, axis=1, keepdims=True)
    w = jnp.exp(a - m_loc)
    return r, cm, b, w, g_tot, m_loc


def _mlstm_stash_terms(terms, sel_ref, cols_ref, r_ref, gm_ref, slot):
    L = TOKEN_TILE
    H, dh = MLSTM_HEADS, MLSTM_HEAD_DIM
    r, cm, b, w, g_tot, m_loc = terms
    stacked = jnp.concatenate([cm, b, w, jnp.zeros((H, L), F32)], axis=0)
    hi, mid, lo = _split3(stacked)
    pad = jnp.zeros((LANES - 3 * 4 * H, L), F32)
    cols_ref[slot] = _dot(jnp.concatenate([hi, mid, lo, pad], axis=0).T.astype(BF16), sel_ref[...])
    r_ref[slot] = jnp.concatenate([r, jnp.zeros((SUBLANES - H, L), F32)], axis=0)
    lane_row = lax.broadcasted_iota(jnp.int32, (1, LANES), 1)

    def layout(col):
        rows = [jnp.broadcast_to(col, (H, LANES))]
        rows += [jnp.where(lane_row < dh, col[2 * p:2 * p + 1, :], col[2 * p + 1:2 * p + 2, :]) for p in range(H // 2)]
        rows += [jnp.zeros((SUBLANES - H - H // 2, LANES), F32)]
        return jnp.concatenate(rows, axis=0)

    gm_ref[slot] = jnp.concatenate([layout(g_tot), layout(m_loc)], axis=0)


def _mlstm_kernel(q_ref, k_ref, v_ref, o_ref, gr_ref, gnext_ref, gb_ref, ng_ref, tri_ref, sel_ref, ones_ref, avg_ref, out_ref,
                  hf_ref, c_ref, n_ref, m_ref, cols_ref, r_ref, gm_ref, rev, init=False):
    d = pl.program_id(1)
    j = pl.program_id(2)
    nblk = pl.num_programs(2)
    L = TOKEN_TILE
    H, dh = MLSTM_HEADS, MLSTM_HEAD_DIM
    n_pairs = H // 2
    row0 = pl.multiple_of(_seq_block(d, j, nblk) * L, L)
    slot = j & 1

    if init:
        c_ref[...] = jnp.zeros_like(c_ref)
        n_ref[...] = jnp.zeros_like(n_ref)
        m_ref[...] = jnp.zeros_like(m_ref)
        _mlstm_stash_terms(_mlstm_block_terms(rev, gr_ref[...] + gb_ref[...], tri_ref), sel_ref, cols_ref, r_ref, gm_ref, 0)
        return

    def body(rev):
        next_terms = _mlstm_block_terms(rev, gnext_ref[...] + gb_ref[...], tri_ref)

        r = r_ref[slot]
        gm = gm_ref[slot]
        slab = lambda i: cols_ref[slot, :, i * LANES:(i + 1) * LANES]
        m0 = m_ref[...]
        g_tot, m_loc = gm[0:SUBLANES], gm[SUBLANES:]
        m_new = jnp.maximum(g_tot + m0, m_loc)
        f_prev = jnp.exp(g_tot + m0 - m_new)
        f_loc = jnp.exp(m_loc - m_new)
        m_ref[...] = m_new
        lane = lax.broadcasted_iota(jnp.int32, (L, LANES), 1)
        first_head = lane < dh
        sq = (LANES, LANES)
        tri_vis = ((lax.broadcasted_iota(jnp.int32, sq, 1) >= lax.broadcasted_iota(jnp.int32, sq, 0)) if rev
                   else (lax.broadcasted_iota(jnp.int32, sq, 1) <= lax.broadcasted_iota(jnp.int32, sq, 0)))
        kt = (k_ref[...] * (dh ** -0.5)).T.astype(BF16)
        zeros_half = jnp.zeros((dh, L), BF16)
        bd_mask = ((lax.broadcasted_iota(jnp.int32, (LANES, LANES), 0) < dh)
                   == (lax.broadcasted_iota(jnp.int32, (LANES, LANES), 1) < dh))
        for p in range(n_pairs):
            ps = slice(p * LANES, (p + 1) * LANES)
            qb = q_ref[:, ps].astype(BF16)
            v = v_ref[:, ps]
            kt_pair = kt[ps, :]
            sm = []
            for half in range(2):
                h = 2 * p + half
                kt_h = kt[h * dh:(h + 1) * dh, :]
                kt_masked = jnp.concatenate([kt_h, zeros_half] if half == 0 else [zeros_half, kt_h], axis=0)
                s = _dot(qb, kt_masked)
                m_b = jnp.maximum(slab(h), m0[h:h + 1, :])
                for lt in range(L // LANES):
                    quads = []
                    for qt in range(L // LANES):
                        rows = slice(qt * LANES, (qt + 1) * LANES)
                        if (lt < qt) if rev else (lt > qt):
                            quads.append(jnp.zeros((LANES, LANES), BF16))
                            continue
                        expo = r[h:h + 1, lt * LANES:(lt + 1) * LANES] - m_b[rows]
                        if lt == qt:
                            expo = jnp.where(tri_vis, expo, -jnp.inf)
                        quads.append((s[rows, lt * LANES:(lt + 1) * LANES] * jnp.exp(expo)).astype(BF16))
                    sm.append(jnp.concatenate(quads, axis=0))
            sm = jnp.concatenate(sm, axis=1)
            v_bd = jnp.concatenate([jnp.where(first_head, v, 0.0), jnp.where(first_head, 0.0, v)], axis=0).astype(BF16)
            m0_pair = m0[H + p:H + p + 1, :]
            m_pair = jnp.maximum(slab(H + p), m0_pair)
            f_pair = jnp.exp(m0_pair - m_pair)
            num = _dot(sm, v_bd) + f_pair * _dot(qb, c_ref[p].astype(BF16))
            den = _dot(sm, ones_ref[...]) + f_pair * _dot(qb, n_ref[p].astype(BF16))
            hh = num / jnp.maximum(jnp.abs(den), jnp.exp(-(slab(H + n_pairs + p) + m_pair)))
            w_pair = slab(H + 2 * n_pairs + p)
            c_loc = jnp.where(bd_mask, _dot(kt_pair, (w_pair * v).astype(BF16)), 0.0)
            n_loc = jnp.where(bd_mask, _dot(kt_pair, w_pair.astype(BF16)), 0.0)
            c_ref[p] = f_prev[H + p:H + p + 1, :] * c_ref[p] + f_loc[H + p:H + p + 1, :] * c_loc
            n_ref[p] = f_prev[H + p:H + p + 1, :] * n_ref[p] + f_loc[H + p:H + p + 1, :] * n_loc
            if not rev:
                hf_ref[pl.ds(row0, L), ps] = hh
            else:
                ht = hf_ref[pl.ds(row0, L), ps] + hh
                mu = _dot(jnp.concatenate(_split3(ht), axis=1).astype(BF16), avg_ref[...])
                cen = ht - mu
                var = _dot(jnp.concatenate(_split3(cen * cen), axis=1).astype(BF16), avg_ref[...])
                y = cen * lax.rsqrt(var + NORM_EPS) * ng_ref[:, ps]
                out_ref[:, ps] = _sigmoid(o_ref[:, ps]) * y
        _mlstm_stash_terms(next_terms, sel_ref, cols_ref, r_ref, gm_ref, 1 - slot)

    body(rev)


def _mlstm_specs(mq, mk, mv, mo, grow, gate_b, norm_g):
    nb, t, w = mq.shape
    nblk = t // TOKEN_TILE
    blk = lambda b, d, j: (b, _seq_block(d, j, nblk), 0)
    tile = pl.BlockSpec((None, TOKEN_TILE, w), blk)
    gb = gate_b.reshape(M_GATES, 1)
    ng = norm_g.reshape(1, w)
    sel, ones_bd, avg = _mlstm_tables()
    upper = np.triu(np.ones((TOKEN_TILE, TOKEN_TILE), np.float32))
    tri = jnp.asarray(np.stack([upper, upper.T]), BF16)
    const = lambda a: pl.BlockSpec(a.shape, lambda b, d, j: (0,) * a.ndim)
    n_pairs = MLSTM_HEADS // 2
    in_specs = [
        tile, tile, tile, tile,
        pl.BlockSpec((None, None, M_GATES, TOKEN_TILE), lambda b, d, j: (b, _seq_block(d, j, nblk), 0, 0)),
        pl.BlockSpec((None, None, M_GATES, TOKEN_TILE),
                     lambda b, d, j: (b, _seq_block(d, jnp.minimum(j + 1, nblk - 1), nblk), 0, 0)),
        const(gb), const(ng),
        pl.BlockSpec((None, TOKEN_TILE, TOKEN_TILE), lambda b, d, j: (d, 0, 0)),
        const(sel), const(ones_bd), const(avg),
    ]
    args = [mq, mk, mv, mo, grow, grow, gb, ng, tri, sel, ones_bd, avg]
    out_spec = pl.BlockSpec((None, TOKEN_TILE, w), lambda b, d, j: (b, jnp.where(d == 0, 0, _seq_block(d, j, nblk)), 0))
    scratch = [
        pltpu.VMEM((t, w), F32),
        pltpu.VMEM((n_pairs, LANES, LANES), F32),
        pltpu.VMEM((n_pairs, LANES, LANES), F32),
        pltpu.VMEM((SUBLANES, LANES), F32),
        pltpu.VMEM((2, TOKEN_TILE, sel.shape[1]), F32),
        pltpu.VMEM((2, SUBLANES, TOKEN_TILE), F32),
        pltpu.VMEM((2, 2 * SUBLANES, LANES), F32),
    ]
    return in_specs, args, out_spec, jax.ShapeDtypeStruct((nb, t, w), F32), scratch


def _lru_kernel(x_ref, xp_ref, xn_ref, y_ref, cw_ref, cb_ref, gw_ref, gb_ref, lam_ref, out_ref,
                hf_ref, xe_ref, carry_ref, a_ref, u_ref, hb_ref, hl_ref, al_ref, rev, init=False):
    d = pl.program_id(1)
    j = pl.program_id(2)
    nblk = pl.num_programs(2)
    L = TOKEN_TILE
    pos = _seq_block(d, j, nblk)
    row0 = pl.multiple_of(pos * L, L)

    if init:
        carry_ref[...] = jnp.zeros_like(carry_ref)
        return

    has_prev = pos >= 2
    has_next = (pos >= 1) & (pos <= nblk - 2)
    xe_ref[0:SUBLANES, :] = jnp.where(has_prev, xp_ref[...], 0.0)
    xe_ref[SUBLANES:SUBLANES + L, :] = x_ref[...]
    xe_ref[SUBLANES + L:, :] = jnp.where(has_next, xn_ref[...], 0.0)
    seq = cb_ref[...]
    for tap in range(CONV_WIDTH):
        off = SUBLANES - CONV_LEFT + tap
        seq = seq + cw_ref[tap:tap + 1, :] * xe_ref[off:off + L, :]
    sb = seq.astype(BF16)
    r = 0.5 + 0.5 * jnp.tanh(0.5 * (_dot(sb, gw_ref[0]) + gb_ref[0:1, :]))
    i = 0.5 + 0.5 * jnp.tanh(0.5 * (_dot(sb, gw_ref[1]) + gb_ref[1:2, :]))
    log_a = (-LRU_C * _softplus(-lam_ref[...])) * r
    a0 = jnp.exp(log_a)
    th = jnp.tanh(log_a)
    u0 = jnp.sqrt(-2.0 * th) * lax.rsqrt(1.0 - th) * (i * seq)
    n_lt = LRU_WIDTH // LANES
    for lt in range(n_lt):
        a_ref[lt] = a0[:, lt * LANES:(lt + 1) * LANES]
        u_ref[lt] = u0[:, lt * LANES:(lt + 1) * LANES]
    seg = L // (LRU_CHAINS * SUBLANES)
    sub = lax.broadcasted_iota(jnp.int32, (SUBLANES, LRU_WIDTH), 0)
    rows = lambda c, i: pl.ds(c * seg * SUBLANES + i, SUBLANES, stride=seg)
    strided = lambda ref, c, i: jnp.concatenate([ref[lt, rows(c, i), :] for lt in range(n_lt)], axis=1)

    def scan(rev):
        steps = range(seg - 1, -1, -1) if rev else range(seg)
        chains = range(LRU_CHAINS - 1, -1, -1) if rev else range(LRU_CHAINS)
        edge = SUBLANES - 1 if rev else 0
        shift = lambda x, n: pltpu.roll(x, SUBLANES - n if rev else n, 0)
        ends, prods = {}, {}
        for c in chains:
            h = jnp.zeros((SUBLANES, LRU_WIDTH), F32)
            ac = jnp.ones((SUBLANES, LRU_WIDTH), F32)
            for i in steps:
                ai = strided(a_ref, c, i)
                h = ai * h + strided(u_ref, c, i)
                ac = ai * ac
                hl_ref[c * seg + i] = h
                al_ref[c * seg + i] = ac
            step = 1
            while step < SUBLANES:
                ok = (sub < SUBLANES - step) if rev else (sub >= step)
                h = jnp.where(ok, ac * shift(h, step) + h, h)
                ac = jnp.where(ok, ac * shift(ac, step), ac)
                step *= 2
            ends[c], prods[c] = h, ac
        carry = carry_ref[...]
        for c in chains:
            true_ends = ends[c] + prods[c] * carry
            carry_in = jnp.where(sub == edge, carry, shift(true_ends, 1))
            carry = true_ends[SUBLANES - 1 - edge:SUBLANES - edge, :]
            for i in range(seg):
                hi = hl_ref[c * seg + i] + al_ref[c * seg + i] * carry_in
                for lt in range(n_lt):
                    hb_ref[lt, rows(c, i), :] = hi[:, lt * LANES:(lt + 1) * LANES]
        carry_ref[...] = carry
        return jnp.concatenate([hb_ref[lt] for lt in range(n_lt)], axis=1)

    def forward():
        hf_ref[pl.ds(row0, L), :] = scan(False)

    def backward():
        h = hf_ref[pl.ds(row0, L), :] + scan(True)
        y = y_ref[...]
        gelu = 0.5 * y * (1.0 + jnp.tanh(np.sqrt(2.0 / np.pi).astype(np.float32) * (y + 0.044715 * (y * y * y))))
        out_ref[...] = h * gelu

    backward() if rev else forward()


def _rglru_specs(rx, ry, conv_w, conv_b, gate_w, gate_b, lam):
    nb, t, w = rx.shape
    nblk = t // TOKEN_TILE
    per_tile = TOKEN_TILE // SUBLANES
    n8 = t // SUBLANES
    blk = lambda b, d, j: (b, _seq_block(d, j, nblk), 0)
    tile = pl.BlockSpec((None, TOKEN_TILE, w), blk)
    halo = lambda f: pl.BlockSpec((None, SUBLANES, w), lambda b, d, j: (b, f(_seq_block(d, j, nblk)), 0))
    prev8 = lambda p: jnp.maximum(p * per_tile - 1, 0)
    next8 = lambda p: jnp.minimum((p + 1) * per_tile, n8 - 1)
    eye = jnp.eye(LRU_BLOCKS, dtype=gate_w.dtype)
    gw = jnp.einsum('dgnij,nm->dgnimj', gate_w, eye).reshape(2, 2, w, w).astype(BF16)
    cb = conv_b.reshape(1, w)
    in_specs = [
        tile, halo(prev8), halo(next8), tile,
        pl.BlockSpec(conv_w.shape, lambda b, d, j: (0, 0)),
        pl.BlockSpec(cb.shape, lambda b, d, j: (0, 0)),
        pl.BlockSpec((None, 2, w, w), lambda b, d, j: (d, 0, 0, 0)),
        pl.BlockSpec((None, 2, w), lambda b, d, j: (d, 0, 0)),
        pl.BlockSpec((None, 1, w), lambda b, d, j: (d, 0, 0)),
    ]
    args = [rx, rx, rx, ry, conv_w, cb, gw, gate_b, lam.reshape(2, 1, w)]
    out_spec = pl.BlockSpec((None, TOKEN_TILE, w), lambda b, d, j: (b, jnp.where(d == 0, 0, _seq_block(d, j, nblk)), 0))
    scratch = [
        pltpu.VMEM((t, w), F32),
        pltpu.VMEM((TOKEN_TILE + 2 * SUBLANES, w), F32),
        pltpu.VMEM((1, w), F32),
        pltpu.VMEM((w // LANES, TOKEN_TILE, LANES), F32),
        pltpu.VMEM((w // LANES, TOKEN_TILE, LANES), F32),
        pltpu.VMEM((w // LANES, TOKEN_TILE, LANES), F32),
        pltpu.VMEM((TOKEN_TILE // SUBLANES, SUBLANES, w), F32),
        pltpu.VMEM((TOKEN_TILE // SUBLANES, SUBLANES, w), F32),
    ]
    return in_specs, args, out_spec, jax.ShapeDtypeStruct((nb, t, w), F32), scratch


def _mixers_kernel(n_mem, n_rec, n_mem_scratch, *refs):
    d = pl.program_id(1)
    j = pl.program_id(2)
    n_in = n_mem + n_rec
    mem_in, rec_in = refs[:n_mem], refs[n_mem:n_in]
    mem_out, rec_out = refs[n_in:n_in + 2]
    mem_scratch = refs[n_in + 2:n_in + 2 + n_mem_scratch]
    rec_scratch = refs[n_in + 2 + n_mem_scratch:]

    def body(rev):
        @pl.when(j == 0)
        def _():
            _mlstm_kernel(*mem_in, mem_out, *mem_scratch, rev=rev, init=True)
            _lru_kernel(*rec_in, rec_out, *rec_scratch, rev=rev, init=True)

        _mlstm_kernel(*mem_in, mem_out, *mem_scratch, rev=rev)
        _lru_kernel(*rec_in, rec_out, *rec_scratch, rev=rev)

    pl.when(d == 0)(functools.partial(body, False))
    pl.when(d == 1)(functools.partial(body, True))


def _mixers(mlstm_args, rglru_args):
    nb, t, _ = mlstm_args[0].shape
    mem_specs, mem_args, mem_out, mem_shape, mem_scratch = _mlstm_specs(*mlstm_args)
    rec_specs, rec_args, rec_out, rec_shape, rec_scratch = _rglru_specs(*rglru_args)
    return pl.pallas_call(
        functools.partial(_mixers_kernel, len(mem_args), len(rec_args), len(mem_scratch)),
        grid=(nb, 2, t // TOKEN_TILE),
        in_specs=mem_specs + rec_specs,
        out_specs=[mem_out, rec_out],
        out_shape=[mem_shape, rec_shape],
        scratch_shapes=mem_scratch + rec_scratch,
        compiler_params=_params(("parallel", "arbitrary", "arbitrary")),
        name="mixers",
    )(*mem_args, *rec_args)


def _outffn_kernel(n_streams, walk, n_att_blocks, *refs):
    k = pl.program_id(0)
    streams = refs[:n_streams]
    (sink_ref, qt_ref, kp_ref, kt_ref, kn_ref, kx_ref, vp_ref, vt_ref, vn_ref, vx_ref,
     mem_ref, rec_ref, mod_ref, modf_ref, g_ref, wo_ref, wi_ref, wd_ref, out_ref,
     act_ref, a0_ref, a1_ref, h0_ref, h1_ref, x0_ref, x1_ref) = refs[n_streams:]

    @pl.when(k == 0)
    def _():
        a1_ref[...] = jnp.zeros_like(a1_ref)
        h0_ref[...] = jnp.zeros_like(h0_ref)
        x0_ref[...] = jnp.zeros_like(x0_ref)

    def body(att_w, att_r, h_w, x_w, h_r, x_r):
        q0 = (TOKEN_TILE // ATT_BLOCK) * walk.at(k, 0)[1]
        lo, hi = pl.ds(0, ATT_BLOCK), pl.ds(ATT_BLOCK, ATT_BLOCK)
        att_blocks = [
            _attn_block(q0, n_att_blocks, sink_ref, qt_ref.at[0], kp_ref, kt_ref.at[lo], kt_ref.at[hi], kx_ref,
                        vp_ref, vt_ref.at[0], vt_ref.at[1], vx_ref, att_w.at[lo]),
            _attn_block(q0 + 1, n_att_blocks, sink_ref, qt_ref.at[1], kt_ref.at[lo], kt_ref.at[hi], kn_ref, kx_ref,
                        vt_ref.at[0], vt_ref.at[1], vn_ref, vx_ref, att_w.at[hi])]
        for blk in att_blocks:
            next(blk)
        x = walk.stream_tile(streams, 1)
        mix = (_dot(att_r[...].astype(BF16), wo_ref[0:ATT_Q, :])
               + _dot(mem_ref[...].astype(BF16), wo_ref[ATT_Q:ATT_Q + M_W, :])
               + _dot(rec_ref[...].astype(BF16), wo_ref[ATT_Q + M_W:, :]))
        x1 = x + mod_ref[2:3, :] * _rms(mix, g_ref[1:2, :])
        x_w[...] = x1
        h_w[...] = (_rms(x1, g_ref[2:3, :]) * (1.0 + mod_ref[4:5, :]) + mod_ref[3:4, :]).astype(BF16)
        h = h_r[...]
        for c0 in range(0, D_FF, FF_CHUNK):
            gate = _dot(h, wi_ref[:, c0:c0 + FF_CHUNK])
            up = _dot(h, wi_ref[:, D_FF + c0:D_FF + c0 + FF_CHUNK])
            half = 0.5 * gate
            act_ref[:, c0:c0 + FF_CHUNK] = ((half + half * jnp.tanh(half)) * up).astype(BF16)
        for blk in att_blocks:
            for _ in blk:
                pass
        f = _dot(act_ref[...], wd_ref[...])
        out_ref[...] = x_r[...] + modf_ref[5:6, :] * _rms(f, g_ref[3:4, :])

    pl.when((k & 1) == 0)(functools.partial(body, a0_ref, a1_ref, h1_ref, x1_ref, h0_ref, x0_ref))
    pl.when((k & 1) == 1)(functools.partial(body, a1_ref, a0_ref, h0_ref, x0_ref, h1_ref, x1_ref))


def _attn_out_ffn(xs, aqt, ak, avt, sink, mem, rec, mod, gain, w_out, w_ffn_in, w_down, first_tile=0):
    streams = xs if isinstance(xs, tuple) else (xs,)
    assert len(streams) == 1 or first_tile == 0
    nb, _, d = streams[0].shape
    t = ak.shape[1]
    per_row = t // TOKEN_TILE - first_tile
    walk = _TileWalk(nb, per_row, first_tile)
    n_att_blocks = t // ATT_BLOCK
    per_tile = TOKEN_TILE // ATT_BLOCK
    ctx_blocks = CTX_LEN // ATT_BLOCK
    s0 = lambda k: walk.at(k, 0)
    s1 = lambda k: walk.at(k, 1)
    s2 = lambda k: walk.at(k, 2)
    before = lambda k: jnp.maximum(per_tile * s0(k)[1] - 1, 0)
    after = lambda k: jnp.minimum(per_tile * (s0(k)[1] + 1), n_att_blocks - 1)
    tile1 = lambda n: pl.BlockSpec((None, TOKEN_TILE, n), lambda k: (*s1(k), 0))
    const = lambda a: pl.BlockSpec(a.shape, lambda k: (0,) * a.ndim, pipeline_mode=pl.Buffered(1))
    att_specs = [
        pl.BlockSpec(memory_space=pltpu.SMEM),
        pl.BlockSpec((None, per_tile, ATT_Q, ATT_BLOCK), lambda k: (*s0(k), 0, 0)),
        pl.BlockSpec((None, ATT_BLOCK, ATT_KV), lambda k: (s0(k)[0], before(k), 0)),
        pl.BlockSpec((None, TOKEN_TILE, ATT_KV), lambda k: (*s0(k), 0)),
        pl.BlockSpec((None, ATT_BLOCK, ATT_KV), lambda k: (s0(k)[0], after(k), 0)),
        pl.BlockSpec((None, CTX_LEN, ATT_KV), lambda k: (s0(k)[0], 0, 0)),
        pl.BlockSpec((None, None, ATT_KV, ATT_BLOCK), lambda k: (s0(k)[0], before(k), 0, 0)),
        pl.BlockSpec((None, per_tile, ATT_KV, ATT_BLOCK), lambda k: (*s0(k), 0, 0)),
        pl.BlockSpec((None, None, ATT_KV, ATT_BLOCK), lambda k: (s0(k)[0], after(k), 0, 0)),
        pl.BlockSpec((None, ctx_blocks, ATT_KV, ATT_BLOCK), lambda k: (s0(k)[0], 0, 0, 0)),
    ]
    return pl.pallas_call(
        functools.partial(_outffn_kernel, len(streams), walk, n_att_blocks),
        grid=(walk.n_tiles + 2,),
        in_specs=walk.stream_specs(streams, 1) + att_specs + [
            tile1(M_W), tile1(LRU_WIDTH),
            pl.BlockSpec((None, N_MOD, d), lambda k: walk.mod_row(s1(k))),
            pl.BlockSpec((None, N_MOD, d), lambda k: walk.mod_row(s2(k))),
            const(gain), const(w_out), const(w_ffn_in), const(w_down),
        ],
        out_specs=pl.BlockSpec((None, TOKEN_TILE, d), lambda k: (s2(k)[0], s2(k)[1] - first_tile, 0)),
        out_shape=jax.ShapeDtypeStruct((nb, per_row * TOKEN_TILE, d), F32),
        scratch_shapes=[pltpu.VMEM((TOKEN_TILE, D_FF), BF16),
                        pltpu.VMEM((TOKEN_TILE, ATT_Q), F32), pltpu.VMEM((TOKEN_TILE, ATT_Q), F32),
                        pltpu.VMEM((TOKEN_TILE, d), BF16), pltpu.VMEM((TOKEN_TILE, d), BF16),
                        pltpu.VMEM((TOKEN_TILE, d), F32), pltpu.VMEM((TOKEN_TILE, d), F32)],
        compiler_params=_params(("arbitrary",)),
        name="attn_out_ffn",
    )(*streams, sink, aqt, ak, ak, ak, ak, avt, avt, avt, avt, mem, rec, mod, mod, gain, w_out, w_ffn_in, w_down)


def _rope_tables(n_lat):
    t = jnp.arange(n_lat)
    row = (t // GRID_W).astype(F32)
    col = (t % GRID_W).astype(F32)
    freqs = ROPE_BASE ** (-jnp.arange(ROPE_PAIRS, dtype=F32) / ROPE_PAIRS)
    ang_r = row[:, None] * freqs
    ang_c = col[:, None] * freqs
    cs = jnp.concatenate([jnp.cos(ang_r), jnp.cos(ang_r), jnp.cos(ang_c), jnp.cos(ang_c)], axis=-1)
    sn = jnp.concatenate([-jnp.sin(ang_r), jnp.sin(ang_r), -jnp.sin(ang_c), jnp.sin(ang_c)], axis=-1)
    cs = jnp.concatenate([jnp.ones((CTX_LEN, HEAD_DIM), F32), cs], axis=0)
    sn = jnp.concatenate([jnp.zeros((CTX_LEN, HEAD_DIM), F32), sn], axis=0)
    return jnp.tile(cs, (1, LANES // HEAD_DIM)), jnp.tile(sn, (1, LANES // HEAD_DIM)), cs.T, sn.T


def kernel(x, c, ctx, c_ctx, w_ada, b_ada, norm_gain, w_in, w_out, attn_sink, mlstm_gate_b, mlstm_norm, conv_w, conv_b,
           lru_gate_w, lru_gate_b, lru_lam, w_ffn_in, w_ffn_out):
    nb, n_lat, d = x.shape
    depth = w_ada.shape[0]
    assert ctx.shape[1] == CTX_LEN and n_lat % TOKEN_TILE == 0 and nb < MOD_ROWS
    cvec = jnp.concatenate([c, c_ctx[None, :], jnp.zeros((MOD_ROWS - nb - 1, d), F32)], axis=0)
    mod = _modulation(cvec, w_ada, b_ada).reshape(depth, MOD_ROWS, N_MOD, d)
    rope = _rope_tables(n_lat)
    xs = (ctx, x) if depth > 1 else jnp.concatenate([ctx, x], axis=1)
    for l in range(depth):
        w_tok, w_feat = _split_in_weights(w_in[l])
        ak, mq, mk, mv, mo, rx, ry, aqt, avt, grow = _in_projection(xs, mod[l], norm_gain[l], w_tok, w_feat, rope)
        mem, rec = _mixers((mq, mk, mv, mo, grow, mlstm_gate_b[l], mlstm_norm[l]),
                           (rx, ry, conv_w[l], conv_b[l], lru_gate_w[l], lru_gate_b[l], lru_lam[l]))
        xs = _attn_out_ffn(xs, aqt, ak, avt, attn_sink[l], mem, rec, mod[l], norm_gain[l], w_out[l].astype(BF16),
                           w_ffn_in[l].astype(BF16), w_ffn_out[l].astype(BF16), first_tile=int(l == depth - 1))
    return xs
```
